```python
import jax
import jax.numpy as jnp
from jax import lax
import numpy as np

D_MODEL = 2048
BATCH = 32
SEQ = 256
DEPTH = 4
DEC_BATCH = 4
DEC_SEQ = 1024
PAST_LEN = 256

GRID_W = 64
N_AB = (DEPTH + 1) // 2
N_C = DEPTH // 2
N_MOD = 6
NORM_EPS = 1e-6

A_WIDTH = D_MODEL // 2
A_HEAD = 64
A_HEADS = A_WIDTH // A_HEAD
W_LORA = 64
ICL_LORA = 64
G_LORA = 160
GN_EPS = 64e-5
RWKV_COLS = 3 * A_WIDTH + G_LORA + 2 * W_LORA + 2 * ICL_LORA

B_HEADS = 8
NOPE = 128
ROPE_D = 64
V_HEAD = 128
Q_LORA = 512
KV_LORA = 512
B_WIDTH = B_HEADS * V_HEAD
ROPE_BASE = 10000.0
Q_BLOCK = 128
AB_COLS = RWKV_COLS + Q_LORA + KV_LORA + ROPE_D

C_HEADS = 4
C_QK = D_MODEL // 2
C_V = D_MODEL
C_DK = C_QK // C_HEADS
C_DV = C_V // C_HEADS
GATE_LORA = 16
GATE_TAU = 16.0
CHUNK = 64
C_COLS = 2 * C_QK + C_V + 2 * GATE_LORA + C_V

N_EXPERTS = 16
D_EXPERT = 2 * D_MODEL
CAP_FACTOR = 2

kernel_name = 'hybrid_rwkv7_mla_gla_ec_diffusion_step'


def rms_norm(x, g, eps=NORM_EPS):
    xf = x.astype(jnp.float32)
    y = xf * lax.rsqrt(jnp.mean(xf * xf, axis=-1, keepdims=True) + eps)
    return (y * g.astype(jnp.float32)).astype(x.dtype)


def modulation(cvec, w_mod, b_mod):
    m = jax.nn.silu(cvec) @ w_mod + b_mod
    return [t[:, None, :] for t in jnp.split(m, N_MOD, axis=-1)]


def token_shift(u, mu):
    prev = jnp.pad(u[:, :-1], ((0, 0), (1, 0), (0, 0)))
    nxt = jnp.pad(u[:, 1:], ((0, 0), (0, 1), (0, 0)))
    return u + mu[0] * (prev - u) + mu[1] * (nxt - u)


def axial_rope_tables(t):
    rows = t // GRID_W
    row = jnp.repeat(jnp.arange(rows, dtype=jnp.float32), GRID_W)
    col = jnp.tile(jnp.arange(GRID_W, dtype=jnp.float32), rows)
    half = ROPE_D // 2
    inv = 1.0 / (ROPE_BASE ** (jnp.arange(0, half, 2, dtype=jnp.float32) / half))
    ang_r = row[:, None] * inv[None, :]
    ang_c = col[:, None] * inv[None, :]
    return (jnp.cos(ang_r), jnp.sin(ang_r), jnp.cos(ang_c), jnp.sin(ang_c))


def _rotate(x, cos, sin):
    m = x.shape[-1] // 2
    x1, x2 = x[..., :m], x[..., m:]
    c = cos[None, :, None, :]
    s = sin[None, :, None, :]
    return jnp.concatenate([x1 * c - x2 * s, x1 * s + x2 * c], axis=-1)


def apply_axial_rope(x, tabs):
    cr, sr, cc, sc = tabs
    half = ROPE_D // 2
    xf = x.astype(jnp.float32)
    y = jnp.concatenate([_rotate(xf[..., :half], cr, sr), _rotate(xf[..., half:], cc, sc)], axis=-1)
    return y.astype(x.dtype)


def mla_attention(q_nope, q_rope, k_nope, k_rope, v):
    bsz, t, h, _ = q_nope.shape
    nb = t // Q_BLOCK
    scale = (NOPE + ROPE_D) ** -0.5
    qn = q_nope.reshape(bsz, nb, Q_BLOCK, h, NOPE).swapaxes(0, 1)
    qr = q_rope.reshape(bsz, nb, Q_BLOCK, h, ROPE_D).swapaxes(0, 1)

    def block(args):
        qn_b, qr_b = args
        s = jnp.einsum('bqhd,bkhd->bhqk', qn_b, k_nope) + jnp.einsum('bqhd,bkd->bhqk', qr_b, k_rope)
        p = jax.nn.softmax(s.astype(jnp.float32) * scale, axis=-1).astype(v.dtype)
        return jnp.einsum('bhqk,bkhd->bqhd', p, v)

    o = lax.map(block, (qn, qr))
    return o.swapaxes(0, 1).reshape(bsz, t, h * V_HEAD)


def rwkv_scan(r, w, k, v, kk, b, s0, reverse):
    def step(s, inp):
        r_t, w_t, k_t, v_t, kk_t, b_t = inp
        sa = jnp.einsum('bhvk,bhk->bhv', s, -kk_t)
        s = s * w_t[:, :, None, :] + sa[..., None] * b_t[:, :, None, :] + v_t[..., None] * k_t[:, :, None, :]
        return s, jnp.einsum('bhvk,bhk->bhv', s, r_t)

    xs = tuple(jnp.moveaxis(z, 1, 0) for z in (r, w, k, v, kk, b))
    s_fin, ys = lax.scan(step, s0, xs, reverse=reverse)
    return jnp.moveaxis(ys, 0, 1), s_fin


def rwkv_time_mix(u, p, s0):
    bsz, t, _ = u.shape
    f32 = jnp.float32

    def heads(z):
        return z.reshape(bsz, t, A_HEADS, A_HEAD)

    u = token_shift(u, p['mu']).astype(f32)
    r = heads(u[..., :A_WIDTH])
    k_raw = u[..., A_WIDTH:2 * A_WIDTH]
    v = heads(u[..., 2 * A_WIDTH:3 * A_WIDTH])
    o = 3 * A_WIDTH
    xg = u[..., o:o + G_LORA]
    xw = u[..., o + G_LORA:o + G_LORA + 2 * W_LORA].reshape(bsz, t, 2, W_LORA)
    xa = u[..., o + G_LORA + 2 * W_LORA:].reshape(bsz, t, 2, ICL_LORA)
    g = jax.nn.sigmoid(xg) @ p['g2']
    kk = heads(k_raw * p['k_k'])
    kk = kk * lax.rsqrt(jnp.maximum(jnp.sum(kk * kk, axis=-1, keepdims=True), 1e-24))
    y = jnp.zeros_like(v)
    bonus = jnp.zeros_like(v)
    states = []
    for d in range(2):
        w_log = -jax.nn.softplus(-(p['w0'][d] + jnp.tanh(xw[:, :, d]) @ p['w2'][d])) - 0.5
        w = jnp.exp(-jnp.exp(w_log))
        a = jax.nn.sigmoid(p['a0'][d] + xa[:, :, d] @ p['a2'][d])
        k = heads(k_raw * (1.0 + (a - 1.0) * p['k_a']))
        y_d, s_d = rwkv_scan(r, heads(w), k, v, kk, kk * heads(a), s0[:, d], reverse=(d == 1))
        y = y + y_d
        bonus = bonus + jnp.sum(r * k * p['r_k'], axis=-1, keepdims=True) * v
        states.append(s_d)
    mu = jnp.mean(y, axis=-1, keepdims=True)
    var = jnp.mean(jnp.square(y - mu), axis=-1, keepdims=True)
    gn_g = p['gn_g'].reshape(A_HEADS, A_HEAD)
    gn_b = p['gn_b'].reshape(A_HEADS, A_HEAD)
    yn = (y - mu) * lax.rsqrt(var + GN_EPS) * gn_g + gn_b
    out = (yn + bonus).reshape(bsz, t, A_WIDTH) * g
    return out, jnp.stack(states, axis=1)


def gla_chunk_scan(q, k, v, lg, s0):
    bsz, t, h, _ = q.shape
    dv = v.shape[-1]
    n = t // CHUNK

    def chunks(z):
        return z.reshape(bsz, n, CHUNK, h, z.shape[-1]).transpose(1, 0, 3, 2, 4)

    q, k, v, lg = chunks(q), chunks(k), chunks(v), chunks(lg)
    b = jnp.cumsum(lg, axis=3)
    bl = b[:, :, :, -1:, :]
    qt = q * jnp.exp(b)
    kt = k * jnp.exp(-b)
    kd = k * jnp.exp(bl - b)
    mask = jnp.tril(jnp.ones((CHUNK, CHUNK), dtype=bool))
    att = jnp.where(mask, jnp.einsum('nbhtk,nbhsk->nbhts', qt, kt), 0.0)
    o_intra = jnp.einsum('nbhts,nbhsv->nbhtv', att, v)

    def step(s, inp):
        qt_c, kd_c, v_c, bl_c = inp
        o_c = jnp.einsum('bhtk,bhkv->bhtv', qt_c, s)
        s = jnp.exp(bl_c[:, :, 0, :])[..., None] * s + jnp.einsum('bhsk,bhsv->bhkv', kd_c, v_c)
        return s, o_c

    s_fin, o_inter = lax.scan(step, s0, (qt, kd, v, bl))
    o = (o_intra + o_inter).transpose(1, 0, 3, 2, 4).reshape(bsz, t, h, dv)
    return o, s_fin


def gla_mix(u, p, s0):
    bsz, t, _ = u.shape
    u = u.astype(jnp.float32)
    q = u[..., :C_QK].reshape(bsz, t, C_HEADS, C_DK) * (C_DK ** -0.5)
    k = u[..., C_QK:2 * C_QK].reshape(bsz, t, C_HEADS, C_DK)
    v = u[..., 2 * C_QK:2 * C_QK + C_V].reshape(bsz, t, C_HEADS, C_DV)
    o = 2 * C_QK + C_V
    xg = u[..., o:o + 2 * GATE_LORA].reshape(bsz, t, 2, GATE_LORA)
    og = u[..., o + 2 * GATE_LORA:]

    def flip(z):
        return jnp.flip(z, axis=1)

    y = jnp.zeros_like(v)
    states = []
    for d in range(2):
        z = xg[:, :, d] @ p['gate_w2'][d] + p['gate_b'][d]
        lg = (jax.nn.log_sigmoid(z) / GATE_TAU).reshape(bsz, t, C_HEADS, C_DK)
        if d == 0:
            y_d, s_d = gla_chunk_scan(q, k, v, lg, s0[:, 0])
        else:
            y_d, s_d = gla_chunk_scan(flip(q), flip(k), flip(v), flip(lg), s0[:, 1])
            y_d = flip(y_d)
        y = y + y_d
        states.append(s_d)
    yn = y * lax.rsqrt(jnp.mean(y * y, axis=-1, keepdims=True) + NORM_EPS) * p['gn']
    out = yn.reshape(bsz, t, C_V) * jax.nn.silu(og)
    return out, jnp.stack(states, axis=1)


def mixer_ab(h, p, s0, ctx, rope_tabs):
    bsz, t, _ = h.shape
    u = h @ p['w_in']
    a_out, a_state = rwkv_time_mix(u[..., :RWKV_COLS], p, s0)
    o = RWKV_COLS
    cq = u[..., o:o + Q_LORA]
    ckv = rms_norm(u[..., o + Q_LORA:o + Q_LORA + KV_LORA], p['kv_norm'])
    kr = u[..., o + Q_LORA + KV_LORA:]
    q = (rms_norm(cq, p['q_norm']) @ p['w_uq']).reshape(bsz, t, B_HEADS, NOPE + ROPE_D)
    q_nope, q_rope = q[..., :NOPE], q[..., NOPE:]
    if rope_tabs is not None:
        q_rope = apply_axial_rope(q_rope, rope_tabs)
        kr = apply_axial_rope(kr[:, :, None, :], rope_tabs)[:, :, 0, :]
    keys_c, keys_r = ckv, kr
    if ctx is not None:
        keys_c = jnp.concatenate([ckv, ctx[0].astype(ckv.dtype)], axis=1)
        keys_r = jnp.concatenate([kr, ctx[1].astype(kr.dtype)], axis=1)
    kv = (keys_c @ p['w_ukv']).reshape(bsz, keys_c.shape[1], B_HEADS, NOPE + V_HEAD)
    b_out = mla_attention(q_nope, q_rope, kv[..., :NOPE], keys_r, kv[..., NOPE:])
    y = jnp.concatenate([a_out.astype(h.dtype), b_out.astype(h.dtype)], axis=-1) @ p['w_out']
    return y, a_state, ckv, kr


def mixer_c(h, p, s0):
    u = h @ p['w_in']
    o, states = gla_mix(u, p, s0)
    return o.astype(h.dtype) @ p['w_out'], states


def expert_choice_ffn(h, w_router, w_gate, w_up, w_down):
    bsz, t, d = h.shape
    cap = CAP_FACTOR * t // N_EXPERTS
    aff = jax.nn.softmax((h @ w_router).astype(jnp.float32), axis=-1)
    gate, idx = lax.top_k(aff.swapaxes(1, 2), cap)
    xs = jax.vmap(lambda hb, ib: hb[ib])(h, idx)
    hid = jax.nn.silu(jnp.einsum('becd,edf->becf', xs, w_gate)) * jnp.einsum('becd,edf->becf', xs, w_up)
    ys = jnp.einsum('becf,efd->becd', hid, w_down) * gate[..., None].astype(h.dtype)
    return jax.vmap(lambda ib, yb: jnp.zeros((t, d), h.dtype).at[ib.reshape(-1)].add(yb.reshape(-1, d)))(idx, ys)


def run_trunk(x, cvec, P, caches, rope_tabs):
    bsz = x.shape[0]
    is_ctx = caches is None
    ckvs, krs, s_rwkv, s_gla = [], [], [], []
    for l in range(DEPTH):
        sh1, sc1, g1, sh2, sc2, g2 = modulation(cvec, P['w_mod'][l], P['b_mod'][l])
        h = rms_norm(x, P['g_norm_mix'][l]) * (1 + sc1) + sh1
        i = l // 2
        if l % 2 == 0:
            p = dict(w_in=P['ab_w_in'][i], mu=P['rwkv_mu'][i], w0=P['rwkv_w0'][i], w2=P['rwkv_w2'][i],
                     a0=P['rwkv_a0'][i], a2=P['rwkv_a2'][i], g2=P['rwkv_g2'][i], k_k=P['rwkv_k_k'][i],
                     k_a=P['rwkv_k_a'][i], r_k=P['rwkv_r_k'][i], gn_g=P['rwkv_gn_g'][i], gn_b=P['rwkv_gn_b'][i],
                     q_norm=P['mla_q_norm'][i], w_uq=P['mla_w_uq'][i], kv_norm=P['mla_kv_norm'][i],
                     w_ukv=P['mla_w_ukv'][i], w_out=P['ab_w_out'][i])
            if is_ctx:
                s0 = jnp.zeros((bsz, 2, A_HEADS, A_HEAD, A_HEAD), jnp.float32)
                ctx = None
            else:
                s0 = caches[2][:, i].astype(jnp.float32)
                ctx = (caches[0][:, i], caches[1][:, i])
            y, st, ckv, kr = mixer_ab(h, p, s0, ctx, rope_tabs)
            if is_ctx:
                ckvs.append(ckv)
                krs.append(kr)
                s_rwkv.append(st.astype(x.dtype))
        else:
            p = dict(w_in=P['c_w_in'][i], gate_w2=P['gla_gate_w2'][i], gate_b=P['gla_gate_b'][i],
                     gn=P['gla_norm'][i], w_out=P['c_w_out'][i])
            if is_ctx:
                s0 = jnp.zeros((bsz, 2, C_HEADS, C_DK, C_DV), jnp.float32)
            else:
                s0 = caches[3][:, i].astype(jnp.float32)
            y, st = mixer_c(h, p, s0)
            if is_ctx:
                s_gla.append(st.astype(x.dtype))
        x = x + g1 * y
        h = rms_norm(x, P['g_norm_ffn'][l]) * (1 + sc2) + sh2
        x = x + g2 * expert_choice_ffn(h, P['moe_router'][l], P['moe_w_gate'][l], P['moe_w_up'][l], P['moe_w_down'][l])
    y = rms_norm(x, P['g_final'])
    if not is_ctx:
        return y, None
    return y, (jnp.stack(ckvs, axis=1), jnp.stack(krs, axis=1), jnp.stack(s_rwkv, axis=1), jnp.stack(s_gla, axis=1))


def setup_inputs(seed: int = 0) -> dict:
    key = jax.random.key(seed)
    keys = jax.random.split(key, 48)
    counter = [0]

    def nrm(shape, scale=1.0):
        kk = keys[counter[0]]
        counter[0] += 1
        return jax.random.normal(kk, shape, jnp.float32) * scale

    def gain(shape):
        return 1.0 + nrm(shape, 0.02)

    return {
        'x_prompt': nrm((BATCH, SEQ, D_MODEL)),
        'x_sample': nrm((DEC_BATCH, DEC_SEQ, D_MODEL)),
        'cache_mla_ckv': nrm((DEC_BATCH, N_AB, PAST_LEN, KV_LORA)),
        'cache_mla_krope': nrm((DEC_BATCH, N_AB, PAST_LEN, ROPE_D)),
        'state_rwkv': nrm((DEC_BATCH, N_AB, 2, A_HEADS, A_HEAD, A_HEAD), 0.5),
        'state_gla': nrm((DEC_BATCH, N_C, 2, C_HEADS, C_DK, C_DV), 0.5),
        'c': nrm((DEC_BATCH, D_MODEL)),
        'c_ctx': nrm((D_MODEL,)),
        'w_mod': nrm((DEPTH, D_MODEL, N_MOD * D_MODEL), 0.5 * D_MODEL ** -0.5),
        'b_mod': nrm((DEPTH, N_MOD * D_MODEL), 0.02),
        'g_norm_mix': gain((DEPTH, D_MODEL)),
        'g_norm_ffn': gain((DEPTH, D_MODEL)),
        'ab_w_in': nrm((N_AB, D_MODEL, AB_COLS), D_MODEL ** -0.5),
        'rwkv_mu': 0.3 + nrm((N_AB, 2, RWKV_COLS), 0.1),
        'rwkv_w0': nrm((N_AB, 2, A_WIDTH), 0.5),
        'rwkv_w2': nrm((N_AB, 2, W_LORA, A_WIDTH), W_LORA ** -0.5),
        'rwkv_a0': nrm((N_AB, 2, A_WIDTH), 0.1),
        'rwkv_a2': nrm((N_AB, 2, ICL_LORA, A_WIDTH), ICL_LORA ** -0.5),
        'rwkv_g2': nrm((N_AB, G_LORA, A_WIDTH), G_LORA ** -0.5),
        'rwkv_k_k': 1.0 + nrm((N_AB, A_WIDTH), 0.1),
        'rwkv_k_a': 1.0 + nrm((N_AB, A_WIDTH), 0.1),
        'rwkv_r_k': nrm((N_AB, A_HEADS, A_HEAD), 0.1),
        'rwkv_gn_g': gain((N_AB, A_WIDTH)),
        'rwkv_gn_b': nrm((N_AB, A_WIDTH), 0.02),
        'mla_q_norm': gain((N_AB, Q_LORA)),
        'mla_w_uq': nrm((N_AB, Q_LORA, B_HEADS * (NOPE + ROPE_D)), Q_LORA ** -0.5),
        'mla_kv_norm': gain((N_AB, KV_LORA)),
        'mla_w_ukv': nrm((N_AB, KV_LORA, B_HEADS * (NOPE + V_HEAD)), KV_LORA ** -0.5),
        'ab_w_out': nrm((N_AB, A_WIDTH + B_WIDTH, D_MODEL), (A_WIDTH + B_WIDTH) ** -0.5),
        'c_w_in': nrm((N_C, D_MODEL, C_COLS), D_MODEL ** -0.5),
        'gla_gate_w2': nrm((N_C, 2, GATE_LORA, C_QK), GATE_LORA ** -0.5),
        'gla_gate_b': nrm((N_C, 2, C_QK), 0.1),
        'gla_norm': gain((N_C, C_DV)),
        'c_w_out': nrm((N_C, C_V, D_MODEL), C_V ** -0.5),
        'moe_router': nrm((DEPTH, D_MODEL, N_EXPERTS), D_MODEL ** -0.5),
        'moe_w_gate': nrm((DEPTH, N_EXPERTS, D_MODEL, D_EXPERT), D_MODEL ** -0.5),
        'moe_w_up': nrm((DEPTH, N_EXPERTS, D_MODEL, D_EXPERT), D_MODEL ** -0.5),
        'moe_w_down': nrm((DEPTH, N_EXPERTS, D_EXPERT, D_MODEL), D_EXPERT ** -0.5),
        'g_final': gain((D_MODEL,)),
    }


def reference(x_prompt, x_sample, cache_mla_ckv, cache_mla_krope, state_rwkv, state_gla, c, c_ctx,
              w_mod, b_mod, g_norm_mix, g_norm_ffn,
              ab_w_in, rwkv_mu, rwkv_w0, rwkv_w2, rwkv_a0, rwkv_a2, rwkv_g2, rwkv_k_k, rwkv_k_a, rwkv_r_k,
              rwkv_gn_g, rwkv_gn_b, mla_q_norm, mla_w_uq, mla_kv_norm, mla_w_ukv, ab_w_out,
              c_w_in, gla_gate_w2, gla_gate_b, gla_norm, c_w_out,
              moe_router, moe_w_gate, moe_w_up, moe_w_down, g_final):
    P = dict(w_mod=w_mod, b_mod=b_mod, g_norm_mix=g_norm_mix, g_norm_ffn=g_norm_ffn,
             ab_w_in=ab_w_in, rwkv_mu=rwkv_mu, rwkv_w0=rwkv_w0, rwkv_w2=rwkv_w2, rwkv_a0=rwkv_a0,
             rwkv_a2=rwkv_a2, rwkv_g2=rwkv_g2, rwkv_k_k=rwkv_k_k, rwkv_k_a=rwkv_k_a, rwkv_r_k=rwkv_r_k,
             rwkv_gn_g=rwkv_gn_g, rwkv_gn_b=rwkv_gn_b, mla_q_norm=mla_q_norm, mla_w_uq=mla_w_uq,
             mla_kv_norm=mla_kv_norm, mla_w_ukv=mla_w_ukv, ab_w_out=ab_w_out,
             c_w_in=c_w_in, gla_gate_w2=gla_gate_w2, gla_gate_b=gla_gate_b, gla_norm=gla_norm, c_w_out=c_w_out,
             moe_router=moe_router, moe_w_gate=moe_w_gate, moe_w_up=moe_w_up, moe_w_down=moe_w_down,
             g_final=g_final)
    y_prompt, ctx_out = run_trunk(x_prompt, c_ctx[None, :], P, None, None)
    new_ckv, new_krope, new_rwkv, new_gla = ctx_out
    rope_tabs = axial_rope_tables(x_sample.shape[1])
    y_sample, _ = run_trunk(x_sample, c, P, (cache_mla_ckv, cache_mla_krope, state_rwkv, state_gla), rope_tabs)
    return (y_prompt, y_sample, new_ckv, new_krope, new_rwkv, new_gla)
```

```python
import functools

import jax
import jax.numpy as jnp
from jax import lax
from jax.experimental import pallas as pl
from jax.experimental.pallas import tpu as pltpu

F32 = jnp.float32
BF16 = jnp.bfloat16
HIGHEST = lax.Precision.HIGHEST

VMEM_LIMIT_BYTES = 56 * 1024 * 1024
LANES = 128

D_MODEL = 2048
DEPTH = 4
GRID_W = 64
N_MOD = 6
NORM_EPS = 1e-6

A_WIDTH = 1024
A_HEAD = 64
A_HEADS = 16
W_LORA = 64
ICL_LORA = 64
G_LORA = 160
G_LORA_PAD = 256
GN_EPS = 64e-5

B_HEADS = 8
NOPE = 128
ROPE_D = 64
V_HEAD = 128
Q_LORA = 512
KV_LORA = 512
ROPE_BASE = 10000.0

C_HEADS = 4
C_QK = 1024
C_V = 2048
C_DK = 256
C_DV = 512
GATE_LORA = 16
GATE_TAU = 16.0
CHUNK = 64

N_EXPERTS = 16
D_EXPERT = 4096
CAP_FACTOR = 2

AB_PAD_COLS = 5120
AB_XG_BLK = 3072 // G_LORA_PAD
AB_XW_BLK = 3328 // LANES
AB_XA_BLK = 3456 // LANES
AB_CQ_BLK = 3584 // Q_LORA
AB_CKV_BLK = 4096 // KV_LORA
AB_KR_COL = 4608
C_PAD_COLS = 6656
C_XG_BLK = 6144 // LANES

TM = 1024


def _cparams(sem):
    return pltpu.CompilerParams(dimension_semantics=sem, vmem_limit_bytes=VMEM_LIMIT_BYTES)


def _sigmoid(x):
    return 1.0 / (1.0 + jnp.exp(-x))


def _softplus(x):
    return jnp.maximum(x, 0.0) + jnp.log(1.0 + jnp.exp(-jnp.abs(x)))


def _silu(x):
    return x * _sigmoid(x)


def _mod_kernel(c_ref, w_ref, b_ref, o_ref):
    a = _silu(c_ref[...])
    o_ref[...] = jnp.dot(a, w_ref[...], preferred_element_type=F32, precision=HIGHEST) + b_ref[...]


def modulation_all(cvecs, w_mod, b_mod):
    tn = 1024
    n = w_mod.shape[-1]
    return pl.pallas_call(
        _mod_kernel,
        grid=(DEPTH, n // tn),
        in_specs=[
            pl.BlockSpec((8, D_MODEL), lambda l, j: (0, 0)),
            pl.BlockSpec((None, D_MODEL, tn), lambda l, j: (l, 0, j)),
            pl.BlockSpec((None, 1, tn), lambda l, j: (l, 0, j)),
        ],
        out_specs=pl.BlockSpec((None, 8, tn), lambda l, j: (l, 0, j)),
        out_shape=jax.ShapeDtypeStruct((DEPTH, 8, n), F32),
        compiler_params=_cparams(("parallel", "parallel")),
        name="modulation",
    )(cvecs, w_mod, b_mod.reshape(DEPTH, 1, n))


def _norm_mod(x, g, sc, sh):
    ms = jnp.mean(x * x, axis=-1, keepdims=True)
    y = x * lax.rsqrt(ms + NORM_EPS) * g
    return y * (1.0 + sc) + sh


def _nm_mm_kernel(x_ref, g_ref, sc_ref, sh_ref, w_ref, *rest, shift_t):
    if shift_t:
        mu_ref, o_ref, h_scr = rest
    else:
        o_ref, h_scr = rest

    @pl.when(pl.program_id(1) == 0)
    def _():
        h_scr[...] = _norm_mod(x_ref[...], g_ref[...], sc_ref[...], sh_ref[...]).astype(BF16)

    acc = jnp.dot(h_scr[...], w_ref[...].astype(BF16), preferred_element_type=F32)
    if shift_t:
        tm = acc.shape[0]
        row = lax.broadcasted_iota(jnp.int32, acc.shape, 0) & (shift_t - 1)
        prev = jnp.where(row == 0, 0.0, pltpu.roll(acc, 1, 0))
        nxt = jnp.where(row == shift_t - 1, 0.0, pltpu.roll(acc, tm - 1, 0))
        acc = acc + mu_ref[0:1, :] * (prev - acc) + mu_ref[1:2, :] * (nxt - acc)
    o_ref[...] = acc.astype(o_ref.dtype)


def nm_matmul(x, g, sc, sh, w, *, rows_per_mod, tn, mu=None, shift_t=0):
    m, d = x.shape
    n = w.shape[1]
    tpm = rows_per_mod // TM
    in_specs = [
        pl.BlockSpec((TM, d), lambda i, j: (i, 0)),
        pl.BlockSpec((1, d), lambda i, j: (0, 0)),
        pl.BlockSpec((None, 1, d), lambda i, j: (i // tpm, 0, 0)),
        pl.BlockSpec((None, 1, d), lambda i, j: (i // tpm, 0, 0)),
        pl.BlockSpec((d, tn), lambda i, j: (0, j)),
    ]
    args = [x, g.reshape(1, d), sc, sh, w]
    if shift_t:
        in_specs.append(pl.BlockSpec((2, tn), lambda i, j: (0, j)))
        args.append(mu)
    return pl.pallas_call(
        functools.partial(_nm_mm_kernel, shift_t=shift_t),
        grid=(m // TM, n // tn),
        in_specs=in_specs,
        out_specs=pl.BlockSpec((TM, tn), lambda i, j: (i, j)),
        out_shape=jax.ShapeDtypeStruct((m, n), F32),
        scratch_shapes=[pltpu.VMEM((TM, d), BF16)],
        compiler_params=_cparams(("parallel", "arbitrary")),
        name="norm_mod_matmul",
    )(*args)


def _rms_mm_kernel(x_ref, g_ref, w_ref, *out_refs, normalize, emit_norm):
    x = x_ref[...]
    if normalize:
        ms = jnp.mean(x * x, axis=-1, keepdims=True)
        x = x * lax.rsqrt(ms + NORM_EPS) * g_ref[...]
    if emit_norm:
        out_refs[1][...] = x
    out_refs[0][...] = jnp.dot(x.astype(BF16), w_ref[...].astype(BF16),
                               preferred_element_type=F32).astype(out_refs[0].dtype)


def rms_matmul(x, col_blk, k, g, w, *, normalize, emit_norm, out_dtype):
    m = x.shape[0]
    n = w.shape[1]
    tm = min(TM, m)
    out_shape = [jax.ShapeDtypeStruct((m, n), out_dtype)]
    out_specs = [pl.BlockSpec((tm, n), lambda i: (i, 0))]
    if emit_norm:
        out_shape.append(jax.ShapeDtypeStruct((m, k), F32))
        out_specs.append(pl.BlockSpec((tm, k), lambda i: (i, 0)))
    res = pl.pallas_call(
        functools.partial(_rms_mm_kernel, normalize=normalize, emit_norm=emit_norm),
        grid=(m // tm,),
        in_specs=[
            pl.BlockSpec((tm, k), lambda i: (i, col_blk)),
            pl.BlockSpec((1, k), lambda i: (0, 0)),
            pl.BlockSpec((k, n), lambda i: (0, 0)),
        ],
        out_specs=out_specs,
        out_shape=out_shape,
        compiler_params=_cparams(("parallel",)),
        name="rms_matmul",
    )(x, g.reshape(1, k), w)
    return res


def _res_mm_kernel(*refs, n_x):
    x_refs = refs[:n_x]
    w_refs = refs[n_x:2 * n_x]
    res_ref, gate_ref, o_ref = refs[2 * n_x:]
    acc = jnp.dot(x_refs[0][...], w_refs[0][...].astype(BF16), preferred_element_type=F32)
    for xr, wr in zip(x_refs[1:], w_refs[1:]):
        acc = acc + jnp.dot(xr[...], wr[...].astype(BF16), preferred_element_type=F32)
    o_ref[...] = res_ref[...] + gate_ref[...] * acc


def res_matmul(xs, w, res, gate, *, rows_per_mod, tn=512):
    m, n = res.shape
    tpm = rows_per_mod // TM
    in_specs, k0 = [], 0
    for xx in xs:
        in_specs.append(pl.BlockSpec((TM, xx.shape[1]), lambda i, j: (i, 0)))
    for xx in xs:
        kx = xx.shape[1]
        blk = k0 // kx
        in_specs.append(pl.BlockSpec((kx, tn), lambda i, j, blk=blk: (blk, j)))
        k0 += kx
    in_specs.append(pl.BlockSpec((TM, tn), lambda i, j: (i, j)))
    in_specs.append(pl.BlockSpec((None, 1, tn), lambda i, j: (i // tpm, 0, j)))
    return pl.pallas_call(
        functools.partial(_res_mm_kernel, n_x=len(xs)),
        grid=(m // TM, n // tn),
        in_specs=in_specs,
        out_specs=pl.BlockSpec((TM, tn), lambda i, j: (i, j)),
        out_shape=jax.ShapeDtypeStruct((m, n), F32),
        compiler_params=_cparams(("parallel", "parallel")),
        name="residual_matmul",
    )(*xs, *([w] * len(xs)), res, gate)


def _final_norm_kernel(x_ref, g_ref, o_ref):
    x = x_ref[...]
    ms = jnp.mean(x * x, axis=-1, keepdims=True)
    o_ref[...] = x * lax.rsqrt(ms + NORM_EPS) * g_ref[...]


def final_norm(x, g):
    m, d = x.shape
    return pl.pallas_call(
        _final_norm_kernel,
        grid=(m // TM,),
        in_specs=[pl.BlockSpec((TM, d), lambda i: (i, 0)), pl.BlockSpec((1, d), lambda i: (0, 0))],
        out_specs=pl.BlockSpec((TM, d), lambda i: (i, 0)),
        out_shape=jax.ShapeDtypeStruct((m, d), F32),
        compiler_params=_cparams(("parallel",)),
        name="final_norm",
    )(x, g.reshape(1, d))


def _head_pair_ones():
    r = lax.broadcasted_iota(jnp.int32, (LANES, LANES), 0) // A_HEAD
    c = lax.broadcasted_iota(jnp.int32, (LANES, LANES), 1) // A_HEAD
    return (r == c).astype(F32)


def _head_pair_eye():
    r = lax.broadcasted_iota(jnp.int32, (A_HEAD, LANES), 0)
    c = lax.broadcasted_iota(jnp.int32, (A_HEAD, LANES), 1) & (A_HEAD - 1)
    return (r == c).astype(F32)


def _rwkv_kernel(r_ref, k_ref, v_ref, xw_ref, xa_ref, xg_ref,
                 w2_ref, w0_ref, a2_ref, a0_ref, g2_ref, kk_ref, ka_ref, rk_ref, gng_ref, gnb_ref,
                 *rest, has_s0):
    if has_s0:
        s0_ref, o_ref, sfin_ref, a_s, w_s, b_s, k_s, y_s, bonus_s, st_s = rest
    else:
        o_ref, sfin_ref, a_s, w_s, b_s, k_s, y_s, bonus_s, st_s = rest
    n_g, t_len, _ = r_ref.shape
    ones_bd = _head_pair_ones()
    ones_bd16 = ones_bd.astype(BF16)
    eye2 = _head_pair_eye()

    def head_sum(z):
        return jnp.dot(z, ones_bd, preferred_element_type=F32, precision=HIGHEST)

    for g in range(n_g):
        r = r_ref[g]
        kraw = k_ref[g]
        v = v_ref[g]
        kk = kraw * kk_ref[...]
        kk = kk * lax.rsqrt(jnp.maximum(head_sum(kk * kk), 1e-24))
        a_s[g] = -kk
        bonus = jnp.zeros_like(v)
        for d in range(2):
            xw = xw_ref[g][:, d * W_LORA:(d + 1) * W_LORA]
            xa = xa_ref[g][:, d * ICL_LORA:(d + 1) * ICL_LORA]
            wl = w0_ref[d] + jnp.dot(jnp.tanh(xw), w2_ref[d], preferred_element_type=F32, precision=HIGHEST)
            w_log = -_softplus(-wl) - 0.5
            w_s[d, g] = jnp.exp(-jnp.exp(w_log))
            a = _sigmoid(a0_ref[d] + jnp.dot(xa, a2_ref[d], preferred_element_type=F32, precision=HIGHEST))
            kd = kraw * (1.0 + (a - 1.0) * ka_ref[...])
            k_s[d, g] = kd
            b_s[d, g] = kk * a
            bonus = bonus + head_sum(r * kd * rk_ref[...]) * v
        bonus_s[g] = bonus

    for d in range(2):
        for g in range(n_g):
            if has_s0:
                st_s[d, g] = jnp.concatenate([s0_ref[g, d, 0], s0_ref[g, d, 1]], axis=1)
            else:
                st_s[d, g] = jnp.zeros((A_HEAD, LANES), F32)

    def step(t, carry):
        for d in range(2):
            tt = t if d == 0 else t_len - 1 - t
            for g in range(n_g):
                s = st_s[d, g]
                row = pl.ds(tt, 1)
                a_t = a_s[g, row, :]
                w_t = w_s[d, g, row, :]
                b_t = b_s[d, g, row, :]
                k_t = k_s[d, g, row, :]
                v_t = v_ref[g, row, :]
                r_t = r_ref[g, row, :]
                sa = jnp.dot((s * a_t).astype(BF16), ones_bd16, preferred_element_type=F32)
                vb = jnp.dot((eye2 * v_t).astype(BF16), ones_bd16, preferred_element_type=F32)
                s = s * w_t + sa * b_t + vb * k_t
                st_s[d, g] = s
                yb = jnp.dot((s * r_t).astype(BF16), ones_bd16, preferred_element_type=F32)
                y_s[d, g, row, :] = jnp.sum(yb * eye2, axis=0, keepdims=True)
        return carry

    lax.fori_loop(0, t_len, step, 0)

    for d in range(2):
        for g in range(n_g):
            s = st_s[d, g]
            sfin_ref[g, d, 0] = s[:, :A_HEAD]
            sfin_ref[g, d, 1] = s[:, A_HEAD:]

    inv_n = 1.0 / A_HEAD
    for g in range(n_g):
        y = y_s[0, g] + y_s[1, g]
        mu = head_sum(y) * inv_n
        yc = y - mu
        var = head_sum(yc * yc) * inv_n
        yn = yc * lax.rsqrt(var + GN_EPS) * gng_ref[...] + gnb_ref[...]
        gate = jnp.dot(_sigmoid(xg_ref[g]).astype(BF16), g2_ref[...].astype(BF16), preferred_element_type=F32)
        o_ref[g] = ((yn + bonus_s[g]) * gate).astype(o_ref.dtype)


def rwkv_mix(u, p, s0, layer_idx, n_g):
    bsz, t_len, _ = u.shape
    hp_n = A_HEADS // 2

    def col(blk_of_hp):
        return pl.BlockSpec((n_g, t_len, LANES), lambda bi, hp: (bi, 0, blk_of_hp(hp)))

    def vec(n_rows):
        return pl.BlockSpec((n_rows, LANES), lambda bi, hp: (0, hp))

    in_specs = [
        col(lambda hp: hp), col(lambda hp: hp_n + hp), col(lambda hp: 2 * hp_n + hp),
        col(lambda hp: AB_XW_BLK), col(lambda hp: AB_XA_BLK),
        pl.BlockSpec((n_g, t_len, G_LORA_PAD), lambda bi, hp: (bi, 0, AB_XG_BLK)),
        pl.BlockSpec((2, W_LORA, LANES), lambda bi, hp: (0, 0, hp)),
        pl.BlockSpec((2, 1, LANES), lambda bi, hp: (0, 0, hp)),
        pl.BlockSpec((2, ICL_LORA, LANES), lambda bi, hp: (0, 0, hp)),
        pl.BlockSpec((2, 1, LANES), lambda bi, hp: (0, 0, hp)),
        pl.BlockSpec((G_LORA_PAD, LANES), lambda bi, hp: (0, hp)),
        vec(1), vec(1), vec(1), vec(1), vec(1),
    ]
    args = [u, u, u, u, u, u, p['w2'], p['w0'].reshape(2, 1, A_WIDTH), p['a2'], p['a0'].reshape(2, 1, A_WIDTH),
            p['g2'], p['k_k'].reshape(1, A_WIDTH), p['k_a'].reshape(1, A_WIDTH), p['r_k'].reshape(1, A_WIDTH),
            p['gn_g'].reshape(1, A_WIDTH), p['gn_b'].reshape(1, A_WIDTH)]
    if s0 is not None:
        in_specs.append(pl.BlockSpec((n_g, None, 2, 2, A_HEAD, A_HEAD),
                                     lambda bi, hp: (bi, layer_idx, 0, hp, 0, 0)))
        args.append(s0)
    blk = (n_g, t_len, LANES)
    out, sfin = pl.pallas_call(
        functools.partial(_rwkv_kernel, has_s0=s0 is not None),
        grid=(bsz // n_g, hp_n),
        in_specs=in_specs,
        out_specs=[
            pl.BlockSpec((n_g, t_len, LANES), lambda bi, hp: (bi, 0, hp)),
            pl.BlockSpec((n_g, 2, 2, A_HEAD, A_HEAD), lambda bi, hp: (bi, 0, hp, 0, 0)),
        ],
        out_shape=[
            jax.ShapeDtypeStruct((bsz, t_len, A_WIDTH), BF16),
            jax.ShapeDtypeStruct((bsz, 2, A_HEADS, A_HEAD, A_HEAD), F32),
        ],
        scratch_shapes=[
            pltpu.VMEM(blk, F32),
            pltpu.VMEM((2,) + blk, F32),
            pltpu.VMEM((2,) + blk, F32),
            pltpu.VMEM((2,) + blk, F32),
            pltpu.VMEM((2,) + blk, F32),
            pltpu.VMEM(blk, F32),
            pltpu.VMEM((2, n_g, A_HEAD, LANES), F32),
        ],
        compiler_params=_cparams(("parallel", "parallel")),
        name="rwkv7_mix",
    )(*args)
    return out, sfin


def _mla_kernel(qn_ref, qr_ref, kn_ref, kr_ref, v_ref, o_ref):
    scale = (NOPE + ROPE_D) ** -0.5
    krb = kr_ref[...].astype(BF16)
    nt = (((1,), (1,)), ((), ()))
    for h in range(B_HEADS):
        qn = qn_ref[:, h * NOPE:(h + 1) * NOPE].astype(BF16)
        qr = qr_ref[:, h * ROPE_D:(h + 1) * ROPE_D].astype(BF16)
        s = lax.dot_general(qn, kn_ref[:, h * NOPE:(h + 1) * NOPE], nt, preferred_element_type=F32)
        s = s + lax.dot_general(qr, krb, nt, preferred_element_type=F32)
        s = s * scale
        m = jnp.max(s, axis=-1, keepdims=True)
        e = jnp.exp(s - m)
        p = e / jnp.sum(e, axis=-1, keepdims=True)
        o = jnp.dot(p.astype(BF16), v_ref[:, h * V_HEAD:(h + 1) * V_HEAD], preferred_element_type=F32)
        o_ref[:, h * V_HEAD:(h + 1) * V_HEAD] = o.astype(o_ref.dtype)


def mla_attention(q, kv, kr, *, tq):
    bsz, t_len, _ = q.shape
    s_len = kv.shape[1]
    hn = B_HEADS * NOPE
    return pl.pallas_call(
        _mla_kernel,
        grid=(bsz, t_len // tq),
        in_specs=[
            pl.BlockSpec((None, tq, hn), lambda b, i: (b, i, 0)),
            pl.BlockSpec((None, tq, B_HEADS * ROPE_D), lambda b, i: (b, i, hn // (B_HEADS * ROPE_D))),
            pl.BlockSpec((None, s_len, hn), lambda b, i: (b, 0, 0)),
            pl.BlockSpec((None, s_len, ROPE_D), lambda b, i: (b, 0, 0)),
            pl.BlockSpec((None, s_len, hn), lambda b, i: (b, 0, 1)),
        ],
        out_specs=pl.BlockSpec((None, tq, hn), lambda b, i: (b, i, 0)),
        out_shape=jax.ShapeDtypeStruct((bsz, t_len, hn), BF16),
        compiler_params=_cparams(("parallel", "parallel")),
        name="mla_attention",
    )(q, q, kv, kr, kv)


def _gla_kernel(q_ref, k_ref, v_ref, og_ref, xg_ref, gw_ref, gb_ref, gn_ref, *rest, has_s0):
    if has_s0:
        s0_ref, o_ref, sfin_ref, lg_s, y_s, st_s = rest
    else:
        o_ref, sfin_ref, lg_s, y_s, st_s = rest
    t_len = q_ref.shape[0]
    n_chunks = t_len // CHUNK
    scale = C_DK ** -0.5
    nt = (((1,), (1,)), ((), ()))
    xg = xg_ref[...]
    for d in range(2):
        z = jnp.dot(xg[:, d * GATE_LORA:(d + 1) * GATE_LORA], gw_ref[d], preferred_element_type=F32,
                    precision=HIGHEST) + gb_ref[d]
        lg_s[d] = -_softplus(-z) * (1.0 / GATE_TAU)
    ri = lax.broadcasted_iota(jnp.int32, (CHUNK, CHUNK), 0)
    ci = lax.broadcasted_iota(jnp.int32, (CHUNK, CHUNK), 1)
    for d in range(2):
        keep = (ri >= ci) if d == 0 else (ri <= ci)
        tri = keep.astype(F32)
        if has_s0:
            st_s[...] = s0_ref[d].T
        else:
            st_s[...] = jnp.zeros(st_s.shape, F32)

        def chunk(i, carry, d=d, keep=keep, tri=tri):
            c = i if d == 0 else n_chunks - 1 - i
            rows = pl.ds(pl.multiple_of(c * CHUNK, CHUNK), CHUNK)
            q = q_ref[rows, :] * scale
            k = k_ref[rows, :]
            v = v_ref[rows, :]
            b = jnp.dot(tri, lg_s[d, rows, :], preferred_element_type=F32, precision=HIGHEST)
            bl = b[CHUNK - 1:CHUNK, :] if d == 0 else b[0:1, :]
            qt = (q * jnp.exp(b)).astype(BF16)
            kt = (k * jnp.exp(-b)).astype(BF16)
            kd = (k * jnp.exp(bl - b)).astype(BF16)
            att = jnp.where(keep, lax.dot_general(qt, kt, nt, preferred_element_type=F32), 0.0)
            st = st_s[...]
            o = jnp.dot(att.astype(BF16), v.astype(BF16), preferred_element_type=F32)
            o = o + lax.dot_general(qt, st.astype(BF16), nt, preferred_element_type=F32)
            st_s[...] = st * jnp.exp(bl) + jnp.dot(v.T.astype(BF16), kd, preferred_element_type=F32)
            if d == 0:
                y_s[rows, :] = o
            else:
                y_s[rows, :] = y_s[rows, :] + o
            return carry

        lax.fori_loop(0, n_chunks, chunk, 0)
        sfin_ref[d] = st_s[...].T
    y = y_s[...]
    yn = y * lax.rsqrt(jnp.mean(y * y, axis=-1, keepdims=True) + NORM_EPS) * gn_ref[...]
    o_ref[...] = (yn * _silu(og_ref[...])).astype(o_ref.dtype)


def gla_mix(u, p, s0, layer_idx):
    bsz, t_len, _ = u.shape
    in_specs = [
        pl.BlockSpec((None, t_len, C_DK), lambda b, h: (b, 0, h)),
        pl.BlockSpec((None, t_len, C_DK), lambda b, h: (b, 0, C_HEADS + h)),
        pl.BlockSpec((None, t_len, C_DV), lambda b, h: (b, 0, C_HEADS + h)),
        pl.BlockSpec((None, t_len, C_DV), lambda b, h: (b, 0, 2 * C_HEADS + h)),
        pl.BlockSpec((None, t_len, LANES), lambda b, h: (b, 0, C_XG_BLK)),
        pl.BlockSpec((2, GATE_LORA, C_DK), lambda b, h: (0, 0, h)),
        pl.BlockSpec((2, 1, C_DK), lambda b, h: (0, 0, h)),
        pl.BlockSpec((1, C_DV), lambda b, h: (0, 0)),
    ]
    args = [u, u, u, u, u, p['gate_w2'], p['gate_b'].reshape(2, 1, C_QK), p['gn'].reshape(1, C_DV)]
    if s0 is not None:
        in_specs.append(pl.BlockSpec((None, None, 2, None, C_DK, C_DV), lambda b, h: (b, layer_idx, 0, h, 0, 0)))
        args.append(s0)
    return pl.pallas_call(
        functools.partial(_gla_kernel, has_s0=s0 is not None),
        grid=(bsz, C_HEADS),
        in_specs=in_specs,
        out_specs=[
            pl.BlockSpec((None, t_len, C_DV), lambda b, h: (b, 0, h)),
            pl.BlockSpec((None, 2, None, C_DK, C_DV), lambda b, h: (b, 0, h, 0, 0)),
        ],
        out_shape=[
            jax.ShapeDtypeStruct((bsz, t_len, C_V), BF16),
            jax.ShapeDtypeStruct((bsz, 2, C_HEADS, C_DK, C_DV), F32),
        ],
        scratch_shapes=[
            pltpu.VMEM((2, t_len, C_DK), F32),
            pltpu.VMEM((t_len, C_DV), F32),
            pltpu.VMEM((C_DV, C_DK), F32),
        ],
        compiler_params=_cparams(("parallel", "parallel")),
        name="gla_mix",
    )(*args)


def _dispatch_kernel(x_ref, g_ref, sc_ref, sh_ref, wr_ref, xe_ref, rank_ref, aff_ref, *, cap):
    t_len = x_ref.shape[0]
    h = _norm_mod(x_ref[...], g_ref[...], sc_ref[...], sh_ref[...])
    hb = h.astype(BF16)
    logits = jnp.dot(h, wr_ref[...], preferred_element_type=F32, precision=HIGHEST)
    lane = lax.broadcasted_iota(jnp.int32, logits.shape, 1)
    logits = jnp.where(lane < N_EXPERTS, logits, -jnp.inf)
    m = jnp.max(logits, axis=-1, keepdims=True)
    e = jnp.exp(logits - m)
    aff = e / jnp.sum(e, axis=-1, keepdims=True)
    aff_ref[...] = aff
    aff_t = aff.T
    sub_i = lax.broadcasted_iota(jnp.int32, (t_len, t_len), 0)
    lane_i = lax.broadcasted_iota(jnp.int32, (t_len, t_len), 1)
    slot = lax.broadcasted_iota(jnp.int32, (cap, t_len), 0).astype(F32)
    rank_cols = jnp.zeros((t_len, LANES), F32)
    lane128 = lax.broadcasted_iota(jnp.int32, (t_len, LANES), 1)
    for ex in range(N_EXPERTS):
        col = aff[:, ex:ex + 1]
        row = aff_t[ex:ex + 1, :]
        ahead = (col > row) | ((col == row) & (sub_i < lane_i))
        rank_row = jnp.sum(ahead.astype(F32), axis=0, keepdims=True)
        onehot = (rank_row == slot).astype(BF16)
        xe_ref[ex] = jnp.dot(onehot, hb, preferred_element_type=F32).astype(BF16)
        ahead_t = (row > col) | ((row == col) & (lane_i < sub_i))
        rank_col = jnp.sum(ahead_t.astype(F32), axis=1, keepdims=True)
        rank_cols = jnp.where(lane128 == ex, rank_col, rank_cols)
    rank_ref[...] = rank_cols


def moe_dispatch(x, g, sc, sh, w_router_pad, cap):
    bsz, t_len, d = x.shape
    per_req = sc.shape[0] > 1
    mod_spec = pl.BlockSpec((None, 1, d), (lambda b: (b, 0, 0)) if per_req else (lambda b: (0, 0, 0)))
    return pl.pallas_call(
        functools.partial(_dispatch_kernel, cap=cap),
        grid=(bsz,),
        in_specs=[
            pl.BlockSpec((None, t_len, d), lambda b: (b, 0, 0)),
            pl.BlockSpec((1, d), lambda b: (0, 0)),
            mod_spec, mod_spec,
            pl.BlockSpec((d, LANES), lambda b: (0, 0)),
        ],
        out_specs=[
            pl.BlockSpec((N_EXPERTS, cap, d), lambda b: (0, b, 0)),
            pl.BlockSpec((None, t_len, LANES), lambda b: (b, 0, 0)),
            pl.BlockSpec((None, t_len, LANES), lambda b: (b, 0, 0)),
        ],
        out_shape=[
            jax.ShapeDtypeStruct((N_EXPERTS, bsz * cap, d), BF16),
            jax.ShapeDtypeStruct((bsz, t_len, LANES), F32),
            jax.ShapeDtypeStruct((bsz, t_len, LANES), F32),
        ],
        compiler_params=_cparams(("parallel",)),
        name="moe_dispatch",
    )(x, g.reshape(1, d), sc, sh, w_router_pad)


EXPERT_TF = 256
EXPERT_TN = 256
N_UP_STEPS = D_EXPERT // EXPERT_TF
N_DOWN_STEPS = D_MODEL // EXPERT_TN


def _experts_kernel(xc_ref, xl_ref, wg_ref, wu_ref, wd_ref, yc_ref, yl_ref, hid_s):
    s = pl.program_id(1)
    mc = xc_ref.shape[0]

    @pl.when(s < N_UP_STEPS)
    def _():
        wg = wg_ref[...].astype(BF16)
        wu = wu_ref[...].astype(BF16)
        cols = pl.ds(pl.multiple_of(s * EXPERT_TF, EXPERT_TF), EXPERT_TF)
        for x_ref, r0 in ((xc_ref, 0), (xl_ref, mc)):
            x = x_ref[...]
            hg = jnp.dot(x, wg, preferred_element_type=F32)
            hu = jnp.dot(x, wu, preferred_element_type=F32)
            hid_s[r0:r0 + x.shape[0], cols] = (_silu(hg) * hu).astype(BF16)

    @pl.when(s >= N_UP_STEPS)
    def _():
        wd = wd_ref[...].astype(BF16)
        yc_ref[...] = jnp.dot(hid_s[0:mc, :], wd, preferred_element_type=F32).astype(yc_ref.dtype)
        yl_ref[...] = jnp.dot(hid_s[mc:, :], wd, preferred_element_type=F32).astype(yl_ref.dtype)


def moe_experts(xc, xl, w_gate, w_up, w_down, layer):
    mc, ml = xc.shape[1], xl.shape[1]
    up_idx = lambda e, s: (layer, e, 0, jnp.minimum(s, N_UP_STEPS - 1))
    down_idx = lambda e, s: (layer, e, 0, jnp.maximum(s - N_UP_STEPS, 0))
    out_idx = lambda e, s: (e, 0, jnp.maximum(s - N_UP_STEPS, 0))
    return pl.pallas_call(
        _experts_kernel,
        grid=(N_EXPERTS, N_UP_STEPS + N_DOWN_STEPS),
        in_specs=[
            pl.BlockSpec((None, mc, D_MODEL), lambda e, s: (e, 0, 0)),
            pl.BlockSpec((None, ml, D_MODEL), lambda e, s: (e, 0, 0)),
            pl.BlockSpec((None, None, D_MODEL, EXPERT_TF), up_idx),
            pl.BlockSpec((None, None, D_MODEL, EXPERT_TF), up_idx),
            pl.BlockSpec((None, None, D_EXPERT, EXPERT_TN), down_idx),
        ],
        out_specs=[
            pl.BlockSpec((None, mc, EXPERT_TN), out_idx),
            pl.BlockSpec((None, ml, EXPERT_TN), out_idx),
        ],
        out_shape=[
            jax.ShapeDtypeStruct((N_EXPERTS, mc, D_MODEL), BF16),
            jax.ShapeDtypeStruct((N_EXPERTS, ml, D_MODEL), BF16),
        ],
        scratch_shapes=[pltpu.VMEM((mc + ml, D_EXPERT), BF16)],
        compiler_params=_cparams(("parallel", "arbitrary")),
        name="moe_experts",
    )(xc, xl, w_gate, w_up, w_down)


def _combine_kernel(x_ref, gate_ref, ye_ref, rank_ref, aff_ref, o_ref, *, cap):
    t_len = x_ref.shape[0]
    slot = lax.broadcasted_iota(jnp.int32, (t_len, cap), 1).astype(F32)
    acc = jnp.zeros(x_ref.shape, F32)
    for ex in range(N_EXPERTS):
        onehot = (rank_ref[:, ex:ex + 1] == slot).astype(BF16)
        acc = acc + aff_ref[:, ex:ex + 1] * jnp.dot(onehot, ye_ref[ex], preferred_element_type=F32)
    o_ref[...] = x_ref[...] + gate_ref[...] * acc


def moe_combine(x, gate, ye, rank, aff, cap):
    bsz, t_len, d = x.shape
    per_req = gate.shape[0] > 1
    tn = 512
    return pl.pallas_call(
        functools.partial(_combine_kernel, cap=cap),
        grid=(bsz, d // tn),
        in_specs=[
            pl.BlockSpec((None, t_len, tn), lambda b, j: (b, 0, j)),
            pl.BlockSpec((None, 1, tn), (lambda b, j: (b, 0, j)) if per_req else (lambda b, j: (0, 0, j))),
            pl.BlockSpec((N_EXPERTS, cap, tn), lambda b, j: (0, b, j)),
            pl.BlockSpec((None, t_len, LANES), lambda b, j: (b, 0, 0)),
            pl.BlockSpec((None, t_len, LANES), lambda b, j: (b, 0, 0)),
        ],
        out_specs=pl.BlockSpec((None, t_len, tn), lambda b, j: (b, 0, j)),
        out_shape=jax.ShapeDtypeStruct((bsz, t_len, d), F32),
        compiler_params=_cparams(("parallel", "parallel")),
        name="moe_combine",
    )(x, gate, ye, rank, aff)


def _pad_cols(w, n):
    return jnp.pad(w, ((0, 0),) * (w.ndim - 1) + ((0, n - w.shape[-1]),))


def _ab_layout(w):
    rkv = w[..., :3 * A_WIDTH]
    o = 3 * A_WIDTH
    xg = _pad_cols(w[..., o:o + G_LORA], G_LORA_PAD)
    o += G_LORA
    xw = w[..., o:o + 2 * W_LORA]
    o += 2 * W_LORA
    xa = w[..., o:o + 2 * ICL_LORA]
    o += 2 * ICL_LORA
    rest = w[..., o:]
    return _pad_cols(jnp.concatenate([rkv, xg, xw, xa, rest], axis=-1), AB_PAD_COLS)


def _c_layout(w):
    qkv = w[:, :2 * C_QK + C_V]
    o = 2 * C_QK + C_V
    xg = _pad_cols(w[:, o:o + 2 * GATE_LORA], LANES)
    og = w[:, o + 2 * GATE_LORA:]
    return _pad_cols(jnp.concatenate([qkv, og, xg], axis=-1), C_PAD_COLS)


def _axial_rope_tables(t):
    rows = t // GRID_W
    row = jnp.repeat(jnp.arange(rows, dtype=F32), GRID_W)
    col = jnp.tile(jnp.arange(GRID_W, dtype=F32), rows)
    half = ROPE_D // 2
    inv = 1.0 / (ROPE_BASE ** (jnp.arange(0, half, 2, dtype=F32) / half))
    ang_r = row[:, None] * inv[None, :]
    ang_c = col[:, None] * inv[None, :]
    return (jnp.cos(ang_r), jnp.sin(ang_r), jnp.cos(ang_c), jnp.sin(ang_c))


def _rotate(x, cos, sin):
    m = x.shape[-1] // 2
    x1, x2 = x[..., :m], x[..., m:]
    c = cos[None, :, None, :]
    s = sin[None, :, None, :]
    return jnp.concatenate([x1 * c - x2 * s, x1 * s + x2 * c], axis=-1)


def _apply_axial_rope(x, tabs):
    cr, sr, cc, sc = tabs
    half = ROPE_D // 2
    return jnp.concatenate([_rotate(x[..., :half], cr, sr), _rotate(x[..., half:], cc, sc)], axis=-1)


def _mixer_layer(x2, l, mod_l, P, caches, rope_tabs, bsz, t_len, outs):
    m = bsz * t_len
    is_ctx = caches is None
    rows_per_mod = m if is_ctx else t_len
    sh1, sc1, g1 = mod_l[0], mod_l[1], mod_l[2]
    i = l // 2
    if l % 2 == 0:
        u = nm_matmul(x2, P['g_norm_mix'][l], sc1, sh1, P['ab_w_in_pad'][i], rows_per_mod=rows_per_mod,
                      tn=512, mu=P['rwkv_mu_pad'][i], shift_t=t_len)
        p = dict(w2=P['rwkv_w2'][i], w0=P['rwkv_w0'][i], a2=P['rwkv_a2'][i], a0=P['rwkv_a0'][i],
                 g2=P['rwkv_g2_pad'][i], k_k=P['rwkv_k_k'][i], k_a=P['rwkv_k_a'][i], r_k=P['rwkv_r_k'][i],
                 gn_g=P['rwkv_gn_g'][i], gn_b=P['rwkv_gn_b'][i])
        u3 = u.reshape(bsz, t_len, AB_PAD_COLS)
        a_out, a_state = rwkv_mix(u3, p, None if is_ctx else caches[2], i, n_g=8 if is_ctx else 2)
        (q,) = rms_matmul(u, AB_CQ_BLK, Q_LORA, P['mla_q_norm'][i], P['mla_w_uq_r'][i],
                          normalize=True, emit_norm=False, out_dtype=F32)
        kv, ckv = rms_matmul(u, AB_CKV_BLK, KV_LORA, P['mla_kv_norm'][i], P['mla_w_ukv_r'][i],
                             normalize=True, emit_norm=True, out_dtype=BF16)
        kr = u3[:, :, AB_KR_COL:AB_KR_COL + ROPE_D]
        q = q.reshape(bsz, t_len, -1)
        kv = kv.reshape(bsz, t_len, -1)
        if is_ctx:
            outs['ckv'].append(ckv.reshape(bsz, t_len, KV_LORA))
            outs['kr'].append(kr)
            outs['rwkv'].append(a_state)
            keys_r = kr
        else:
            hn = B_HEADS * NOPE
            q_rope = _apply_axial_rope(q[..., hn:].reshape(bsz, t_len, B_HEADS, ROPE_D), rope_tabs)
            q = jnp.concatenate([q[..., :hn], q_rope.reshape(bsz, t_len, B_HEADS * ROPE_D)], axis=-1)
            kr = _apply_axial_rope(kr[:, :, None, :], rope_tabs)[:, :, 0, :]
            past = caches[0].shape[2]
            (kv_ctx,) = rms_matmul(caches[0][:, i].reshape(bsz * past, KV_LORA), 0, KV_LORA,
                                   P['mla_kv_norm'][i], P['mla_w_ukv_r'][i],
                                   normalize=False, emit_norm=False, out_dtype=BF16)
            kv = jnp.concatenate([kv, kv_ctx.reshape(bsz, past, -1)], axis=1)
            keys_r = jnp.concatenate([kr, caches[1][:, i]], axis=1)
        b_out = mla_attention(q, kv, keys_r, tq=256)
        return res_matmul([a_out.reshape(m, A_WIDTH), b_out.reshape(m, A_WIDTH)], P['ab_w_out'][i], x2, g1,
                          rows_per_mod=rows_per_mod)
    u = nm_matmul(x2, P['g_norm_mix'][l], sc1, sh1, P['c_w_in_pad'][i], rows_per_mod=rows_per_mod, tn=512)
    p = dict(gate_w2=P['gla_gate_w2'][i], gate_b=P['gla_gate_b'][i], gn=P['gla_norm'][i])
    o, st = gla_mix(u.reshape(bsz, t_len, C_PAD_COLS), p, None if is_ctx else caches[3], i)
    if is_ctx:
        outs['gla'].append(st)
    return res_matmul([o.reshape(m, C_V)], P['c_w_out'][i], x2, g1, rows_per_mod=rows_per_mod)


def kernel(x_prompt, x_sample, cache_mla_ckv, cache_mla_krope, state_rwkv, state_gla, c, c_ctx, w_mod, b_mod, g_norm_mix, g_norm_ffn, ab_w_in, rwkv_mu, rwkv_w0, rwkv_w2, rwkv_a0, rwkv_a2, rwkv_g2, rwkv_k_k, rwkv_k_a, rwkv_r_k, rwkv_gn_g, rwkv_gn_b, mla_q_norm, mla_w_uq, mla_kv_norm, mla_w_ukv, ab_w_out, c_w_in, gla_gate_w2, gla_gate_b, gla_norm, c_w_out, moe_router, moe_w_gate, moe_w_up, moe_w_down, g_final):
    n_ab = ab_w_in.shape[0]
    n_c = c_w_in.shape[0]
    w_uq = mla_w_uq.reshape(n_ab, Q_LORA, B_HEADS, NOPE + ROPE_D)
    w_ukv = mla_w_ukv.reshape(n_ab, KV_LORA, B_HEADS, NOPE + V_HEAD)
    P = dict(
        g_norm_mix=g_norm_mix, g_norm_ffn=g_norm_ffn, g_final=g_final,
        ab_w_in_pad=_ab_layout(ab_w_in), rwkv_mu_pad=_ab_layout(rwkv_mu),
        rwkv_w0=rwkv_w0, rwkv_w2=rwkv_w2, rwkv_a0=rwkv_a0, rwkv_a2=rwkv_a2,
        rwkv_g2_pad=jnp.pad(rwkv_g2, ((0, 0), (0, G_LORA_PAD - G_LORA), (0, 0))),
        rwkv_k_k=rwkv_k_k, rwkv_k_a=rwkv_k_a, rwkv_r_k=rwkv_r_k.reshape(n_ab, A_WIDTH),
        rwkv_gn_g=rwkv_gn_g, rwkv_gn_b=rwkv_gn_b,
        mla_q_norm=mla_q_norm, mla_kv_norm=mla_kv_norm,
        mla_w_uq_r=jnp.concatenate([w_uq[..., :NOPE].reshape(n_ab, Q_LORA, -1),
                                    w_uq[..., NOPE:].reshape(n_ab, Q_LORA, -1)], axis=-1),
        mla_w_ukv_r=jnp.concatenate([w_ukv[..., :NOPE].reshape(n_ab, KV_LORA, -1),
                                     w_ukv[..., NOPE:].reshape(n_ab, KV_LORA, -1)], axis=-1),
        ab_w_out=ab_w_out,
        c_w_in_pad=jnp.stack([_c_layout(c_w_in[i]) for i in range(n_c)]),
        gla_gate_w2=gla_gate_w2, gla_gate_b=gla_gate_b, gla_norm=gla_norm, c_w_out=c_w_out,
    )
    router_pad = _pad_cols(moe_router, LANES)

    cvecs = jnp.concatenate([c_ctx[None, :], c, jnp.zeros((8 - 1 - c.shape[0], D_MODEL), F32)], axis=0)
    mods = modulation_all(cvecs, w_mod, b_mod)
    n_lat = c.shape[0]

    rope_tabs = _axial_rope_tables(x_sample.shape[1])
    caches = (cache_mla_ckv, cache_mla_krope, state_rwkv, state_gla)
    bc, tc, _ = x_prompt.shape
    bl, tl, _ = x_sample.shape
    cap_c = CAP_FACTOR * tc // N_EXPERTS
    cap_l = CAP_FACTOR * tl // N_EXPERTS
    xc = x_prompt.reshape(bc * tc, D_MODEL)
    xl = x_sample.reshape(bl * tl, D_MODEL)
    outs = dict(ckv=[], kr=[], rwkv=[], gla=[])
    for l in range(DEPTH):
        mod_c = [t[:, None, :] for t in jnp.split(mods[l, 0:1], N_MOD, axis=-1)]
        mod_l = [t[:, None, :] for t in jnp.split(mods[l, 1:1 + n_lat], N_MOD, axis=-1)]
        xc = _mixer_layer(xc, l, mod_c, P, None, None, bc, tc, outs)
        xl = _mixer_layer(xl, l, mod_l, P, caches, rope_tabs, bl, tl, None)
        xc3 = xc.reshape(bc, tc, D_MODEL)
        xl3 = xl.reshape(bl, tl, D_MODEL)
        xe_c, rank_c, aff_c = moe_dispatch(xc3, g_norm_ffn[l], mod_c[4], mod_c[3], router_pad[l], cap_c)
        xe_l, rank_l, aff_l = moe_dispatch(xl3, g_norm_ffn[l], mod_l[4], mod_l[3], router_pad[l], cap_l)
        ye_c, ye_l = moe_experts(xe_c, xe_l, moe_w_gate, moe_w_up, moe_w_down, l)
        xc = moe_combine(xc3, mod_c[5], ye_c, rank_c, aff_c, cap_c).reshape(bc * tc, D_MODEL)
        xl = moe_combine(xl3, mod_l[5], ye_l, rank_l, aff_l, cap_l).reshape(bl * tl, D_MODEL)
    y_prompt = final_norm(xc, g_final).reshape(bc, tc, D_MODEL)
    y_sample = final_norm(xl, g_final).reshape(bl, tl, D_MODEL)
    return (y_prompt, y_sample, jnp.stack(outs['ckv'], axis=1), jnp.stack(outs['kr'], axis=1),
            jnp.stack(outs['rwkv'], axis=1), jnp.stack(outs['gla'], axis=1))
```

```python
import functools

import jax
import jax.numpy as jnp
from jax import lax
from jax.experimental import pallas as pl
from jax.experimental.pallas import tpu as pltpu

F32 = jnp.float32
BF16 = jnp.bfloat16
HIGHEST = lax.Precision.HIGHEST

VMEM_LIMIT_BYTES = 56 * 1024 * 1024
LANES = 128

D_MODEL = 2048
DEPTH = 4
GRID_W = 64
N_MOD = 6
NORM_EPS = 1e-6

A_WIDTH = 1024
A_HEAD = 64
A_HEADS = 16
W_LORA = 64
ICL_LORA = 64
G_LORA = 160
G_LORA_PAD = 256
GN_EPS = 64e-5

B_HEADS = 8
NOPE = 128
ROPE_D = 64
V_HEAD = 128
Q_LORA = 512
KV_LORA = 512
ROPE_BASE = 10000.0

C_HEADS = 4
C_QK = 1024
C_V = 2048
C_DK = 256
C_DV = 512
GATE_LORA = 16
GATE_TAU = 16.0
CHUNK = 64

N_EXPERTS = 16
D_EXPERT = 4096
CAP_FACTOR = 2

AB_PAD_COLS = 5120
AB_XG_BLK = 3072 // G_LORA_PAD
AB_XW_BLK = 3328 // LANES
AB_XA_BLK = 3456 // LANES
AB_CQ_BLK = 3584 // Q_LORA
AB_CKV_BLK = 4096 // KV_LORA
AB_KR_COL = 4608
C_PAD_COLS = 6656
C_XG_BLK = 6144 // LANES

TM = 1024


def _cparams(sem):
    return pltpu.CompilerParams(dimension_semantics=sem, vmem_limit_bytes=VMEM_LIMIT_BYTES)


def _sigmoid(x):
    return 1.0 / (1.0 + jnp.exp(-x))


def _softplus(x):
    return jnp.maximum(x, 0.0) + jnp.log(1.0 + jnp.exp(-jnp.abs(x)))


def _silu(x):
    return x * _sigmoid(x)


def _mod_kernel(c_ref, w_ref, b_ref, o_ref):
    a = _silu(c_ref[...])
    o_ref[...] = jnp.dot(a, w_ref[...], preferred_element_type=F32, precision=HIGHEST) + b_ref[...]


def modulation_all(cvecs, w_mod, b_mod):
    tn = 1024
    n = w_mod.shape[-1]
    return pl.pallas_call(
        _mod_kernel,
        grid=(DEPTH, n // tn),
        in_specs=[
            pl.BlockSpec((8, D_MODEL), lambda l, j: (0, 0)),
            pl.BlockSpec((None, D_MODEL, tn), lambda l, j: (l, 0, j)),
            pl.BlockSpec((None, 1, tn), lambda l, j: (l, 0, j)),
        ],
        out_specs=pl.BlockSpec((None, 8, tn), lambda l, j: (l, 0, j)),
        out_shape=jax.ShapeDtypeStruct((DEPTH, 8, n), F32),
        compiler_params=_cparams(("parallel", "parallel")),
        name="modulation",
    )(cvecs, w_mod, b_mod.reshape(DEPTH, 1, n))


def _norm_mod(x, g, sc, sh):
    ms = jnp.mean(x * x, axis=-1, keepdims=True)
    y = x * lax.rsqrt(ms + NORM_EPS) * g
    return y * (1.0 + sc) + sh


def _nm_mm_kernel(x_ref, g_ref, sc_ref, sh_ref, w_ref, *rest, shift_t):
    if shift_t:
        mu_ref, o_ref, h_scr = rest
    else:
        o_ref, h_scr = rest

    @pl.when(pl.program_id(1) == 0)
    def _():
        h_scr[...] = _norm_mod(x_ref[...], g_ref[...], sc_ref[...], sh_ref[...]).astype(BF16)

    acc = jnp.dot(h_scr[...], w_ref[...].astype(BF16), preferred_element_type=F32)
    if shift_t:
        tm = acc.shape[0]
        row = lax.broadcasted_iota(jnp.int32, acc.shape, 0) & (shift_t - 1)
        prev = jnp.where(row == 0, 0.0, pltpu.roll(acc, 1, 0))
        nxt = jnp.where(row == shift_t - 1, 0.0, pltpu.roll(acc, tm - 1, 0))
        acc = acc + mu_ref[0:1, :] * (prev - acc) + mu_ref[1:2, :] * (nxt - acc)
    o_ref[...] = acc.astype(o_ref.dtype)


def nm_matmul(x, g, sc, sh, w, *, rows_per_mod, tn, mu=None, shift_t=0):
    m, d = x.shape
    n = w.shape[1]
    tpm = rows_per_mod // TM
    in_specs = [
        pl.BlockSpec((TM, d), lambda i, j: (i, 0)),
        pl.BlockSpec((1, d), lambda i, j: (0, 0)),
        pl.BlockSpec((None, 1, d), lambda i, j: (i // tpm, 0, 0)),
        pl.BlockSpec((None, 1, d), lambda i, j: (i // tpm, 0, 0)),
        pl.BlockSpec((d, tn), lambda i, j: (0, j)),
    ]
    args = [x, g.reshape(1, d), sc, sh, w]
    if shift_t:
        in_specs.append(pl.BlockSpec((2, tn), lambda i, j: (0, j)))
        args.append(mu)
    return pl.pallas_call(
        functools.partial(_nm_mm_kernel, shift_t=shift_t),
        grid=(m // TM, n // tn),
        in_specs=in_specs,
        out_specs=pl.BlockSpec((TM, tn), lambda i, j: (i, j)),
        out_shape=jax.ShapeDtypeStruct((m, n), F32),
        scratch_shapes=[pltpu.VMEM((TM, d), BF16)],
        compiler_params=_cparams(("parallel", "arbitrary")),
        name="norm_mod_matmul",
    )(*args)


def _rms_mm_kernel(x_ref, g_ref, w_ref, *out_refs, normalize, emit_norm):
    x = x_ref[...]
    if normalize:
        ms = jnp.mean(x * x, axis=-1, keepdims=True)
        x = x * lax.rsqrt(ms + NORM_EPS) * g_ref[...]
    if emit_norm:
        out_refs[1][...] = x
    out_refs[0][...] = jnp.dot(x.astype(BF16), w_ref[...].astype(BF16),
                               preferred_element_type=F32).astype(out_refs[0].dtype)


def rms_matmul(x, col_blk, k, g, w, *, normalize, emit_norm, out_dtype):
    m = x.shape[0]
    n = w.shape[1]
    tm = min(TM, m)
    out_shape = [jax.ShapeDtypeStruct((m, n), out_dtype)]
    out_specs = [pl.BlockSpec((tm, n), lambda i: (i, 0))]
    if emit_norm:
        out_shape.append(jax.ShapeDtypeStruct((m, k), F32))
        out_specs.append(pl.BlockSpec((tm, k), lambda i: (i, 0)))
    res = pl.pallas_call(
        functools.partial(_rms_mm_kernel, normalize=normalize, emit_norm=emit_norm),
        grid=(m // tm,),
        in_specs=[
            pl.BlockSpec((tm, k), lambda i: (i, col_blk)),
            pl.BlockSpec((1, k), lambda i: (0, 0)),
            pl.BlockSpec((k, n), lambda i: (0, 0)),
        ],
        out_specs=out_specs,
        out_shape=out_shape,
        compiler_params=_cparams(("parallel",)),
        name="rms_matmul",
    )(x, g.reshape(1, k), w)
    return res


def _res_mm_kernel(*refs, n_x):
    x_refs = refs[:n_x]
    w_refs = refs[n_x:2 * n_x]
    res_ref, gate_ref, o_ref = refs[2 * n_x:]
    acc = jnp.dot(x_refs[0][...], w_refs[0][...].astype(BF16), preferred_element_type=F32)
    for xr, wr in zip(x_refs[1:], w_refs[1:]):
        acc = acc + jnp.dot(xr[...], wr[...].astype(BF16), preferred_element_type=F32)
    o_ref[...] = res_ref[...] + gate_ref[...] * acc


def res_matmul(xs, w, res, gate, *, rows_per_mod, tn=512):
    m, n = res.shape
    tpm = rows_per_mod // TM
    in_specs, k0 = [], 0
    for xx in xs:
        in_specs.append(pl.BlockSpec((TM, xx.shape[1]), lambda i, j: (i, 0)))
    for xx in xs:
        kx = xx.shape[1]
        blk = k0 // kx
        in_specs.append(pl.BlockSpec((kx, tn), lambda i, j, blk=blk: (blk, j)))
        k0 += kx
    in_specs.append(pl.BlockSpec((TM, tn), lambda i, j: (i, j)))
    in_specs.append(pl.BlockSpec((None, 1, tn), lambda i, j: (i // tpm, 0, j)))
    return pl.pallas_call(
        functools.partial(_res_mm_kernel, n_x=len(xs)),
        grid=(m // TM, n // tn),
        in_specs=in_specs,
        out_specs=pl.BlockSpec((TM, tn), lambda i, j: (i, j)),
        out_shape=jax.ShapeDtypeStruct((m, n), F32),
        compiler_params=_cparams(("parallel", "parallel")),
        name="residual_matmul",
    )(*xs, *([w] * len(xs)), res, gate)


def _final_norm_kernel(x_ref, g_ref, o_ref):
    x = x_ref[...]
    ms = jnp.mean(x * x, axis=-1, keepdims=True)
    o_ref[...] = x * lax.rsqrt(ms + NORM_EPS) * g_ref[...]


def final_norm(x, g):
    m, d = x.shape
    return pl.pallas_call(
        _final_norm_kernel,
        grid=(m // TM,),
        in_specs=[pl.BlockSpec((TM, d), lambda i: (i, 0)), pl.BlockSpec((1, d), lambda i: (0, 0))],
        out_specs=pl.BlockSpec((TM, d), lambda i: (i, 0)),
        out_shape=jax.ShapeDtypeStruct((m, d), F32),
        compiler_params=_cparams(("parallel",)),
        name="final_norm",
    )(x, g.reshape(1, d))


def _head_pair_ones():
    r = lax.broadcasted_iota(jnp.int32, (LANES, LANES), 0) // A_HEAD
    c = lax.broadcasted_iota(jnp.int32, (LANES, LANES), 1) // A_HEAD
    return (r == c).astype(F32)


def _head_pair_eye():
    r = lax.broadcasted_iota(jnp.int32, (A_HEAD, LANES), 0)
    c = lax.broadcasted_iota(jnp.int32, (A_HEAD, LANES), 1) & (A_HEAD - 1)
    return (r == c).astype(F32)


RWKV_G = 4
RWKV_HPB = 2
RWKV_TB = 256
RWKV_W = RWKV_HPB * LANES


def _head_sum(z, ones_bd):
    parts = [jnp.dot(z[:, j * LANES:(j + 1) * LANES], ones_bd, preferred_element_type=F32, precision=HIGHEST)
             for j in range(z.shape[1] // LANES)]
    return parts[0] if len(parts) == 1 else jnp.concatenate(parts, axis=1)


def _rwkv_scan_kernel(*refs, has_s0):
    views = (refs[0:5], refs[5:10])
    w2_ref, w0_ref, a2_ref, a0_ref, kk_ref, ka_ref = refs[10:16]
    rest = refs[16:]
    if has_s0:
        s0_ref, rest = rest[0], rest[1:]
    yf_ref, yb_ref, sfin_ref, a_s, w_s, b_s, k_s, r_s, v_s, y_s, st_s = rest
    y_refs = (yf_ref, yb_ref)
    n_g, tb_len, _ = views[0][0].shape
    tb = pl.program_id(2)
    n_tb = pl.num_programs(2)
    ones_bd = _head_pair_ones()
    ones_bd16 = ones_bd.astype(BF16)
    eye2 = _head_pair_eye()

    def put(dst, d, g, val):
        for j in range(RWKV_HPB):
            dst[d, g, j] = val[:, j * LANES:(j + 1) * LANES]

    for d in range(2):
        r_ref, k_ref, v_ref, xw_ref, xa_ref = views[d]
        for g in range(n_g):
            kraw = k_ref[g]
            kk = kraw * kk_ref[...]
            kk = kk * lax.rsqrt(jnp.maximum(_head_sum(kk * kk, ones_bd), 1e-24))
            put(a_s, d, g, -kk)
            xw = xw_ref[g][:, d * W_LORA:(d + 1) * W_LORA]
            xa = xa_ref[g][:, d * ICL_LORA:(d + 1) * ICL_LORA]
            wl = w0_ref[d] + jnp.dot(jnp.tanh(xw), w2_ref[d], preferred_element_type=F32, precision=HIGHEST)
            put(w_s, d, g, jnp.exp(-jnp.exp(-_softplus(-wl) - 0.5)))
            a = _sigmoid(a0_ref[d] + jnp.dot(xa, a2_ref[d], preferred_element_type=F32, precision=HIGHEST))
            put(k_s, d, g, kraw * (1.0 + (a - 1.0) * ka_ref[...]))
            put(b_s, d, g, kk * a)
            put(r_s, d, g, r_ref[g])
            put(v_s, d, g, v_ref[g])

    chains = [(d, g, j) for d in range(2) for g in range(n_g) for j in range(RWKV_HPB)]

    @pl.when(tb == 0)
    def _():
        for d, g, j in chains:
            if has_s0:
                st_s[d, g, j] = jnp.concatenate([s0_ref[g, d, 2 * j], s0_ref[g, d, 2 * j + 1]], axis=1)
            else:
                st_s[d, g, j] = jnp.zeros((A_HEAD, LANES), F32)

    def step(t, states):
        new_states = []
        rows = [pl.ds(t if d == 0 else tb_len - 1 - t, 1) for d in range(2)]
        prev_rows = [pl.ds(jnp.maximum(t - 1, 0), 1), pl.ds(jnp.minimum(tb_len - t, tb_len - 1), 1)]
        lhs = []
        for (d, g, j), s in zip(chains, states):
            lhs.append((s * a_s[d, g, j, rows[d], :]).astype(BF16))
            lhs.append((eye2 * v_s[d, g, j, rows[d], :]).astype(BF16))
            lhs.append((s * r_s[d, g, j, prev_rows[d], :]).astype(BF16))
        res = jnp.dot(jnp.concatenate(lhs, axis=0), ones_bd16, preferred_element_type=F32)
        for i, ((d, g, j), s) in enumerate(zip(chains, states)):
            sa, vb, yb = (res[(3 * i + q) * A_HEAD:(3 * i + q + 1) * A_HEAD] for q in range(3))
            y_s[d, g, j, prev_rows[d], :] = jnp.sum(yb * eye2, axis=0, keepdims=True)
            row = rows[d]
            new_states.append(s * w_s[d, g, j, row, :] + sa * b_s[d, g, j, row, :] + vb * k_s[d, g, j, row, :])
        return tuple(new_states)

    final = lax.fori_loop(0, tb_len, step, tuple(st_s[c] for c in chains))
    last_rows = (pl.ds(tb_len - 1, 1), pl.ds(0, 1))
    lhs = [(s * r_s[d, g, j, last_rows[d], :]).astype(BF16) for (d, g, j), s in zip(chains, final)]
    res = jnp.dot(jnp.concatenate(lhs, axis=0), ones_bd16, preferred_element_type=F32)
    for i, (c, s) in enumerate(zip(chains, final)):
        d, g, j = c
        st_s[c] = s
        y_s[d, g, j, last_rows[d], :] = jnp.sum(res[i * A_HEAD:(i + 1) * A_HEAD] * eye2, axis=0, keepdims=True)
    for d, g, j in chains:
        y_refs[d][g, :, j * LANES:(j + 1) * LANES] = y_s[d, g, j]

    @pl.when(tb == n_tb - 1)
    def _():
        for d, g, j in chains:
            s = st_s[d, g, j]
            sfin_ref[g, d, 2 * j] = s[:, :A_HEAD]
            sfin_ref[g, d, 2 * j + 1] = s[:, A_HEAD:]


def rwkv_scan(u, p, s0, layer_idx):
    bsz, t_len, _ = u.shape
    n_tb = t_len // RWKV_TB
    n_hpg = A_WIDTH // RWKV_W
    fwd = lambda tb: tb
    bwd = lambda tb: n_tb - 1 - tb

    def view(tmap):
        def col(first_blk, width=RWKV_W, per_group=True):
            return pl.BlockSpec((RWKV_G, RWKV_TB, width),
                                lambda bi, hg, tb: (bi, tmap(tb), first_blk + (hg if per_group else 0)))
        return [col(0), col(n_hpg), col(2 * n_hpg),
                col(AB_XW_BLK, LANES, False), col(AB_XA_BLK, LANES, False)]

    in_specs = view(fwd) + view(bwd) + [
        pl.BlockSpec((2, W_LORA, RWKV_W), lambda bi, hg, tb: (0, 0, hg)),
        pl.BlockSpec((2, 1, RWKV_W), lambda bi, hg, tb: (0, 0, hg)),
        pl.BlockSpec((2, ICL_LORA, RWKV_W), lambda bi, hg, tb: (0, 0, hg)),
        pl.BlockSpec((2, 1, RWKV_W), lambda bi, hg, tb: (0, 0, hg)),
        pl.BlockSpec((1, RWKV_W), lambda bi, hg, tb: (0, hg)),
        pl.BlockSpec((1, RWKV_W), lambda bi, hg, tb: (0, hg)),
    ]
    args = [u] * 10 + [p['w2'], p['w0'].reshape(2, 1, A_WIDTH), p['a2'], p['a0'].reshape(2, 1, A_WIDTH),
                       p['k_k'].reshape(1, A_WIDTH), p['k_a'].reshape(1, A_WIDTH)]
    heads_blk = 2 * RWKV_HPB
    if s0 is not None:
        in_specs.append(pl.BlockSpec((RWKV_G, None, 2, heads_blk, A_HEAD, A_HEAD),
                                     lambda bi, hg, tb: (bi, layer_idx, 0, hg, 0, 0)))
        args.append(s0)
    blk = (2, RWKV_G, RWKV_HPB, RWKV_TB, LANES)
    return pl.pallas_call(
        functools.partial(_rwkv_scan_kernel, has_s0=s0 is not None),
        grid=(bsz // RWKV_G, n_hpg, n_tb),
        in_specs=in_specs,
        out_specs=[
            pl.BlockSpec((RWKV_G, RWKV_TB, RWKV_W), lambda bi, hg, tb: (bi, fwd(tb), hg)),
            pl.BlockSpec((RWKV_G, RWKV_TB, RWKV_W), lambda bi, hg, tb: (bi, bwd(tb), hg)),
            pl.BlockSpec((RWKV_G, 2, heads_blk, A_HEAD, A_HEAD), lambda bi, hg, tb: (bi, 0, hg, 0, 0)),
        ],
        out_shape=[
            jax.ShapeDtypeStruct((bsz, t_len, A_WIDTH), F32),
            jax.ShapeDtypeStruct((bsz, t_len, A_WIDTH), F32),
            jax.ShapeDtypeStruct((bsz, 2, A_HEADS, A_HEAD, A_HEAD), F32),
        ],
        scratch_shapes=[
            pltpu.VMEM(blk, F32),
            pltpu.VMEM(blk, F32),
            pltpu.VMEM(blk, F32),
            pltpu.VMEM(blk, F32),
            pltpu.VMEM(blk, F32),
            pltpu.VMEM(blk, F32),
            pltpu.VMEM(blk, F32),
            pltpu.VMEM((2, RWKV_G, RWKV_HPB, A_HEAD, LANES), F32),
        ],
        compiler_params=_cparams(("parallel", "parallel", "arbitrary")),
        name="rwkv7_scan",
    )(*args)


def _rwkv_post_kernel(yf_ref, yb_ref, r_ref, k_ref, v_ref, xa_ref, xg_ref,
                      a2_ref, a0_ref, ka_ref, rk_ref, gng_ref, gnb_ref, g2_ref, o_ref):
    ones_bd = _head_pair_ones()
    inv_n = 1.0 / A_HEAD
    y = yf_ref[...] + yb_ref[...]
    mu = _head_sum(y, ones_bd) * inv_n
    yc = y - mu
    var = _head_sum(yc * yc, ones_bd) * inv_n
    yn = yc * lax.rsqrt(var + GN_EPS) * gng_ref[...] + gnb_ref[...]
    r = r_ref[...]
    kraw = k_ref[...]
    rk_sum = jnp.zeros_like(r)
    for d in range(2):
        xa = xa_ref[:, d * ICL_LORA:(d + 1) * ICL_LORA]
        a = _sigmoid(a0_ref[d] + jnp.dot(xa, a2_ref[d], preferred_element_type=F32, precision=HIGHEST))
        rk_sum = rk_sum + r * (kraw * (1.0 + (a - 1.0) * ka_ref[...])) * rk_ref[...]
    bonus = _head_sum(rk_sum, ones_bd) * v_ref[...]
    gate = jnp.dot(_sigmoid(xg_ref[...]).astype(BF16), g2_ref[...].astype(BF16), preferred_element_type=F32)
    o_ref[...] = ((yn + bonus) * gate).astype(o_ref.dtype)


def rwkv_post(yf, yb, u2, p):
    m = yf.shape[0]
    tm = 256
    full = lambda shape: pl.BlockSpec(shape, lambda i: (0,) * len(shape))
    row = lambda width, blk: pl.BlockSpec((tm, width), lambda i: (i, blk))
    return pl.pallas_call(
        _rwkv_post_kernel,
        grid=(m // tm,),
        in_specs=[
            row(A_WIDTH, 0), row(A_WIDTH, 0),
            row(A_WIDTH, 0), row(A_WIDTH, 1), row(A_WIDTH, 2),
            row(LANES, AB_XA_BLK), row(G_LORA_PAD, AB_XG_BLK),
            full((2, ICL_LORA, A_WIDTH)), full((2, 1, A_WIDTH)),
            full((1, A_WIDTH)), full((1, A_WIDTH)), full((1, A_WIDTH)), full((1, A_WIDTH)),
            full((G_LORA_PAD, A_WIDTH)),
        ],
        out_specs=row(A_WIDTH, 0),
        out_shape=jax.ShapeDtypeStruct((m, A_WIDTH), BF16),
        compiler_params=_cparams(("parallel",)),
        name="rwkv7_post",
    )(yf, yb, u2, u2, u2, u2, u2, p['a2'], p['a0'].reshape(2, 1, A_WIDTH), p['k_a'].reshape(1, A_WIDTH),
      p['r_k'].reshape(1, A_WIDTH), p['gn_g'].reshape(1, A_WIDTH), p['gn_b'].reshape(1, A_WIDTH), p['g2'])


def rwkv_mix(u, p, s0, layer_idx):
    bsz, t_len, _ = u.shape
    m = bsz * t_len
    yf, yb, sfin = rwkv_scan(u, p, s0, layer_idx)
    out = rwkv_post(yf.reshape(m, A_WIDTH), yb.reshape(m, A_WIDTH), u.reshape(m, AB_PAD_COLS), p)
    return out, sfin


def _mla_kernel(qn_ref, qr_ref, kn_ref, kr_ref, v_ref, o_ref):
    scale = (NOPE + ROPE_D) ** -0.5
    krb = kr_ref[...].astype(BF16)
    nt = (((1,), (1,)), ((), ()))
    for h in range(B_HEADS):
        qn = qn_ref[:, h * NOPE:(h + 1) * NOPE].astype(BF16)
        qr = qr_ref[:, h * ROPE_D:(h + 1) * ROPE_D].astype(BF16)
        s = lax.dot_general(qn, kn_ref[:, h * NOPE:(h + 1) * NOPE], nt, preferred_element_type=F32)
        s = s + lax.dot_general(qr, krb, nt, preferred_element_type=F32)
        s = s * scale
        m = jnp.max(s, axis=-1, keepdims=True)
        e = jnp.exp(s - m)
        p = e / jnp.sum(e, axis=-1, keepdims=True)
        o = jnp.dot(p.astype(BF16), v_ref[:, h * V_HEAD:(h + 1) * V_HEAD], preferred_element_type=F32)
        o_ref[:, h * V_HEAD:(h + 1) * V_HEAD] = o.astype(o_ref.dtype)


def mla_attention(q, kv, kr, *, tq):
    bsz, t_len, _ = q.shape
    s_len = kv.shape[1]
    hn = B_HEADS * NOPE
    return pl.pallas_call(
        _mla_kernel,
        grid=(bsz, t_len // tq),
        in_specs=[
            pl.BlockSpec((None, tq, hn), lambda b, i: (b, i, 0)),
            pl.BlockSpec((None, tq, B_HEADS * ROPE_D), lambda b, i: (b, i, hn // (B_HEADS * ROPE_D))),
            pl.BlockSpec((None, s_len, hn), lambda b, i: (b, 0, 0)),
            pl.BlockSpec((None, s_len, ROPE_D), lambda b, i: (b, 0, 0)),
            pl.BlockSpec((None, s_len, hn), lambda b, i: (b, 0, 1)),
        ],
        out_specs=pl.BlockSpec((None, tq, hn), lambda b, i: (b, i, 0)),
        out_shape=jax.ShapeDtypeStruct((bsz, t_len, hn), BF16),
        compiler_params=_cparams(("parallel", "parallel")),
        name="mla_attention",
    )(q, q, kv, kr, kv)


def _gla_kernel(q_ref, k_ref, v_ref, og_ref, xg_ref, gw_ref, gb_ref, gn_ref, *rest, has_s0):
    if has_s0:
        s0_ref, o_ref, sfin_ref, lgf_s, lgb_s, yf_s, yb_s, stf_s, stb_s = rest
    else:
        o_ref, sfin_ref, lgf_s, lgb_s, yf_s, yb_s, stf_s, stb_s = rest
    lg_s, y_s, st_s = (lgf_s, lgb_s), (yf_s, yb_s), (stf_s, stb_s)
    t_len = q_ref.shape[0]
    n_chunks = t_len // CHUNK
    scale = C_DK ** -0.5
    nt = (((1,), (1,)), ((), ()))
    xg = xg_ref[...]
    ri = lax.broadcasted_iota(jnp.int32, (CHUNK, CHUNK), 0)
    ci = lax.broadcasted_iota(jnp.int32, (CHUNK, CHUNK), 1)
    keeps = (ri >= ci, ri <= ci)
    tris = tuple(kp.astype(F32) for kp in keeps)
    for d in range(2):
        z = jnp.dot(xg[:, d * GATE_LORA:(d + 1) * GATE_LORA], gw_ref[d], preferred_element_type=F32,
                    precision=HIGHEST) + gb_ref[d]
        lg_s[d][...] = -_softplus(-z) * (1.0 / GATE_TAU)
        if has_s0:
            st_s[d][...] = s0_ref[d].T
        else:
            st_s[d][...] = jnp.zeros(st_s[d].shape, F32)

    def chunk(i, carry):
        for d in range(2):
            c = i if d == 0 else n_chunks - 1 - i
            rows = pl.ds(pl.multiple_of(c * CHUNK, CHUNK), CHUNK)
            q = q_ref[rows, :] * scale
            k = k_ref[rows, :]
            v = v_ref[rows, :]
            b = jnp.dot(tris[d], lg_s[d][rows, :], preferred_element_type=F32, precision=HIGHEST)
            bl = b[CHUNK - 1:CHUNK, :] if d == 0 else b[0:1, :]
            qt = (q * jnp.exp(b)).astype(BF16)
            kt = (k * jnp.exp(-b)).astype(BF16)
            kd = (k * jnp.exp(bl - b)).astype(BF16)
            att = jnp.where(keeps[d], lax.dot_general(qt, kt, nt, preferred_element_type=F32), 0.0)
            st = st_s[d][...]
            o = jnp.dot(att.astype(BF16), v.astype(BF16), preferred_element_type=F32)
            y_s[d][rows, :] = o + lax.dot_general(qt, st.astype(BF16), nt, preferred_element_type=F32)
            st_s[d][...] = st * jnp.exp(bl) + jnp.dot(v.T.astype(BF16), kd, preferred_element_type=F32)
        return carry

    lax.fori_loop(0, n_chunks, chunk, 0)
    for d in range(2):
        sfin_ref[d] = st_s[d][...].T
    y = yf_s[...] + yb_s[...]
    yn = y * lax.rsqrt(jnp.mean(y * y, axis=-1, keepdims=True) + NORM_EPS) * gn_ref[...]
    o_ref[...] = (yn * _silu(og_ref[...])).astype(o_ref.dtype)


def gla_mix(u, p, s0, layer_idx):
    bsz, t_len, _ = u.shape
    in_specs = [
        pl.BlockSpec((None, t_len, C_DK), lambda b, h: (b, 0, h)),
        pl.BlockSpec((None, t_len, C_DK), lambda b, h: (b, 0, C_HEADS + h)),
        pl.BlockSpec((None, t_len, C_DV), lambda b, h: (b, 0, C_HEADS + h)),
        pl.BlockSpec((None, t_len, C_DV), lambda b, h: (b, 0, 2 * C_HEADS + h)),
        pl.BlockSpec((None, t_len, LANES), lambda b, h: (b, 0, C_XG_BLK)),
        pl.BlockSpec((2, GATE_LORA, C_DK), lambda b, h: (0, 0, h)),
        pl.BlockSpec((2, 1, C_DK), lambda b, h: (0, 0, h)),
        pl.BlockSpec((1, C_DV), lambda b, h: (0, 0)),
    ]
    args = [u, u, u, u, u, p['gate_w2'], p['gate_b'].reshape(2, 1, C_QK), p['gn'].reshape(1, C_DV)]
    if s0 is not None:
        in_specs.append(pl.BlockSpec((None, None, 2, None, C_DK, C_DV), lambda b, h: (b, layer_idx, 0, h, 0, 0)))
        args.append(s0)
    return pl.pallas_call(
        functools.partial(_gla_kernel, has_s0=s0 is not None),
        grid=(bsz, C_HEADS),
        in_specs=in_specs,
        out_specs=[
            pl.BlockSpec((None, t_len, C_DV), lambda b, h: (b, 0, h)),
            pl.BlockSpec((None, 2, None, C_DK, C_DV), lambda b, h: (b, 0, h, 0, 0)),
        ],
        out_shape=[
            jax.ShapeDtypeStruct((bsz, t_len, C_V), BF16),
            jax.ShapeDtypeStruct((bsz, 2, C_HEADS, C_DK, C_DV), F32),
        ],
        scratch_shapes=[
            pltpu.VMEM((t_len, C_DK), F32),
            pltpu.VMEM((t_len, C_DK), F32),
            pltpu.VMEM((t_len, C_DV), F32),
            pltpu.VMEM((t_len, C_DV), F32),
            pltpu.VMEM((C_DV, C_DK), F32),
            pltpu.VMEM((C_DV, C_DK), F32),
        ],
        compiler_params=_cparams(("parallel", "parallel")),
        name="gla_mix",
    )(*args)


DISPATCH_TN = 512


def _dispatch_kernel(x_ref, g_ref, sc_ref, sh_ref, wr_ref, xe_ref, rank_ref, aff_ref, hb_s, onehot_s, *, cap):
    t_len = x_ref.shape[0]
    j = pl.program_id(1)

    @pl.when(j == 0)
    def _():
        h = _norm_mod(x_ref[...], g_ref[...], sc_ref[...], sh_ref[...])
        for n in range(hb_s.shape[0]):
            hb_s[n] = h[:, n * DISPATCH_TN:(n + 1) * DISPATCH_TN].astype(BF16)
        logits = jnp.dot(h, wr_ref[...], preferred_element_type=F32, precision=HIGHEST)
        lane = lax.broadcasted_iota(jnp.int32, logits.shape, 1)
        logits = jnp.where(lane < N_EXPERTS, logits, -jnp.inf)
        m = jnp.max(logits, axis=-1, keepdims=True)
        e = jnp.exp(logits - m)
        aff = e / jnp.sum(e, axis=-1, keepdims=True)
        aff_ref[...] = aff
        aff_t = aff.T
        sub_i = lax.broadcasted_iota(jnp.int32, (t_len, t_len), 0)
        lane_i = lax.broadcasted_iota(jnp.int32, (t_len, t_len), 1)
        slot = lax.broadcasted_iota(jnp.int32, (cap, t_len), 0).astype(F32)
        expert_row = lax.broadcasted_iota(jnp.int32, (N_EXPERTS, t_len), 0)
        ranks = jnp.zeros((N_EXPERTS, t_len), F32)
        for ex in range(N_EXPERTS):
            col = aff[:, ex:ex + 1]
            row = aff_t[ex:ex + 1, :]
            ahead = (col > row) | ((col == row) & (sub_i < lane_i))
            rank_row = jnp.sum(ahead.astype(F32), axis=0, keepdims=True)
            onehot_s[ex * cap:(ex + 1) * cap, :] = (rank_row == slot).astype(BF16)
            ranks = jnp.where(expert_row == ex, rank_row, ranks)
        pad = jnp.zeros((LANES - N_EXPERTS, t_len), F32)
        rank_ref[...] = jnp.concatenate([ranks, pad], axis=0).T

    rows = jnp.dot(onehot_s[...], hb_s[j], preferred_element_type=F32).astype(BF16)
    for ex in range(N_EXPERTS):
        xe_ref[ex] = rows[ex * cap:(ex + 1) * cap]


def moe_dispatch(x, g, sc, sh, w_router_pad, cap):
    bsz, t_len, d = x.shape
    per_req = sc.shape[0] > 1
    mod_spec = pl.BlockSpec((None, 1, d), (lambda b, j: (b, 0, 0)) if per_req else (lambda b, j: (0, 0, 0)))
    return pl.pallas_call(
        functools.partial(_dispatch_kernel, cap=cap),
        grid=(bsz, d // DISPATCH_TN),
        in_specs=[
            pl.BlockSpec((None, t_len, d), lambda b, j: (b, 0, 0)),
            pl.BlockSpec((1, d), lambda b, j: (0, 0)),
            mod_spec, mod_spec,
            pl.BlockSpec((d, LANES), lambda b, j: (0, 0)),
        ],
        out_specs=[
            pl.BlockSpec((N_EXPERTS, cap, DISPATCH_TN), lambda b, j: (0, b, j)),
            pl.BlockSpec((None, t_len, LANES), lambda b, j: (b, 0, 0)),
            pl.BlockSpec((None, t_len, LANES), lambda b, j: (b, 0, 0)),
        ],
        out_shape=[
            jax.ShapeDtypeStruct((N_EXPERTS, bsz * cap, d), BF16),
            jax.ShapeDtypeStruct((bsz, t_len, LANES), F32),
            jax.ShapeDtypeStruct((bsz, t_len, LANES), F32),
        ],
        scratch_shapes=[
            pltpu.VMEM((d // DISPATCH_TN, t_len, DISPATCH_TN), BF16),
            pltpu.VMEM((N_EXPERTS * cap, t_len), BF16),
        ],
        compiler_params=_cparams(("parallel", "arbitrary")),
        name="moe_dispatch",
    )(x, g.reshape(1, d), sc, sh, w_router_pad)


EXPERT_TF = 256
EXPERT_TN = 256
N_UP_STEPS = D_EXPERT // EXPERT_TF
N_DOWN_STEPS = D_MODEL // EXPERT_TN


def _experts_kernel(xc_ref, xl_ref, wg_ref, wu_ref, wd_ref, yc_ref, yl_ref, hid_s):
    s = pl.program_id(1)
    mc = xc_ref.shape[0]

    @pl.when(s < N_UP_STEPS)
    def _():
        wg = wg_ref[...].astype(BF16)
        wu = wu_ref[...].astype(BF16)
        cols = pl.ds(pl.multiple_of(s * EXPERT_TF, EXPERT_TF), EXPERT_TF)
        for x_ref, r0 in ((xc_ref, 0), (xl_ref, mc)):
            x = x_ref[...]
            hg = jnp.dot(x, wg, preferred_element_type=F32)
            hu = jnp.dot(x, wu, preferred_element_type=F32)
            hid_s[r0:r0 + x.shape[0], cols] = (_silu(hg) * hu).astype(BF16)

    @pl.when(s >= N_UP_STEPS)
    def _():
        wd = wd_ref[...].astype(BF16)
        yc_ref[...] = jnp.dot(hid_s[0:mc, :], wd, preferred_element_type=F32).astype(yc_ref.dtype)
        yl_ref[...] = jnp.dot(hid_s[mc:, :], wd, preferred_element_type=F32).astype(yl_ref.dtype)


def moe_experts(xc, xl, w_gate, w_up, w_down, layer):
    mc, ml = xc.shape[1], xl.shape[1]
    up_idx = lambda e, s: (layer, e, 0, jnp.minimum(s, N_UP_STEPS - 1))
    down_idx = lambda e, s: (layer, e, 0, jnp.maximum(s - N_UP_STEPS, 0))
    out_idx = lambda e, s: (e, 0, jnp.maximum(s - N_UP_STEPS, 0))
    return pl.pallas_call(
        _experts_kernel,
        grid=(N_EXPERTS, N_UP_STEPS + N_DOWN_STEPS),
        in_specs=[
            pl.BlockSpec((None, mc, D_MODEL), lambda e, s: (e, 0, 0)),
            pl.BlockSpec((None, ml, D_MODEL), lambda e, s: (e, 0, 0)),
            pl.BlockSpec((None, None, D_MODEL, EXPERT_TF), up_idx),
            pl.BlockSpec((None, None, D_MODEL, EXPERT_TF), up_idx),
            pl.BlockSpec((None, None, D_EXPERT, EXPERT_TN), down_idx),
        ],
        out_specs=[
            pl.BlockSpec((None, mc, EXPERT_TN), out_idx),
            pl.BlockSpec((None, ml, EXPERT_TN), out_idx),
        ],
        out_shape=[
            jax.ShapeDtypeStruct((N_EXPERTS, mc, D_MODEL), BF16),
            jax.ShapeDtypeStruct((N_EXPERTS, ml, D_MODEL), BF16),
        ],
        scratch_shapes=[pltpu.VMEM((mc + ml, D_EXPERT), BF16)],
        compiler_params=_cparams(("parallel", "arbitrary")),
        name="moe_experts",
    )(xc, xl, w_gate, w_up, w_down)


def _combine_kernel(x_ref, gate_ref, ye_ref, rank_ref, aff_ref, o_ref, *, cap):
    t_len = x_ref.shape[0]
    slot = lax.broadcasted_iota(jnp.int32, (t_len, cap), 1).astype(F32)
    acc = jnp.zeros(x_ref.shape, F32)
    for ex in range(N_EXPERTS):
        onehot = (rank_ref[:, ex:ex + 1] == slot).astype(BF16)
        acc = acc + aff_ref[:, ex:ex + 1] * jnp.dot(onehot, ye_ref[ex], preferred_element_type=F32)
    o_ref[...] = x_ref[...] + gate_ref[...] * acc


def moe_combine(x, gate, ye, rank, aff, cap):
    bsz, t_len, d = x.shape
    per_req = gate.shape[0] > 1
    tn = 512
    return pl.pallas_call(
        functools.partial(_combine_kernel, cap=cap),
        grid=(bsz, d // tn),
        in_specs=[
            pl.BlockSpec((None, t_len, tn), lambda b, j: (b, 0, j)),
            pl.BlockSpec((None, 1, tn), (lambda b, j: (b, 0, j)) if per_req else (lambda b, j: (0, 0, j))),
            pl.BlockSpec((N_EXPERTS, cap, tn), lambda b, j: (0, b, j)),
            pl.BlockSpec((None, t_len, LANES), lambda b, j: (b, 0, 0)),
            pl.BlockSpec((None, t_len, LANES), lambda b, j: (b, 0, 0)),
        ],
        out_specs=pl.BlockSpec((None, t_len, tn), lambda b, j: (b, 0, j)),
        out_shape=jax.ShapeDtypeStruct((bsz, t_len, d), F32),
        compiler_params=_cparams(("parallel", "parallel")),
        name="moe_combine",
    )(x, gate, ye, rank, aff)


def _pad_cols(w, n):
    return jnp.pad(w, ((0, 0),) * (w.ndim - 1) + ((0, n - w.shape[-1]),))


def _ab_layout(w):
    rkv = w[..., :3 * A_WIDTH]
    o = 3 * A_WIDTH
    xg = _pad_cols(w[..., o:o + G_LORA], G_LORA_PAD)
    o += G_LORA
    xw = w[..., o:o + 2 * W_LORA]
    o += 2 * W_LORA
    xa = w[..., o:o + 2 * ICL_LORA]
    o += 2 * ICL_LORA
    rest = w[..., o:]
    return _pad_cols(jnp.concatenate([rkv, xg, xw, xa, rest], axis=-1), AB_PAD_COLS)


def _c_layout(w):
    qkv = w[:, :2 * C_QK + C_V]
    o = 2 * C_QK + C_V
    xg = _pad_cols(w[:, o:o + 2 * GATE_LORA], LANES)
    og = w[:, o + 2 * GATE_LORA:]
    return _pad_cols(jnp.concatenate([qkv, og, xg], axis=-1), C_PAD_COLS)


def _axial_rope_tables(t):
    rows = t // GRID_W
    row = jnp.repeat(jnp.arange(rows, dtype=F32), GRID_W)
    col = jnp.tile(jnp.arange(GRID_W, dtype=F32), rows)
    half = ROPE_D // 2
    inv = 1.0 / (ROPE_BASE ** (jnp.arange(0, half, 2, dtype=F32) / half))
    ang_r = row[:, None] * inv[None, :]
    ang_c = col[:, None] * inv[None, :]
    return (jnp.cos(ang_r), jnp.sin(ang_r), jnp.cos(ang_c), jnp.sin(ang_c))


def _rotate(x, cos, sin):
    m = x.shape[-1] // 2
    x1, x2 = x[..., :m], x[..., m:]
    c = cos[None, :, None, :]
    s = sin[None, :, None, :]
    return jnp.concatenate([x1 * c - x2 * s, x1 * s + x2 * c], axis=-1)


def _apply_axial_rope(x, tabs):
    cr, sr, cc, sc = tabs
    half = ROPE_D // 2
    return jnp.concatenate([_rotate(x[..., :half], cr, sr), _rotate(x[..., half:], cc, sc)], axis=-1)


def _mixer_layer(x2, l, mod_l, P, caches, rope_tabs, bsz, t_len, outs):
    m = bsz * t_len
    is_ctx = caches is None
    rows_per_mod = m if is_ctx else t_len
    sh1, sc1, g1 = mod_l[0], mod_l[1], mod_l[2]
    i = l // 2
    if l % 2 == 0:
        u = nm_matmul(x2, P['g_norm_mix'][l], sc1, sh1, P['ab_w_in_pad'][i], rows_per_mod=rows_per_mod,
                      tn=512, mu=P['rwkv_mu_pad'][i], shift_t=t_len)
        p = dict(w2=P['rwkv_w2'][i], w0=P['rwkv_w0'][i], a2=P['rwkv_a2'][i], a0=P['rwkv_a0'][i],
                 g2=P['rwkv_g2_pad'][i], k_k=P['rwkv_k_k'][i], k_a=P['rwkv_k_a'][i], r_k=P['rwkv_r_k'][i],
                 gn_g=P['rwkv_gn_g'][i], gn_b=P['rwkv_gn_b'][i])
        u3 = u.reshape(bsz, t_len, AB_PAD_COLS)
        a_out, a_state = rwkv_mix(u3, p, None if is_ctx else caches[2], i)
        (q,) = rms_matmul(u, AB_CQ_BLK, Q_LORA, P['mla_q_norm'][i], P['mla_w_uq_r'][i],
                          normalize=True, emit_norm=False, out_dtype=F32)
        kv, ckv = rms_matmul(u, AB_CKV_BLK, KV_LORA, P['mla_kv_norm'][i], P['mla_w_ukv_r'][i],
                             normalize=True, emit_norm=True, out_dtype=BF16)
        kr = u3[:, :, AB_KR_COL:AB_KR_COL + ROPE_D]
        q = q.reshape(bsz, t_len, -1)
        kv = kv.reshape(bsz, t_len, -1)
        if is_ctx:
            outs['ckv'].append(ckv.reshape(bsz, t_len, KV_LORA))
            outs['kr'].append(kr)
            outs['rwkv'].append(a_state)
            keys_r = kr
        else:
            hn = B_HEADS * NOPE
            q_rope = _apply_axial_rope(q[..., hn:].reshape(bsz, t_len, B_HEADS, ROPE_D), rope_tabs)
            q = jnp.concatenate([q[..., :hn], q_rope.reshape(bsz, t_len, B_HEADS * ROPE_D)], axis=-1)
            kr = _apply_axial_rope(kr[:, :, None, :], rope_tabs)[:, :, 0, :]
            past = caches[0].shape[2]
            (kv_ctx,) = rms_matmul(caches[0][:, i].reshape(bsz * past, KV_LORA), 0, KV_LORA,
                                   P['mla_kv_norm'][i], P['mla_w_ukv_r'][i],
                                   normalize=False, emit_norm=False, out_dtype=BF16)
            kv = jnp.concatenate([kv, kv_ctx.reshape(bsz, past, -1)], axis=1)
            keys_r = jnp.concatenate([kr, caches[1][:, i]], axis=1)
        b_out = mla_attention(q, kv, keys_r, tq=256)
        return res_matmul([a_out, b_out.reshape(m, A_WIDTH)], P['ab_w_out'][i], x2, g1,
                          rows_per_mod=rows_per_mod)
    u = nm_matmul(x2, P['g_norm_mix'][l], sc1, sh1, P['c_w_in_pad'][i], rows_per_mod=rows_per_mod, tn=512)
    p = dict(gate_w2=P['gla_gate_w2'][i], gate_b=P['gla_gate_b'][i], gn=P['gla_norm'][i])
    o, st = gla_mix(u.reshape(bsz, t_len, C_PAD_COLS), p, None if is_ctx else caches[3], i)
    if is_ctx:
        outs['gla'].append(st)
    return res_matmul([o.reshape(m, C_V)], P['c_w_out'][i], x2, g1, rows_per_mod=rows_per_mod)


def kernel(x_prompt, x_sample, cache_mla_ckv, cache_mla_krope, state_rwkv, state_gla, c, c_ctx, w_mod, b_mod, g_norm_mix, g_norm_ffn, ab_w_in, rwkv_mu, rwkv_w0, rwkv_w2, rwkv_a0, rwkv_a2, rwkv_g2, rwkv_k_k, rwkv_k_a, rwkv_r_k, rwkv_gn_g, rwkv_gn_b, mla_q_norm, mla_w_uq, mla_kv_norm, mla_w_ukv, ab_w_out, c_w_in, gla_gate_w2, gla_gate_b, gla_norm, c_w_out, moe_router, moe_w_gate, moe_w_up, moe_w_down, g_final):
    n_ab = ab_w_in.shape[0]
    n_c = c_w_in.shape[0]
    w_uq = mla_w_uq.reshape(n_ab, Q_LORA, B_HEADS, NOPE + ROPE_D)
    w_ukv = mla_w_ukv.reshape(n_ab, KV_LORA, B_HEADS, NOPE + V_HEAD)
    P = dict(
        g_norm_mix=g_norm_mix, g_norm_ffn=g_norm_ffn, g_final=g_final,
        ab_w_in_pad=_ab_layout(ab_w_in), rwkv_mu_pad=_ab_layout(rwkv_mu),
        rwkv_w0=rwkv_w0, rwkv_w2=rwkv_w2, rwkv_a0=rwkv_a0, rwkv_a2=rwkv_a2,
        rwkv_g2_pad=jnp.pad(rwkv_g2, ((0, 0), (0, G_LORA_PAD - G_LORA), (0, 0))),
        rwkv_k_k=rwkv_k_k, rwkv_k_a=rwkv_k_a, rwkv_r_k=rwkv_r_k.reshape(n_ab, A_WIDTH),
        rwkv_gn_g=rwkv_gn_g, rwkv_gn_b=rwkv_gn_b,
        mla_q_norm=mla_q_norm, mla_kv_norm=mla_kv_norm,
        mla_w_uq_r=jnp.concatenate([w_uq[..., :NOPE].reshape(n_ab, Q_LORA, -1),
                                    w_uq[..., NOPE:].reshape(n_ab, Q_LORA, -1)], axis=-1),
        mla_w_ukv_r=jnp.concatenate([w_ukv[..., :NOPE].reshape(n_ab, KV_LORA, -1),
                                     w_ukv[..., NOPE:].reshape(n_ab, KV_LORA, -1)], axis=-1),
        ab_w_out=ab_w_out,
        c_w_in_pad=jnp.stack([_c_layout(c_w_in[i]) for i in range(n_c)]),
        gla_gate_w2=gla_gate_w2, gla_gate_b=gla_gate_b, gla_norm=gla_norm, c_w_out=c_w_out,
    )
    router_pad = _pad_cols(moe_router, LANES)

    cvecs = jnp.concatenate([c_ctx[None, :], c, jnp.zeros((8 - 1 - c.shape[0], D_MODEL), F32)], axis=0)
    mods = modulation_all(cvecs, w_mod, b_mod)
    n_lat = c.shape[0]

    rope_tabs = _axial_rope_tables(x_sample.shape[1])
    caches = (cache_mla_ckv, cache_mla_krope, state_rwkv, state_gla)
    bc, tc, _ = x_prompt.shape
    bl, tl, _ = x_sample.shape
    cap_c = CAP_FACTOR * tc // N_EXPERTS
    cap_l = CAP_FACTOR * tl // N_EXPERTS
    xc = x_prompt.reshape(bc * tc, D_MODEL)
    xl = x_sample.reshape(bl * tl, D_MODEL)
    outs = dict(ckv=[], kr=[], rwkv=[], gla=[])
    for l in range(DEPTH):
        mod_c = [t[:, None, :] for t in jnp.split(mods[l, 0:1], N_MOD, axis=-1)]
        mod_l = [t[:, None, :] for t in jnp.split(mods[l, 1:1 + n_lat], N_MOD, axis=-1)]
        xc = _mixer_layer(xc, l, mod_c, P, None, None, bc, tc, outs)
        xl = _mixer_layer(xl, l, mod_l, P, caches, rope_tabs, bl, tl, None)
        xc3 = xc.reshape(bc, tc, D_MODEL)
        xl3 = xl.reshape(bl, tl, D_MODEL)
        xe_c, rank_c, aff_c = moe_dispatch(xc3, g_norm_ffn[l], mod_c[4], mod_c[3], router_pad[l], cap_c)
        xe_l, rank_l, aff_l = moe_dispatch(xl3, g_norm_ffn[l], mod_l[4], mod_l[3], router_pad[l], cap_l)
        ye_c, ye_l = moe_experts(xe_c, xe_l, moe_w_gate, moe_w_up, moe_w_down, l)
        xc = moe_combine(xc3, mod_c[5], ye_c, rank_c, aff_c, cap_c).reshape(bc * tc, D_MODEL)
        xl = moe_combine(xl3, mod_l[5], ye_l, rank_l, aff_l, cap_l).reshape(bl * tl, D_MODEL)
    y_prompt = final_norm(xc, g_final).reshape(bc, tc, D_MODEL)
    y_sample = final_norm(xl, g_final).reshape(bl, tl, D_MODEL)
    return (y_prompt, y_sample, jnp.stack(outs['ckv'], axis=1), jnp.stack(outs['kr'], axis=1),
            jnp.stack(outs['rwkv'], axis=1), jnp.stack(outs['gla'], axis=1))
```

```python
import functools

import jax
import jax.numpy as jnp
from jax import lax
from jax.experimental import pallas as pl
from jax.experimental.pallas import tpu as pltpu

F32 = jnp.float32
BF16 = jnp.bfloat16
HIGHEST = lax.Precision.HIGHEST

VMEM_LIMIT_BYTES = 56 * 1024 * 1024
LANES = 128

D_MODEL = 2048
DEPTH = 4
GRID_W = 64
N_MOD = 6
NORM_EPS = 1e-6

A_WIDTH = 1024
A_HEAD = 64
A_HEADS = 16
W_LORA = 64
ICL_LORA = 64
G_LORA = 160
G_LORA_PAD = 256
GN_EPS = 64e-5

B_HEADS = 8
NOPE = 128
ROPE_D = 64
V_HEAD = 128
Q_LORA = 512
KV_LORA = 512
ROPE_BASE = 10000.0

C_HEADS = 4
C_QK = 1024
C_V = 2048
C_DK = 256
C_DV = 512
GATE_LORA = 16
GATE_TAU = 16.0
CHUNK = 64

N_EXPERTS = 16
D_EXPERT = 4096
CAP_FACTOR = 2

AB_PAD_COLS = 5120
AB_XG_BLK = 3072 // G_LORA_PAD
AB_XW_BLK = 3328 // LANES
AB_XA_BLK = 3456 // LANES
AB_CQ_BLK = 3584 // Q_LORA
AB_CKV_BLK = 4096 // KV_LORA
AB_KR_COL = 4608
C_PAD_COLS = 6656
C_XG_BLK = 6144 // LANES

TM = 1024


def _cparams(sem):
    return pltpu.CompilerParams(dimension_semantics=sem, vmem_limit_bytes=VMEM_LIMIT_BYTES)


def _sigmoid(x):
    return 1.0 / (1.0 + jnp.exp(-x))


def _softplus(x):
    return jnp.maximum(x, 0.0) + jnp.log(1.0 + jnp.exp(-jnp.abs(x)))


def _silu(x):
    return x * _sigmoid(x)


def _dot_01(a, b, m01_left=True):
    x = b if m01_left else a
    hi = x.astype(BF16)
    r1 = x - hi.astype(F32)
    mid = r1.astype(BF16)
    lo = (r1 - mid.astype(F32)).astype(BF16)
    if m01_left:
        return jnp.dot(jnp.concatenate([a, a, a], axis=1), jnp.concatenate([hi, mid, lo], axis=0),
                       preferred_element_type=F32)
    return jnp.dot(jnp.concatenate([hi, mid, lo], axis=1), jnp.concatenate([b, b, b], axis=0),
                   preferred_element_type=F32)


def _dot_x3(a, b, stack=True):
    ah = a.astype(BF16)
    al = (a - ah.astype(F32)).astype(BF16)
    bh = b.astype(BF16)
    bl = (b - bh.astype(F32)).astype(BF16)
    if stack:
        return jnp.dot(jnp.concatenate([ah, ah, al], axis=1), jnp.concatenate([bh, bl, bh], axis=0),
                       preferred_element_type=F32)
    return (jnp.dot(ah, bh, preferred_element_type=F32) + jnp.dot(ah, bl, preferred_element_type=F32)
            + jnp.dot(al, bh, preferred_element_type=F32))


def _mod_kernel(c_ref, w_ref, b_ref, o_ref):
    a = _silu(c_ref[...])
    o_ref[...] = _dot_x3(a, w_ref[...], stack=False) + b_ref[...]


def modulation_all(cvecs, w_mod, b_mod):
    tn = 1024
    n = w_mod.shape[-1]
    return pl.pallas_call(
        _mod_kernel,
        grid=(DEPTH, n // tn),
        in_specs=[
            pl.BlockSpec((8, D_MODEL), lambda l, j: (0, 0)),
            pl.BlockSpec((None, D_MODEL, tn), lambda l, j: (l, 0, j)),
            pl.BlockSpec((None, 1, tn), lambda l, j: (l, 0, j)),
        ],
        out_specs=pl.BlockSpec((None, 8, tn), lambda l, j: (l, 0, j)),
        out_shape=jax.ShapeDtypeStruct((DEPTH, 8, n), F32),
        compiler_params=_cparams(("parallel", "parallel")),
        name="modulation",
    )(cvecs, w_mod, b_mod.reshape(DEPTH, 1, n))


def _norm_mod(x, g, sc, sh):
    ms = jnp.mean(x * x, axis=-1, keepdims=True)
    y = x * lax.rsqrt(ms + NORM_EPS) * g
    return y * (1.0 + sc) + sh


def _nm_mm_kernel(x_ref, g_ref, sc_ref, sh_ref, w_ref, *rest, shift_t):
    if shift_t:
        mu_ref, o_ref, h_scr = rest
    else:
        o_ref, h_scr = rest

    @pl.when(pl.program_id(1) == 0)
    def _():
        h_scr[...] = _norm_mod(x_ref[...], g_ref[...], sc_ref[...], sh_ref[...]).astype(BF16)

    acc = jnp.dot(h_scr[...], w_ref[...].astype(BF16), preferred_element_type=F32)
    if shift_t:
        tm = acc.shape[0]
        row = lax.broadcasted_iota(jnp.int32, acc.shape, 0) & (shift_t - 1)
        prev = jnp.where(row == 0, 0.0, pltpu.roll(acc, 1, 0))
        nxt = jnp.where(row == shift_t - 1, 0.0, pltpu.roll(acc, tm - 1, 0))
        acc = acc + mu_ref[0:1, :] * (prev - acc) + mu_ref[1:2, :] * (nxt - acc)
    o_ref[...] = acc.astype(o_ref.dtype)


def nm_matmul(x, g, sc, sh, w, *, rows_per_mod, tn, mu=None, shift_t=0):
    m, d = x.shape
    n = w.shape[1]
    tpm = rows_per_mod // TM
    in_specs = [
        pl.BlockSpec((TM, d), lambda i, j: (i, 0)),
        pl.BlockSpec((1, d), lambda i, j: (0, 0)),
        pl.BlockSpec((None, 1, d), lambda i, j: (i // tpm, 0, 0)),
        pl.BlockSpec((None, 1, d), lambda i, j: (i // tpm, 0, 0)),
        pl.BlockSpec((d, tn), lambda i, j: (0, j)),
    ]
    args = [x, g.reshape(1, d), sc, sh, w]
    if shift_t:
        in_specs.append(pl.BlockSpec((2, tn), lambda i, j: (0, j)))
        args.append(mu)
    return pl.pallas_call(
        functools.partial(_nm_mm_kernel, shift_t=shift_t),
        grid=(m // TM, n // tn),
        in_specs=in_specs,
        out_specs=pl.BlockSpec((TM, tn), lambda i, j: (i, j)),
        out_shape=jax.ShapeDtypeStruct((m, n), F32),
        scratch_shapes=[pltpu.VMEM((TM, d), BF16)],
        compiler_params=_cparams(("parallel", "arbitrary")),
        name="norm_mod_matmul",
    )(*args)


def _rms_mm_kernel(x_ref, g_ref, w_ref, *out_refs, normalize, emit_norm):
    x = x_ref[...]
    if normalize:
        ms = jnp.mean(x * x, axis=-1, keepdims=True)
        x = x * lax.rsqrt(ms + NORM_EPS) * g_ref[...]
    if emit_norm:
        out_refs[1][...] = x
    out_refs[0][...] = jnp.dot(x.astype(BF16), w_ref[...].astype(BF16),
                               preferred_element_type=F32).astype(out_refs[0].dtype)


def rms_matmul(x, col_blk, k, g, w, *, normalize, emit_norm, out_dtype):
    m = x.shape[0]
    n = w.shape[1]
    tm = min(TM, m)
    out_shape = [jax.ShapeDtypeStruct((m, n), out_dtype)]
    out_specs = [pl.BlockSpec((tm, n), lambda i: (i, 0))]
    if emit_norm:
        out_shape.append(jax.ShapeDtypeStruct((m, k), F32))
        out_specs.append(pl.BlockSpec((tm, k), lambda i: (i, 0)))
    res = pl.pallas_call(
        functools.partial(_rms_mm_kernel, normalize=normalize, emit_norm=emit_norm),
        grid=(m // tm,),
        in_specs=[
            pl.BlockSpec((tm, k), lambda i: (i, col_blk)),
            pl.BlockSpec((1, k), lambda i: (0, 0)),
            pl.BlockSpec((k, n), lambda i: (0, 0)),
        ],
        out_specs=out_specs,
        out_shape=out_shape,
        compiler_params=_cparams(("parallel",)),
        name="rms_matmul",
    )(x, g.reshape(1, k), w)
    return res


def _res_mm_kernel(*refs, n_x):
    x_refs = refs[:n_x]
    w_refs = refs[n_x:2 * n_x]
    res_ref, gate_ref, o_ref = refs[2 * n_x:]
    acc = jnp.dot(x_refs[0][...], w_refs[0][...].astype(BF16), preferred_element_type=F32)
    for xr, wr in zip(x_refs[1:], w_refs[1:]):
        acc = acc + jnp.dot(xr[...], wr[...].astype(BF16), preferred_element_type=F32)
    o_ref[...] = res_ref[...] + gate_ref[...] * acc


def res_matmul(xs, w, res, gate, *, rows_per_mod, tn=512):
    m, n = res.shape
    tpm = rows_per_mod // TM
    in_specs, k0 = [], 0
    for xx in xs:
        in_specs.append(pl.BlockSpec((TM, xx.shape[1]), lambda i, j: (i, 0)))
    for xx in xs:
        kx = xx.shape[1]
        blk = k0 // kx
        in_specs.append(pl.BlockSpec((kx, tn), lambda i, j, blk=blk: (blk, j)))
        k0 += kx
    in_specs.append(pl.BlockSpec((TM, tn), lambda i, j: (i, j)))
    in_specs.append(pl.BlockSpec((None, 1, tn), lambda i, j: (i // tpm, 0, j)))
    return pl.pallas_call(
        functools.partial(_res_mm_kernel, n_x=len(xs)),
        grid=(m // TM, n // tn),
        in_specs=in_specs,
        out_specs=pl.BlockSpec((TM, tn), lambda i, j: (i, j)),
        out_shape=jax.ShapeDtypeStruct((m, n), F32),
        compiler_params=_cparams(("parallel", "parallel")),
        name="residual_matmul",
    )(*xs, *([w] * len(xs)), res, gate)


def _final_norm_kernel(x_ref, g_ref, o_ref):
    x = x_ref[...]
    ms = jnp.mean(x * x, axis=-1, keepdims=True)
    o_ref[...] = x * lax.rsqrt(ms + NORM_EPS) * g_ref[...]


def final_norm(x, g):
    m, d = x.shape
    return pl.pallas_call(
        _final_norm_kernel,
        grid=(m // TM,),
        in_specs=[pl.BlockSpec((TM, d), lambda i: (i, 0)), pl.BlockSpec((1, d), lambda i: (0, 0))],
        out_specs=pl.BlockSpec((TM, d), lambda i: (i, 0)),
        out_shape=jax.ShapeDtypeStruct((m, d), F32),
        compiler_params=_cparams(("parallel",)),
        name="final_norm",
    )(x, g.reshape(1, d))


def _head_pair_ones():
    r = lax.broadcasted_iota(jnp.int32, (LANES, LANES), 0) // A_HEAD
    c = lax.broadcasted_iota(jnp.int32, (LANES, LANES), 1) // A_HEAD
    return (r == c).astype(F32)


def _head_pair_eye():
    r = lax.broadcasted_iota(jnp.int32, (A_HEAD, LANES), 0)
    c = lax.broadcasted_iota(jnp.int32, (A_HEAD, LANES), 1) & (A_HEAD - 1)
    return (r == c).astype(F32)


RWKV_G = 4
RWKV_HPB = 2
RWKV_TB = 256
RWKV_W = RWKV_HPB * LANES
RWKV_MM_GROUPS = 2
RWKV_UNROLL = 4


def _head_sum(z, ones_bd):
    ones16 = ones_bd.astype(BF16)
    parts = [_dot_01(z[:, j * LANES:(j + 1) * LANES], ones16, m01_left=False) for j in range(z.shape[1] // LANES)]
    return parts[0] if len(parts) == 1 else jnp.concatenate(parts, axis=1)


def _rwkv_scan_kernel(*refs, has_s0):
    views = (refs[0:5], refs[5:10])
    w2_ref, w0_ref, a2_ref, a0_ref, kk_ref, ka_ref = refs[10:16]
    rest = refs[16:]
    if has_s0:
        s0_ref, rest = rest[0], rest[1:]
    yf_ref, yb_ref, sfin_ref, a_s, w_s, b_s, k_s, r_s, v_s, y_s, st_s = rest
    y_refs = (yf_ref, yb_ref)
    n_g, tb_len, _ = views[0][0].shape
    tb = pl.program_id(2)
    n_tb = pl.num_programs(2)
    ones_bd = _head_pair_ones()
    ones_bd16 = ones_bd.astype(BF16)
    eye2 = _head_pair_eye()

    def put(dst, d, g, val):
        for j in range(RWKV_HPB):
            dst[d, g, j] = val[:, j * LANES:(j + 1) * LANES]

    for d in range(2):
        r_ref, k_ref, v_ref, xw_ref, xa_ref = views[d]
        for g in range(n_g):
            kraw = k_ref[g]
            kk = kraw * kk_ref[...]
            kk = kk * lax.rsqrt(jnp.maximum(_head_sum(kk * kk, ones_bd), 1e-24))
            put(a_s, d, g, -kk)
            xw = xw_ref[g][:, d * W_LORA:(d + 1) * W_LORA]
            xa = xa_ref[g][:, d * ICL_LORA:(d + 1) * ICL_LORA]
            wl = w0_ref[d] + _dot_x3(jnp.tanh(xw), w2_ref[d])
            put(w_s, d, g, jnp.exp(-jnp.exp(-_softplus(-wl) - 0.5)))
            a = _sigmoid(a0_ref[d] + _dot_x3(xa, a2_ref[d]))
            put(k_s, d, g, kraw * (1.0 + (a - 1.0) * ka_ref[...]))
            put(b_s, d, g, kk * a)
            put(r_s, d, g, r_ref[g])
            put(v_s, d, g, v_ref[g])

    chains = [(d, g, j) for d in range(2) for g in range(n_g) for j in range(RWKV_HPB)]

    @pl.when(tb == 0)
    def _():
        for d, g, j in chains:
            if has_s0:
                st_s[d, g, j] = jnp.concatenate([s0_ref[g, d, 2 * j], s0_ref[g, d, 2 * j + 1]], axis=1)
            else:
                st_s[d, g, j] = jnp.zeros((A_HEAD, LANES), F32)

    group_len = len(chains) // RWKV_MM_GROUPS
    groups = [chains[i * group_len:(i + 1) * group_len] for i in range(RWKV_MM_GROUPS)]

    def step(t, carry):
        rows = [pl.ds(t if d == 0 else tb_len - 1 - t, 1) for d in range(2)]
        prev_rows = [pl.ds(jnp.maximum(t - 1, 0), 1), pl.ds(jnp.minimum(tb_len - t, tb_len - 1), 1)]
        for grp in groups:
            lhs = []
            for d, g, j in grp:
                s = st_s[d, g, j]
                lhs.append((s * a_s[d, g, j, rows[d], :]).astype(BF16))
                lhs.append((eye2 * v_s[d, g, j, rows[d], :]).astype(BF16))
                lhs.append((s * r_s[d, g, j, prev_rows[d], :]).astype(BF16))
            res = jnp.dot(jnp.concatenate(lhs, axis=0), ones_bd16, preferred_element_type=F32)
            for i, (d, g, j) in enumerate(grp):
                sa, vb, yb = (res[(3 * i + q) * A_HEAD:(3 * i + q + 1) * A_HEAD] for q in range(3))
                y_s[d, g, j, prev_rows[d], :] = jnp.sum(yb * eye2, axis=0, keepdims=True)
                row = rows[d]
                st_s[d, g, j] = (st_s[d, g, j] * w_s[d, g, j, row, :] + sa * b_s[d, g, j, row, :]
                                 + vb * k_s[d, g, j, row, :])
        return carry

    lax.fori_loop(0, tb_len, step, 0, unroll=RWKV_UNROLL)
    last_rows = (pl.ds(tb_len - 1, 1), pl.ds(0, 1))
    lhs = [(st_s[d, g, j] * r_s[d, g, j, last_rows[d], :]).astype(BF16) for d, g, j in chains]
    res = jnp.dot(jnp.concatenate(lhs, axis=0), ones_bd16, preferred_element_type=F32)
    for i, (d, g, j) in enumerate(chains):
        y_s[d, g, j, last_rows[d], :] = jnp.sum(res[i * A_HEAD:(i + 1) * A_HEAD] * eye2, axis=0, keepdims=True)
    for d, g, j in chains:
        y_refs[d][g, :, j * LANES:(j + 1) * LANES] = y_s[d, g, j]

    @pl.when(tb == n_tb - 1)
    def _():
        for d, g, j in chains:
            s = st_s[d, g, j]
            sfin_ref[g, d, 2 * j] = s[:, :A_HEAD]
            sfin_ref[g, d, 2 * j + 1] = s[:, A_HEAD:]


def rwkv_scan(u, p, s0, layer_idx):
    bsz, t_len, _ = u.shape
    n_tb = t_len // RWKV_TB
    n_hpg = A_WIDTH // RWKV_W
    fwd = lambda tb: tb
    bwd = lambda tb: n_tb - 1 - tb

    def view(tmap):
        def col(first_blk, width=RWKV_W, per_group=True):
            return pl.BlockSpec((RWKV_G, RWKV_TB, width),
                                lambda bi, hg, tb: (bi, tmap(tb), first_blk + (hg if per_group else 0)))
        return [col(0), col(n_hpg), col(2 * n_hpg),
                col(AB_XW_BLK, LANES, False), col(AB_XA_BLK, LANES, False)]

    in_specs = view(fwd) + view(bwd) + [
        pl.BlockSpec((2, W_LORA, RWKV_W), lambda bi, hg, tb: (0, 0, hg)),
        pl.BlockSpec((2, 1, RWKV_W), lambda bi, hg, tb: (0, 0, hg)),
        pl.BlockSpec((2, ICL_LORA, RWKV_W), lambda bi, hg, tb: (0, 0, hg)),
        pl.BlockSpec((2, 1, RWKV_W), lambda bi, hg, tb: (0, 0, hg)),
        pl.BlockSpec((1, RWKV_W), lambda bi, hg, tb: (0, hg)),
        pl.BlockSpec((1, RWKV_W), lambda bi, hg, tb: (0, hg)),
    ]
    args = [u] * 10 + [p['w2'], p['w0'].reshape(2, 1, A_WIDTH), p['a2'], p['a0'].reshape(2, 1, A_WIDTH),
                       p['k_k'].reshape(1, A_WIDTH), p['k_a'].reshape(1, A_WIDTH)]
    heads_blk = 2 * RWKV_HPB
    if s0 is not None:
        in_specs.append(pl.BlockSpec((RWKV_G, None, 2, heads_blk, A_HEAD, A_HEAD),
                                     lambda bi, hg, tb: (bi, layer_idx, 0, hg, 0, 0)))
        args.append(s0)
    blk = (2, RWKV_G, RWKV_HPB, RWKV_TB, LANES)
    return pl.pallas_call(
        functools.partial(_rwkv_scan_kernel, has_s0=s0 is not None),
        grid=(bsz // RWKV_G, n_hpg, n_tb),
        in_specs=in_specs,
        out_specs=[
            pl.BlockSpec((RWKV_G, RWKV_TB, RWKV_W), lambda bi, hg, tb: (bi, fwd(tb), hg)),
            pl.BlockSpec((RWKV_G, RWKV_TB, RWKV_W), lambda bi, hg, tb: (bi, bwd(tb), hg)),
            pl.BlockSpec((RWKV_G, 2, heads_blk, A_HEAD, A_HEAD), lambda bi, hg, tb: (bi, 0, hg, 0, 0)),
        ],
        out_shape=[
            jax.ShapeDtypeStruct((bsz, t_len, A_WIDTH), F32),
            jax.ShapeDtypeStruct((bsz, t_len, A_WIDTH), F32),
            jax.ShapeDtypeStruct((bsz, 2, A_HEADS, A_HEAD, A_HEAD), F32),
        ],
        scratch_shapes=[
            pltpu.VMEM(blk, F32),
            pltpu.VMEM(blk, F32),
            pltpu.VMEM(blk, F32),
            pltpu.VMEM(blk, F32),
            pltpu.VMEM(blk, F32),
            pltpu.VMEM(blk, F32),
            pltpu.VMEM(blk, F32),
            pltpu.VMEM((2, RWKV_G, RWKV_HPB, A_HEAD, LANES), F32),
        ],
        compiler_params=_cparams(("parallel", "parallel", "arbitrary")),
        name="rwkv7_scan",
    )(*args)


def _rwkv_post_kernel(yf_ref, yb_ref, r_ref, k_ref, v_ref, xa_ref, xg_ref,
                      a2_ref, a0_ref, ka_ref, rk_ref, gng_ref, gnb_ref, g2_ref, o_ref):
    ones_bd = _head_pair_ones()
    inv_n = 1.0 / A_HEAD
    y = yf_ref[...] + yb_ref[...]
    mu = _head_sum(y, ones_bd) * inv_n
    yc = y - mu
    var = _head_sum(yc * yc, ones_bd) * inv_n
    yn = yc * lax.rsqrt(var + GN_EPS) * gng_ref[...] + gnb_ref[...]
    r = r_ref[...]
    kraw = k_ref[...]
    rk_sum = jnp.zeros_like(r)
    for d in range(2):
        xa = xa_ref[:, d * ICL_LORA:(d + 1) * ICL_LORA]
        a = _sigmoid(a0_ref[d] + _dot_x3(xa, a2_ref[d]))
        rk_sum = rk_sum + r * (kraw * (1.0 + (a - 1.0) * ka_ref[...])) * rk_ref[...]
    bonus = _head_sum(rk_sum, ones_bd) * v_ref[...]
    gate = jnp.dot(_sigmoid(xg_ref[...]).astype(BF16), g2_ref[...].astype(BF16), preferred_element_type=F32)
    o_ref[...] = ((yn + bonus) * gate).astype(o_ref.dtype)


def rwkv_post(yf, yb, u2, p):
    m = yf.shape[0]
    tm = 256
    full = lambda shape: pl.BlockSpec(shape, lambda i: (0,) * len(shape))
    row = lambda width, blk: pl.BlockSpec((tm, width), lambda i: (i, blk))
    return pl.pallas_call(
        _rwkv_post_kernel,
        grid=(m // tm,),
        in_specs=[
            row(A_WIDTH, 0), row(A_WIDTH, 0),
            row(A_WIDTH, 0), row(A_WIDTH, 1), row(A_WIDTH, 2),
            row(LANES, AB_XA_BLK), row(G_LORA_PAD, AB_XG_BLK),
            full((2, ICL_LORA, A_WIDTH)), full((2, 1, A_WIDTH)),
            full((1, A_WIDTH)), full((1, A_WIDTH)), full((1, A_WIDTH)), full((1, A_WIDTH)),
            full((G_LORA_PAD, A_WIDTH)),
        ],
        out_specs=row(A_WIDTH, 0),
        out_shape=jax.ShapeDtypeStruct((m, A_WIDTH), BF16),
        compiler_params=_cparams(("parallel",)),
        name="rwkv7_post",
    )(yf, yb, u2, u2, u2, u2, u2, p['a2'], p['a0'].reshape(2, 1, A_WIDTH), p['k_a'].reshape(1, A_WIDTH),
      p['r_k'].reshape(1, A_WIDTH), p['gn_g'].reshape(1, A_WIDTH), p['gn_b'].reshape(1, A_WIDTH), p['g2'])


def rwkv_mix(u, p, s0, layer_idx):
    bsz, t_len, _ = u.shape
    m = bsz * t_len
    yf, yb, sfin = rwkv_scan(u, p, s0, layer_idx)
    out = rwkv_post(yf.reshape(m, A_WIDTH), yb.reshape(m, A_WIDTH), u.reshape(m, AB_PAD_COLS), p)
    return out, sfin


def _mla_kernel(qn_ref, qr_ref, kn_ref, kr_ref, v_ref, o_ref):
    scale = (NOPE + ROPE_D) ** -0.5
    krb = kr_ref[...].astype(BF16)
    nt = (((1,), (1,)), ((), ()))
    for h in range(B_HEADS):
        qn = qn_ref[:, h * NOPE:(h + 1) * NOPE].astype(BF16)
        qr = qr_ref[:, h * ROPE_D:(h + 1) * ROPE_D].astype(BF16)
        s = lax.dot_general(qn, kn_ref[:, h * NOPE:(h + 1) * NOPE], nt, preferred_element_type=F32)
        s = s + lax.dot_general(qr, krb, nt, preferred_element_type=F32)
        s = s * scale
        m = jnp.max(s, axis=-1, keepdims=True)
        e = jnp.exp(s - m)
        p = e / jnp.sum(e, axis=-1, keepdims=True)
        o = jnp.dot(p.astype(BF16), v_ref[:, h * V_HEAD:(h + 1) * V_HEAD], preferred_element_type=F32)
        o_ref[:, h * V_HEAD:(h + 1) * V_HEAD] = o.astype(o_ref.dtype)


def mla_attention(q, kv, kr, *, tq):
    bsz, t_len, _ = q.shape
    s_len = kv.shape[1]
    hn = B_HEADS * NOPE
    return pl.pallas_call(
        _mla_kernel,
        grid=(bsz, t_len // tq),
        in_specs=[
            pl.BlockSpec((None, tq, hn), lambda b, i: (b, i, 0)),
            pl.BlockSpec((None, tq, B_HEADS * ROPE_D), lambda b, i: (b, i, hn // (B_HEADS * ROPE_D))),
            pl.BlockSpec((None, s_len, hn), lambda b, i: (b, 0, 0)),
            pl.BlockSpec((None, s_len, ROPE_D), lambda b, i: (b, 0, 0)),
            pl.BlockSpec((None, s_len, hn), lambda b, i: (b, 0, 1)),
        ],
        out_specs=pl.BlockSpec((None, tq, hn), lambda b, i: (b, i, 0)),
        out_shape=jax.ShapeDtypeStruct((bsz, t_len, hn), BF16),
        compiler_params=_cparams(("parallel", "parallel")),
        name="mla_attention",
    )(q, q, kv, kr, kv)


GLA_SUPER = 4


def _gla_kernel(q_ref, k_ref, v_ref, og_ref, xg_ref, gw_ref, gb_ref, gn_ref, *rest, has_s0):
    if has_s0:
        s0_ref, rest = rest[0], rest[1:]
    o_ref, sfin_ref = rest[:2]
    lg_s, y_s, st_s, qt_s, u_s, dec_s = (rest[2 + 2 * i:4 + 2 * i] for i in range(6))
    t_len = q_ref.shape[0]
    n_chunks = t_len // CHUNK
    scale = C_DK ** -0.5
    nt = (((1,), (1,)), ((), ()))
    xg = xg_ref[...]
    span = GLA_SUPER * CHUNK
    ri = lax.broadcasted_iota(jnp.int32, (span, span), 0)
    ci = lax.broadcasted_iota(jnp.int32, (span, span), 1)
    same_chunk = (ri // CHUNK) == (ci // CHUNK)
    keeps = (same_chunk & (ri >= ci), same_chunk & (ri <= ci))
    tris = tuple(kp.astype(BF16) for kp in keeps)
    for d in range(2):
        z = _dot_x3(xg[:, d * GATE_LORA:(d + 1) * GATE_LORA], gw_ref[d], stack=False) + gb_ref[d]
        lg_s[d][...] = -_softplus(-z) * (1.0 / GATE_TAU)
        if has_s0:
            st_s[d][...] = s0_ref[d].T
        else:
            st_s[d][...] = jnp.zeros(st_s[d].shape, F32)

    def chunks_local(sc, carry):
        rows = pl.ds(pl.multiple_of(sc * span, span), span)
        q = q_ref[rows, :] * scale
        k = k_ref[rows, :]
        v = v_ref[rows, :]
        vb = v.astype(BF16)
        for d in range(2):
            b = _dot_01(tris[d], lg_s[d][rows, :])
            edge = CHUNK - 1 if d == 0 else 0
            bls = [b[cc * CHUNK + edge:cc * CHUNK + edge + 1, :] for cc in range(GLA_SUPER)]
            bl = jnp.concatenate([jnp.broadcast_to(x, (CHUNK, C_DK)) for x in bls], axis=0)
            qt = (q * jnp.exp(b)).astype(BF16)
            kt = (k * jnp.exp(-b)).astype(BF16)
            kd = (k * jnp.exp(bl - b)).astype(BF16)
            att = jnp.where(keeps[d], lax.dot_general(qt, kt, nt, preferred_element_type=F32), 0.0)
            y_s[d][rows, :] = jnp.dot(att.astype(BF16), vb, preferred_element_type=F32)
            qt_s[d][rows, :] = qt
            for cc in range(GLA_SUPER):
                part = slice(cc * CHUNK, (cc + 1) * CHUNK)
                u_s[d][sc * GLA_SUPER + cc] = jnp.dot(v[part].T.astype(BF16), kd[part], preferred_element_type=F32)
                dec_s[d][sc * GLA_SUPER + cc] = jnp.broadcast_to(jnp.exp(bls[cc]), (8, C_DK))
        return carry

    lax.fori_loop(0, n_chunks // GLA_SUPER, chunks_local, 0)

    def chunk_state(i, carry):
        for d in range(2):
            c = i if d == 0 else n_chunks - 1 - i
            rows = pl.ds(pl.multiple_of(c * CHUNK, CHUNK), CHUNK)
            st = st_s[d][...]
            y_s[d][rows, :] = y_s[d][rows, :] + lax.dot_general(qt_s[d][rows, :], st.astype(BF16), nt,
                                                                preferred_element_type=F32)
            st_s[d][...] = st * dec_s[d][c, 0:1, :] + u_s[d][c]
        return carry

    lax.fori_loop(0, n_chunks, chunk_state, 0)
    for d in range(2):
        sfin_ref[d] = st_s[d][...].T
    y = y_s[0][...] + y_s[1][...]
    yn = y * lax.rsqrt(jnp.mean(y * y, axis=-1, keepdims=True) + NORM_EPS) * gn_ref[...]
    o_ref[...] = (yn * _silu(og_ref[...])).astype(o_ref.dtype)


def gla_mix(u, p, s0, layer_idx):
    bsz, t_len, _ = u.shape
    in_specs = [
        pl.BlockSpec((None, t_len, C_DK), lambda b, h: (b, 0, h)),
        pl.BlockSpec((None, t_len, C_DK), lambda b, h: (b, 0, C_HEADS + h)),
        pl.BlockSpec((None, t_len, C_DV), lambda b, h: (b, 0, C_HEADS + h)),
        pl.BlockSpec((None, t_len, C_DV), lambda b, h: (b, 0, 2 * C_HEADS + h)),
        pl.BlockSpec((None, t_len, LANES), lambda b, h: (b, 0, C_XG_BLK)),
        pl.BlockSpec((2, GATE_LORA, C_DK), lambda b, h: (0, 0, h)),
        pl.BlockSpec((2, 1, C_DK), lambda b, h: (0, 0, h)),
        pl.BlockSpec((1, C_DV), lambda b, h: (0, 0)),
    ]
    args = [u, u, u, u, u, p['gate_w2'], p['gate_b'].reshape(2, 1, C_QK), p['gn'].reshape(1, C_DV)]
    if s0 is not None:
        in_specs.append(pl.BlockSpec((None, None, 2, None, C_DK, C_DV), lambda b, h: (b, layer_idx, 0, h, 0, 0)))
        args.append(s0)
    return pl.pallas_call(
        functools.partial(_gla_kernel, has_s0=s0 is not None),
        grid=(bsz, C_HEADS),
        in_specs=in_specs,
        out_specs=[
            pl.BlockSpec((None, t_len, C_DV), lambda b, h: (b, 0, h)),
            pl.BlockSpec((None, 2, None, C_DK, C_DV), lambda b, h: (b, 0, h, 0, 0)),
        ],
        out_shape=[
            jax.ShapeDtypeStruct((bsz, t_len, C_V), BF16),
            jax.ShapeDtypeStruct((bsz, 2, C_HEADS, C_DK, C_DV), F32),
        ],
        scratch_shapes=[
            pltpu.VMEM((t_len, C_DK), F32),
            pltpu.VMEM((t_len, C_DK), F32),
            pltpu.VMEM((t_len, C_DV), F32),
            pltpu.VMEM((t_len, C_DV), F32),
            pltpu.VMEM((C_DV, C_DK), F32),
            pltpu.VMEM((C_DV, C_DK), F32),
            pltpu.VMEM((t_len, C_DK), BF16),
            pltpu.VMEM((t_len, C_DK), BF16),
            pltpu.VMEM((t_len // CHUNK, C_DV, C_DK), F32),
            pltpu.VMEM((t_len // CHUNK, C_DV, C_DK), F32),
            pltpu.VMEM((t_len // CHUNK, 8, C_DK), F32),
            pltpu.VMEM((t_len // CHUNK, 8, C_DK), F32),
        ],
        compiler_params=_cparams(("parallel", "parallel")),
        name="gla_mix",
    )(*args)


DISPATCH_TN = 512


def _dispatch_kernel(x_ref, g_ref, sc_ref, sh_ref, wr_ref, xe_ref, rank_ref, aff_ref, hb_s, onehot_s, *, cap):
    t_len = x_ref.shape[0]
    j = pl.program_id(1)

    @pl.when(j == 0)
    def _():
        h = _norm_mod(x_ref[...], g_ref[...], sc_ref[...], sh_ref[...])
        for n in range(hb_s.shape[0]):
            hb_s[n] = h[:, n * DISPATCH_TN:(n + 1) * DISPATCH_TN].astype(BF16)
        logits = _dot_x3(h, wr_ref[...], stack=False)
        lane = lax.broadcasted_iota(jnp.int32, logits.shape, 1)
        logits = jnp.where(lane < N_EXPERTS, logits, -jnp.inf)
        m = jnp.max(logits, axis=-1, keepdims=True)
        e = jnp.exp(logits - m)
        aff = e / jnp.sum(e, axis=-1, keepdims=True)
        aff_ref[...] = aff
        aff_t = aff.T
        sub_i = lax.broadcasted_iota(jnp.int32, (t_len, t_len), 0)
        lane_i = lax.broadcasted_iota(jnp.int32, (t_len, t_len), 1)
        slot = lax.broadcasted_iota(jnp.int32, (cap, t_len), 0).astype(F32)
        expert_row = lax.broadcasted_iota(jnp.int32, (N_EXPERTS, t_len), 0)
        ranks = jnp.zeros((N_EXPERTS, t_len), F32)
        for ex in range(N_EXPERTS):
            col = aff[:, ex:ex + 1]
            row = aff_t[ex:ex + 1, :]
            ahead = (col > row) | ((col == row) & (sub_i < lane_i))
            rank_row = jnp.sum(ahead.astype(F32), axis=0, keepdims=True)
            onehot_s[ex * cap:(ex + 1) * cap, :] = (rank_row == slot).astype(BF16)
            ranks = jnp.where(expert_row == ex, rank_row, ranks)
        pad = jnp.zeros((LANES - N_EXPERTS, t_len), F32)
        rank_ref[...] = jnp.concatenate([ranks, pad], axis=0).T

    rows = jnp.dot(onehot_s[...], hb_s[j], preferred_element_type=F32).astype(BF16)
    for ex in range(N_EXPERTS):
        xe_ref[ex] = rows[ex * cap:(ex + 1) * cap]


def moe_dispatch(x, g, sc, sh, w_router_pad, cap):
    bsz, t_len, d = x.shape
    per_req = sc.shape[0] > 1
    mod_spec = pl.BlockSpec((None, 1, d), (lambda b, j: (b, 0, 0)) if per_req else (lambda b, j: (0, 0, 0)))
    return pl.pallas_call(
        functools.partial(_dispatch_kernel, cap=cap),
        grid=(bsz, d // DISPATCH_TN),
        in_specs=[
            pl.BlockSpec((None, t_len, d), lambda b, j: (b, 0, 0)),
            pl.BlockSpec((1, d), lambda b, j: (0, 0)),
            mod_spec, mod_spec,
            pl.BlockSpec((d, LANES), lambda b, j: (0, 0)),
        ],
        out_specs=[
            pl.BlockSpec((N_EXPERTS, cap, DISPATCH_TN), lambda b, j: (0, b, j)),
            pl.BlockSpec((None, t_len, LANES), lambda b, j: (b, 0, 0)),
            pl.BlockSpec((None, t_len, LANES), lambda b, j: (b, 0, 0)),
        ],
        out_shape=[
            jax.ShapeDtypeStruct((N_EXPERTS, bsz * cap, d), BF16),
            jax.ShapeDtypeStruct((bsz, t_len, LANES), F32),
            jax.ShapeDtypeStruct((bsz, t_len, LANES), F32),
        ],
        scratch_shapes=[
            pltpu.VMEM((d // DISPATCH_TN, t_len, DISPATCH_TN), BF16),
            pltpu.VMEM((N_EXPERTS * cap, t_len), BF16),
        ],
        compiler_params=_cparams(("parallel", "arbitrary")),
        name="moe_dispatch",
    )(x, g.reshape(1, d), sc, sh, w_router_pad)


EXPERT_TF = 256
EXPERT_TN = 256
N_UP_STEPS = D_EXPERT // EXPERT_TF
N_DOWN_STEPS = D_MODEL // EXPERT_TN


def _experts_kernel(xc_ref, xl_ref, wg_ref, wu_ref, wd_ref, yc_ref, yl_ref, hid_s):
    s = pl.program_id(1)
    mc = xc_ref.shape[0]

    @pl.when(s < N_UP_STEPS)
    def _():
        wg = wg_ref[...].astype(BF16)
        wu = wu_ref[...].astype(BF16)
        cols = pl.ds(pl.multiple_of(s * EXPERT_TF, EXPERT_TF), EXPERT_TF)
        for x_ref, r0 in ((xc_ref, 0), (xl_ref, mc)):
            x = x_ref[...]
            hg = jnp.dot(x, wg, preferred_element_type=F32)
            hu = jnp.dot(x, wu, preferred_element_type=F32)
            hid_s[r0:r0 + x.shape[0], cols] = (_silu(hg) * hu).astype(BF16)

    @pl.when(s >= N_UP_STEPS)
    def _():
        wd = wd_ref[...].astype(BF16)
        yc_ref[...] = jnp.dot(hid_s[0:mc, :], wd, preferred_element_type=F32).astype(yc_ref.dtype)
        yl_ref[...] = jnp.dot(hid_s[mc:, :], wd, preferred_element_type=F32).astype(yl_ref.dtype)


def moe_experts(xc, xl, w_gate, w_up, w_down, layer):
    mc, ml = xc.shape[1], xl.shape[1]
    up_idx = lambda e, s: (layer, e, 0, jnp.minimum(s, N_UP_STEPS - 1))
    down_idx = lambda e, s: (layer, e, 0, jnp.maximum(s - N_UP_STEPS, 0))
    out_idx = lambda e, s: (e, 0, jnp.maximum(s - N_UP_STEPS, 0))
    return pl.pallas_call(
        _experts_kernel,
        grid=(N_EXPERTS, N_UP_STEPS + N_DOWN_STEPS),
        in_specs=[
            pl.BlockSpec((None, mc, D_MODEL), lambda e, s: (e, 0, 0)),
            pl.BlockSpec((None, ml, D_MODEL), lambda e, s: (e, 0, 0)),
            pl.BlockSpec((None, None, D_MODEL, EXPERT_TF), up_idx),
            pl.BlockSpec((None, None, D_MODEL, EXPERT_TF), up_idx),
            pl.BlockSpec((None, None, D_EXPERT, EXPERT_TN), down_idx),
        ],
        out_specs=[
            pl.BlockSpec((None, mc, EXPERT_TN), out_idx),
            pl.BlockSpec((None, ml, EXPERT_TN), out_idx),
        ],
        out_shape=[
            jax.ShapeDtypeStruct((N_EXPERTS, mc, D_MODEL), BF16),
            jax.ShapeDtypeStruct((N_EXPERTS, ml, D_MODEL), BF16),
        ],
        scratch_shapes=[pltpu.VMEM((mc + ml, D_EXPERT), BF16)],
        compiler_params=_cparams(("parallel", "arbitrary")),
        name="moe_experts",
    )(xc, xl, w_gate, w_up, w_down)


def _combine_kernel(x_ref, gate_ref, ye_ref, rank_ref, aff_ref, o_ref, *, cap):
    t_len = x_ref.shape[0]
    slot = lax.broadcasted_iota(jnp.int32, (t_len, cap), 1).astype(F32)
    acc = jnp.zeros(x_ref.shape, F32)
    for ex in range(N_EXPERTS):
        onehot = (rank_ref[:, ex:ex + 1] == slot).astype(BF16)
        acc = acc + aff_ref[:, ex:ex + 1] * jnp.dot(onehot, ye_ref[ex], preferred_element_type=F32)
    o_ref[...] = x_ref[...] + gate_ref[...] * acc


def moe_combine(x, gate, ye, rank, aff, cap):
    bsz, t_len, d = x.shape
    per_req = gate.shape[0] > 1
    tn = 512
    return pl.pallas_call(
        functools.partial(_combine_kernel, cap=cap),
        grid=(bsz, d // tn),
        in_specs=[
            pl.BlockSpec((None, t_len, tn), lambda b, j: (b, 0, j)),
            pl.BlockSpec((None, 1, tn), (lambda b, j: (b, 0, j)) if per_req else (lambda b, j: (0, 0, j))),
            pl.BlockSpec((N_EXPERTS, cap, tn), lambda b, j: (0, b, j)),
            pl.BlockSpec((None, t_len, LANES), lambda b, j: (b, 0, 0)),
            pl.BlockSpec((None, t_len, LANES), lambda b, j: (b, 0, 0)),
        ],
        out_specs=pl.BlockSpec((None, t_len, tn), lambda b, j: (b, 0, j)),
        out_shape=jax.ShapeDtypeStruct((bsz, t_len, d), F32),
        compiler_params=_cparams(("parallel", "parallel")),
        name="moe_combine",
    )(x, gate, ye, rank, aff)


def _pad_cols(w, n):
    return jnp.pad(w, ((0, 0),) * (w.ndim - 1) + ((0, n - w.shape[-1]),))


def _ab_layout(w):
    rkv = w[..., :3 * A_WIDTH]
    o = 3 * A_WIDTH
    xg = _pad_cols(w[..., o:o + G_LORA], G_LORA_PAD)
    o += G_LORA
    xw = w[..., o:o + 2 * W_LORA]
    o += 2 * W_LORA
    xa = w[..., o:o + 2 * ICL_LORA]
    o += 2 * ICL_LORA
    rest = w[..., o:]
    return _pad_cols(jnp.concatenate([rkv, xg, xw, xa, rest], axis=-1), AB_PAD_COLS)


def _c_layout(w):
    qkv = w[:, :2 * C_QK + C_V]
    o = 2 * C_QK + C_V
    xg = _pad_cols(w[:, o:o + 2 * GATE_LORA], LANES)
    og = w[:, o + 2 * GATE_LORA:]
    return _pad_cols(jnp.concatenate([qkv, og, xg], axis=-1), C_PAD_COLS)


def _axial_rope_tables(t):
    rows = t // GRID_W
    row = jnp.repeat(jnp.arange(rows, dtype=F32), GRID_W)
    col = jnp.tile(jnp.arange(GRID_W, dtype=F32), rows)
    half = ROPE_D // 2
    inv = 1.0 / (ROPE_BASE ** (jnp.arange(0, half, 2, dtype=F32) / half))
    ang_r = row[:, None] * inv[None, :]
    ang_c = col[:, None] * inv[None, :]
    return (jnp.cos(ang_r), jnp.sin(ang_r), jnp.cos(ang_c), jnp.sin(ang_c))


def _rotate(x, cos, sin):
    m = x.shape[-1] // 2
    x1, x2 = x[..., :m], x[..., m:]
    c = cos[None, :, None, :]
    s = sin[None, :, None, :]
    return jnp.concatenate([x1 * c - x2 * s, x1 * s + x2 * c], axis=-1)


def _apply_axial_rope(x, tabs):
    cr, sr, cc, sc = tabs
    half = ROPE_D // 2
    return jnp.concatenate([_rotate(x[..., :half], cr, sr), _rotate(x[..., half:], cc, sc)], axis=-1)


def _mixer_layer(x2, l, mod_l, P, caches, rope_tabs, bsz, t_len, outs):
    m = bsz * t_len
    is_ctx = caches is None
    rows_per_mod = m if is_ctx else t_len
    sh1, sc1, g1 = mod_l[0], mod_l[1], mod_l[2]
    i = l // 2
    if l % 2 == 0:
        u = nm_matmul(x2, P['g_norm_mix'][l], sc1, sh1, P['ab_w_in_pad'][i], rows_per_mod=rows_per_mod,
                      tn=512, mu=P['rwkv_mu_pad'][i], shift_t=t_len)
        p = dict(w2=P['rwkv_w2'][i], w0=P['rwkv_w0'][i], a2=P['rwkv_a2'][i], a0=P['rwkv_a0'][i],
                 g2=P['rwkv_g2_pad'][i], k_k=P['rwkv_k_k'][i], k_a=P['rwkv_k_a'][i], r_k=P['rwkv_r_k'][i],
                 gn_g=P['rwkv_gn_g'][i], gn_b=P['rwkv_gn_b'][i])
        u3 = u.reshape(bsz, t_len, AB_PAD_COLS)
        a_out, a_state = rwkv_mix(u3, p, None if is_ctx else caches[2], i)
        (q,) = rms_matmul(u, AB_CQ_BLK, Q_LORA, P['mla_q_norm'][i], P['mla_w_uq_r'][i],
                          normalize=True, emit_norm=False, out_dtype=F32)
        kv, ckv = rms_matmul(u, AB_CKV_BLK, KV_LORA, P['mla_kv_norm'][i], P['mla_w_ukv_r'][i],
                             normalize=True, emit_norm=True, out_dtype=BF16)
        kr = u3[:, :, AB_KR_COL:AB_KR_COL + ROPE_D]
        q = q.reshape(bsz, t_len, -1)
        kv = kv.reshape(bsz, t_len, -1)
        if is_ctx:
            outs['ckv'].append(ckv.reshape(bsz, t_len, KV_LORA))
            outs['kr'].append(kr)
            outs['rwkv'].append(a_state)
            keys_r = kr
        else:
            hn = B_HEADS * NOPE
            q_rope = _apply_axial_rope(q[..., hn:].reshape(bsz, t_len, B_HEADS, ROPE_D), rope_tabs)
            q = jnp.concatenate([q[..., :hn], q_rope.reshape(bsz, t_len, B_HEADS * ROPE_D)], axis=-1)
            kr = _apply_axial_rope(kr[:, :, None, :], rope_tabs)[:, :, 0, :]
            past = caches[0].shape[2]
            (kv_ctx,) = rms_matmul(caches[0][:, i].reshape(bsz * past, KV_LORA), 0, KV_LORA,
                                   P['mla_kv_norm'][i], P['mla_w_ukv_r'][i],
                                   normalize=False, emit_norm=False, out_dtype=BF16)
            kv = jnp.concatenate([kv, kv_ctx.reshape(bsz, past, -1)], axis=1)
            keys_r = jnp.concatenate([kr, caches[1][:, i]], axis=1)
        b_out = mla_attention(q, kv, keys_r, tq=256)
        return res_matmul([a_out, b_out.reshape(m, A_WIDTH)], P['ab_w_out'][i], x2, g1,
                          rows_per_mod=rows_per_mod)
    u = nm_matmul(x2, P['g_norm_mix'][l], sc1, sh1, P['c_w_in_pad'][i], rows_per_mod=rows_per_mod, tn=512)
    p = dict(gate_w2=P['gla_gate_w2'][i], gate_b=P['gla_gate_b'][i], gn=P['gla_norm'][i])
    o, st = gla_mix(u.reshape(bsz, t_len, C_PAD_COLS), p, None if is_ctx else caches[3], i)
    if is_ctx:
        outs['gla'].append(st)
    return res_matmul([o.reshape(m, C_V)], P['c_w_out'][i], x2, g1, rows_per_mod=rows_per_mod)


def kernel(x_prompt, x_sample, cache_mla_ckv, cache_mla_krope, state_rwkv, state_gla, c, c_ctx, w_mod, b_mod, g_norm_mix, g_norm_ffn, ab_w_in, rwkv_mu, rwkv_w0, rwkv_w2, rwkv_a0, rwkv_a2, rwkv_g2, rwkv_k_k, rwkv_k_a, rwkv_r_k, rwkv_gn_g, rwkv_gn_b, mla_q_norm, mla_w_uq, mla_kv_norm, mla_w_ukv, ab_w_out, c_w_in, gla_gate_w2, gla_gate_b, gla_norm, c_w_out, moe_router, moe_w_gate, moe_w_up, moe_w_down, g_final):
    n_ab = ab_w_in.shape[0]
    n_c = c_w_in.shape[0]
    w_uq = mla_w_uq.reshape(n_ab, Q_LORA, B_HEADS, NOPE + ROPE_D)
    w_ukv = mla_w_ukv.reshape(n_ab, KV_LORA, B_HEADS, NOPE + V_HEAD)
    P = dict(
        g_norm_mix=g_norm_mix, g_norm_ffn=g_norm_ffn, g_final=g_final,
        ab_w_in_pad=_ab_layout(ab_w_in), rwkv_mu_pad=_ab_layout(rwkv_mu),
        rwkv_w0=rwkv_w0, rwkv_w2=rwkv_w2, rwkv_a0=rwkv_a0, rwkv_a2=rwkv_a2,
        rwkv_g2_pad=jnp.pad(rwkv_g2, ((0, 0), (0, G_LORA_PAD - G_LORA), (0, 0))),
        rwkv_k_k=rwkv_k_k, rwkv_k_a=rwkv_k_a, rwkv_r_k=rwkv_r_k.reshape(n_ab, A_WIDTH),
        rwkv_gn_g=rwkv_gn_g, rwkv_gn_b=rwkv_gn_b,
        mla_q_norm=mla_q_norm, mla_kv_norm=mla_kv_norm,
        mla_w_uq_r=jnp.concatenate([w_uq[..., :NOPE].reshape(n_ab, Q_LORA, -1),
                                    w_uq[..., NOPE:].reshape(n_ab, Q_LORA, -1)], axis=-1),
        mla_w_ukv_r=jnp.concatenate([w_ukv[..., :NOPE].reshape(n_ab, KV_LORA, -1),
                                     w_ukv[..., NOPE:].reshape(n_ab, KV_LORA, -1)], axis=-1),
        ab_w_out=ab_w_out,
        c_w_in_pad=jnp.stack([_c_layout(c_w_in[i]) for i in range(n_c)]),
        gla_gate_w2=gla_gate_w2, gla_gate_b=gla_gate_b, gla_norm=gla_norm, c_w_out=c_w_out,
    )
    router_pad = _pad_cols(moe_router, LANES)

    cvecs = jnp.concatenate([c_ctx[None, :], c, jnp.zeros((8 - 1 - c.shape[0], D_MODEL), F32)], axis=0)
    mods = modulation_all(cvecs, w_mod, b_mod)
    n_lat = c.shape[0]

    rope_tabs = _axial_rope_tables(x_sample.shape[1])
    caches = (cache_mla_ckv, cache_mla_krope, state_rwkv, state_gla)
    bc, tc, _ = x_prompt.shape
    bl, tl, _ = x_sample.shape
    cap_c = CAP_FACTOR * tc // N_EXPERTS
    cap_l = CAP_FACTOR * tl // N_EXPERTS
    xc = x_prompt.reshape(bc * tc, D_MODEL)
    xl = x_sample.reshape(bl * tl, D_MODEL)
    outs = dict(ckv=[], kr=[], rwkv=[], gla=[])
    for l in range(DEPTH):
        mod_c = [t[:, None, :] for t in jnp.split(mods[l, 0:1], N_MOD, axis=-1)]
        mod_l = [t[:, None, :] for t in jnp.split(mods[l, 1:1 + n_lat], N_MOD, axis=-1)]
        xc = _mixer_layer(xc, l, mod_c, P, None, None, bc, tc, outs)
        xl = _mixer_layer(xl, l, mod_l, P, caches, rope_tabs, bl, tl, None)
        xc3 = xc.reshape(bc, tc, D_MODEL)
        xl3 = xl.reshape(bl, tl, D_MODEL)
        xe_c, rank_c, aff_c = moe_dispatch(xc3, g_norm_ffn[l], mod_c[4], mod_c[3], router_pad[l], cap_c)
        xe_l, rank_l, aff_l = moe_dispatch(xl3, g_norm_ffn[l], mod_l[4], mod_l[3], router_pad[l], cap_l)
        ye_c, ye_l = moe_experts(xe_c, xe_l, moe_w_gate, moe_w_up, moe_w_down, l)
        xc = moe_combine(xc3, mod_c[5], ye_c, rank_c, aff_c, cap_c).reshape(bc * tc, D_MODEL)
        xl = moe_combine(xl3, mod_l[5], ye_l, rank_l, aff_l, cap_l).reshape(bl * tl, D_MODEL)
    y_prompt = final_norm(xc, g_final).reshape(bc, tc, D_MODEL)
    y_sample = final_norm(xl, g_final).reshape(bl, tl, D_MODEL)
    return (y_prompt, y_sample, jnp.stack(outs['ckv'], axis=1), jnp.stack(outs['kr'], axis=1),
            jnp.stack(outs['rwkv'], axis=1), jnp.stack(outs['gla'], axis=1))
```

```python
import functools

import jax
import jax.numpy as jnp
from jax import lax
from jax.experimental import pallas as pl
from jax.experimental.pallas import tpu as pltpu

F32 = jnp.float32
BF16 = jnp.bfloat16
HIGHEST = lax.Precision.HIGHEST

VMEM_LIMIT_BYTES = 56 * 1024 * 1024
LANES = 128

D_MODEL = 2048
DEPTH = 4
GRID_W = 64
N_MOD = 6
NORM_EPS = 1e-6

A_WIDTH = 1024
A_HEAD = 64
A_HEADS = 16
W_LORA = 64
ICL_LORA = 64
G_LORA = 160
G_LORA_PAD = 256
GN_EPS = 64e-5

B_HEADS = 8
NOPE = 128
ROPE_D = 64
V_HEAD = 128
Q_LORA = 512
KV_LORA = 512
ROPE_BASE = 10000.0

C_HEADS = 4
C_QK = 1024
C_V = 2048
C_DK = 256
C_DV = 512
GATE_LORA = 16
GATE_TAU = 16.0
CHUNK = 64

N_EXPERTS = 16
D_EXPERT = 4096
CAP_FACTOR = 2

AB_PAD_COLS = 5120
AB_XG_BLK = 3072 // G_LORA_PAD
AB_XW_BLK = 3328 // LANES
AB_XA_BLK = 3456 // LANES
AB_CQ_BLK = 3584 // Q_LORA
AB_CKV_BLK = 4096 // KV_LORA
AB_KR_COL = 4608
AB_SHIFT_COLS = 3584
C_PAD_COLS = 6656
C_XG_BLK = 6144 // LANES

TM = 1024


def _cparams(sem):
    return pltpu.CompilerParams(dimension_semantics=sem, vmem_limit_bytes=VMEM_LIMIT_BYTES)


def _sigmoid(x):
    return 1.0 / (1.0 + jnp.exp(-x))


def _softplus(x):
    return jnp.maximum(x, 0.0) + jnp.log(1.0 + jnp.exp(-jnp.abs(x)))


def _silu(x):
    return x * _sigmoid(x)


def _dot_01(a, b, m01_left=True):
    x = b if m01_left else a
    hi = x.astype(BF16)
    r1 = x - hi.astype(F32)
    mid = r1.astype(BF16)
    lo = (r1 - mid.astype(F32)).astype(BF16)
    if m01_left:
        return jnp.dot(jnp.concatenate([a, a, a], axis=1), jnp.concatenate([hi, mid, lo], axis=0),
                       preferred_element_type=F32)
    return jnp.dot(jnp.concatenate([hi, mid, lo], axis=1), jnp.concatenate([b, b, b], axis=0),
                   preferred_element_type=F32)


def _dot_x3(a, b, stack=True):
    ah = a.astype(BF16)
    al = (a - ah.astype(F32)).astype(BF16)
    bh = b.astype(BF16)
    bl = (b - bh.astype(F32)).astype(BF16)
    if stack:
        return jnp.dot(jnp.concatenate([ah, ah, al], axis=1), jnp.concatenate([bh, bl, bh], axis=0),
                       preferred_element_type=F32)
    return (jnp.dot(ah, bh, preferred_element_type=F32) + jnp.dot(ah, bl, preferred_element_type=F32)
            + jnp.dot(al, bh, preferred_element_type=F32))


def _mod_kernel(c_ref, w_ref, b_ref, o_ref):
    a = _silu(c_ref[...])
    o_ref[...] = _dot_x3(a, w_ref[...], stack=False) + b_ref[...]


def modulation_all(cvecs, w_mod, b_mod):
    tn = 1024
    n = w_mod.shape[-1]
    return pl.pallas_call(
        _mod_kernel,
        grid=(DEPTH, n // tn),
        in_specs=[
            pl.BlockSpec((8, D_MODEL), lambda l, j: (0, 0)),
            pl.BlockSpec((None, D_MODEL, tn), lambda l, j: (l, 0, j)),
            pl.BlockSpec((None, 1, tn), lambda l, j: (l, 0, j)),
        ],
        out_specs=pl.BlockSpec((None, 8, tn), lambda l, j: (l, 0, j)),
        out_shape=jax.ShapeDtypeStruct((DEPTH, 8, n), F32),
        compiler_params=_cparams(("parallel", "parallel")),
        name="modulation",
    )(cvecs, w_mod, b_mod.reshape(DEPTH, 1, n))


def _norm_mod(x, g, sc, sh):
    ms = jnp.mean(x * x, axis=-1, keepdims=True)
    y = x * lax.rsqrt(ms + NORM_EPS) * g
    return y * (1.0 + sc) + sh


def _nm_mm_kernel(x_ref, g_ref, sc_ref, sh_ref, w_ref, *rest, shift_t):
    if shift_t:
        mu_ref, o_ref, h_scr = rest
    else:
        o_ref, h_scr = rest

    @pl.when(pl.program_id(1) == 0)
    def _():
        h_scr[...] = _norm_mod(x_ref[...], g_ref[...], sc_ref[...], sh_ref[...]).astype(BF16)

    acc = jnp.dot(h_scr[...], w_ref[...].astype(BF16), preferred_element_type=F32)
    if not shift_t:
        o_ref[...] = acc
        return
    tn = acc.shape[1]
    shifted_tiles = -(-AB_SHIFT_COLS // tn)

    @pl.when(pl.program_id(1) < shifted_tiles)
    def _():
        tm = acc.shape[0]
        row = lax.broadcasted_iota(jnp.int32, acc.shape, 0) & (shift_t - 1)
        prev = jnp.where(row == 0, 0.0, pltpu.roll(acc, 1, 0))
        nxt = jnp.where(row == shift_t - 1, 0.0, pltpu.roll(acc, tm - 1, 0))
        o_ref[...] = acc + mu_ref[0:1, :] * (prev - acc) + mu_ref[1:2, :] * (nxt - acc)

    @pl.when(pl.program_id(1) >= shifted_tiles)
    def _():
        o_ref[...] = acc


def nm_matmul(x, g, sc, sh, w, *, rows_per_mod, tn, mu=None, shift_t=0):
    m, d = x.shape
    n = w.shape[1]
    tpm = rows_per_mod // TM
    in_specs = [
        pl.BlockSpec((TM, d), lambda i, j: (i, 0)),
        pl.BlockSpec((1, d), lambda i, j: (0, 0)),
        pl.BlockSpec((None, 1, d), lambda i, j: (i // tpm, 0, 0)),
        pl.BlockSpec((None, 1, d), lambda i, j: (i // tpm, 0, 0)),
        pl.BlockSpec((d, tn), lambda i, j: (0, j)),
    ]
    args = [x, g.reshape(1, d), sc, sh, w]
    if shift_t:
        in_specs.append(pl.BlockSpec((2, tn), lambda i, j: (0, j)))
        args.append(mu)
    return pl.pallas_call(
        functools.partial(_nm_mm_kernel, shift_t=shift_t),
        grid=(m // TM, n // tn),
        in_specs=in_specs,
        out_specs=pl.BlockSpec((TM, tn), lambda i, j: (i, j)),
        out_shape=jax.ShapeDtypeStruct((m, n), F32),
        scratch_shapes=[pltpu.VMEM((TM, d), BF16)],
        compiler_params=_cparams(("parallel", "arbitrary")),
        name="norm_mod_matmul",
    )(*args)


def _rms_mm_kernel(x_ref, g_ref, w_ref, *out_refs, normalize, emit_norm):
    x = x_ref[...]
    if normalize:
        ms = jnp.mean(x * x, axis=-1, keepdims=True)
        x = x * lax.rsqrt(ms + NORM_EPS) * g_ref[...]
    if emit_norm:
        out_refs[1][...] = x
    out_refs[0][...] = jnp.dot(x.astype(BF16), w_ref[...].astype(BF16),
                               preferred_element_type=F32).astype(out_refs[0].dtype)


def rms_matmul(x, col_blk, k, g, w, *, normalize, emit_norm, out_dtype):
    m = x.shape[0]
    n = w.shape[1]
    tm = min(TM, m)
    out_shape = [jax.ShapeDtypeStruct((m, n), out_dtype)]
    out_specs = [pl.BlockSpec((tm, n), lambda i: (i, 0))]
    if emit_norm:
        out_shape.append(jax.ShapeDtypeStruct((m, k), F32))
        out_specs.append(pl.BlockSpec((tm, k), lambda i: (i, 0)))
    res = pl.pallas_call(
        functools.partial(_rms_mm_kernel, normalize=normalize, emit_norm=emit_norm),
        grid=(m // tm,),
        in_specs=[
            pl.BlockSpec((tm, k), lambda i: (i, col_blk)),
            pl.BlockSpec((1, k), lambda i: (0, 0)),
            pl.BlockSpec((k, n), lambda i: (0, 0)),
        ],
        out_specs=out_specs,
        out_shape=out_shape,
        compiler_params=_cparams(("parallel",)),
        name="rms_matmul",
    )(x, g.reshape(1, k), w)
    return res


def _res_mm_kernel(*refs, n_x):
    x_refs = refs[:n_x]
    w_refs = refs[n_x:2 * n_x]
    res_ref, gate_ref, o_ref = refs[2 * n_x:]
    acc = jnp.dot(x_refs[0][...], w_refs[0][...].astype(BF16), preferred_element_type=F32)
    for xr, wr in zip(x_refs[1:], w_refs[1:]):
        acc = acc + jnp.dot(xr[...], wr[...].astype(BF16), preferred_element_type=F32)
    o_ref[...] = res_ref[...] + gate_ref[...] * acc


def res_matmul(xs, w, res, gate, *, rows_per_mod, tn=512):
    m, n = res.shape
    tpm = rows_per_mod // TM
    in_specs, k0 = [], 0
    for xx in xs:
        in_specs.append(pl.BlockSpec((TM, xx.shape[1]), lambda i, j: (i, 0)))
    for xx in xs:
        kx = xx.shape[1]
        blk = k0 // kx
        in_specs.append(pl.BlockSpec((kx, tn), lambda i, j, blk=blk: (blk, j)))
        k0 += kx
    in_specs.append(pl.BlockSpec((TM, tn), lambda i, j: (i, j)))
    in_specs.append(pl.BlockSpec((None, 1, tn), lambda i, j: (i // tpm, 0, j)))
    return pl.pallas_call(
        functools.partial(_res_mm_kernel, n_x=len(xs)),
        grid=(m // TM, n // tn),
        in_specs=in_specs,
        out_specs=pl.BlockSpec((TM, tn), lambda i, j: (i, j)),
        out_shape=jax.ShapeDtypeStruct((m, n), F32),
        compiler_params=_cparams(("parallel", "parallel")),
        name="residual_matmul",
    )(*xs, *([w] * len(xs)), res, gate)


def _final_norm_kernel(x_ref, g_ref, o_ref):
    x = x_ref[...]
    ms = jnp.mean(x * x, axis=-1, keepdims=True)
    o_ref[...] = x * lax.rsqrt(ms + NORM_EPS) * g_ref[...]


def final_norm(x, g):
    m, d = x.shape
    return pl.pallas_call(
        _final_norm_kernel,
        grid=(m // TM,),
        in_specs=[pl.BlockSpec((TM, d), lambda i: (i, 0)), pl.BlockSpec((1, d), lambda i: (0, 0))],
        out_specs=pl.BlockSpec((TM, d), lambda i: (i, 0)),
        out_shape=jax.ShapeDtypeStruct((m, d), F32),
        compiler_params=_cparams(("parallel",)),
        name="final_norm",
    )(x, g.reshape(1, d))


def _head_ones(width):
    r = lax.broadcasted_iota(jnp.int32, (width, width), 0) // A_HEAD
    c = lax.broadcasted_iota(jnp.int32, (width, width), 1) // A_HEAD
    return (r == c).astype(BF16)


def _head_eye(width):
    r = lax.broadcasted_iota(jnp.int32, (A_HEAD, width), 0)
    c = lax.broadcasted_iota(jnp.int32, (A_HEAD, width), 1) & (A_HEAD - 1)
    return r == c


RWKV_G = 4
RWKV_HPB = 2
RWKV_TB = 256
RWKV_W = RWKV_HPB * LANES
RWKV_MM_GROUPS = 2
RWKV_UNROLL = 8


def _head_sum(z, ones16):
    ones128 = ones16[:LANES, :LANES]
    parts = [_dot_01(z[:, j * LANES:(j + 1) * LANES], ones128, m01_left=False)
             for j in range(z.shape[1] // LANES)]
    return parts[0] if len(parts) == 1 else jnp.concatenate(parts, axis=1)


def _rwkv_scan_kernel(*refs, has_s0):
    views = (refs[0:5], refs[5:10])
    w2_ref, w0_ref, a2_ref, a0_ref, kk_ref, ka_ref = refs[10:16]
    rest = refs[16:]
    if has_s0:
        s0_ref, rest = rest[0], rest[1:]
    yf_ref, yb_ref, sfin_ref, a_s, w_s, b_s, k_s, st_s = rest
    y_refs = (yf_ref, yb_ref)
    n_g, tb_len, width = views[0][0].shape
    n_heads = width // A_HEAD
    tb = pl.program_id(2)
    n_tb = pl.num_programs(2)
    ones16 = _head_ones(width)
    eye = _head_eye(width)
    eye_f32 = eye.astype(F32)
    eye_mask16 = eye_f32.astype(BF16) > 0.5
    zero16 = jnp.zeros((A_HEAD, width), BF16)

    for d in range(2):
        _, k_ref, _, xw_ref, xa_ref = views[d]
        for g in range(n_g):
            kraw = k_ref[g]
            kk = kraw * kk_ref[...]
            kk = kk * lax.rsqrt(jnp.maximum(_head_sum(kk * kk, ones16), 1e-24))
            a_s[d, g] = -kk
            xw = xw_ref[g][:, d * W_LORA:(d + 1) * W_LORA]
            xa = xa_ref[g][:, d * ICL_LORA:(d + 1) * ICL_LORA]
            wl = w0_ref[d] + _dot_x3(jnp.tanh(xw), w2_ref[d])
            w_s[d, g] = jnp.exp(-jnp.exp(-_softplus(-wl) - 0.5))
            a = _sigmoid(a0_ref[d] + _dot_x3(xa, a2_ref[d]))
            k_s[d, g] = kraw * (1.0 + (a - 1.0) * ka_ref[...])
            b_s[d, g] = kk * a

    chains = [(d, g) for d in range(2) for g in range(n_g)]

    @pl.when(tb == 0)
    def _():
        for d, g in chains:
            if has_s0:
                st_s[d, g] = jnp.concatenate([s0_ref[g, d, h] for h in range(n_heads)], axis=1)
            else:
                st_s[d, g] = jnp.zeros((A_HEAD, width), F32)

    group_len = len(chains) // RWKV_MM_GROUPS
    groups = [chains[i * group_len:(i + 1) * group_len] for i in range(RWKV_MM_GROUPS)]

    def readout(yb):
        return jnp.sum(yb * eye_f32, axis=0, keepdims=True)

    def step(t, carry):
        rows = [pl.ds(t if d == 0 else tb_len - 1 - t, 1) for d in range(2)]
        prev_rows = [pl.ds(jnp.maximum(t - 1, 0), 1), pl.ds(jnp.minimum(tb_len - t, tb_len - 1), 1)]
        for grp in groups:
            lhs = []
            for d, g in grp:
                r_ref, _, v_ref, _, _ = views[d]
                s = st_s[d, g]
                lhs.append((s * a_s[d, g, rows[d], :]).astype(BF16))
                lhs.append(jnp.where(eye, v_ref[g, rows[d], :], 0.0).astype(BF16))
                lhs.append((s * r_ref[g, prev_rows[d], :]).astype(BF16))
            res = jnp.dot(jnp.concatenate(lhs, axis=0), ones16, preferred_element_type=F32)
            for i, (d, g) in enumerate(grp):
                sa, vb, yb = (res[(3 * i + q) * A_HEAD:(3 * i + q + 1) * A_HEAD] for q in range(3))
                y_refs[d][g, prev_rows[d], :] = readout(yb)
                row = rows[d]
                st_s[d, g] = st_s[d, g] * w_s[d, g, row, :] + sa * b_s[d, g, row, :] + vb * k_s[d, g, row, :]
        return carry

    lax.fori_loop(0, tb_len, step, 0, unroll=RWKV_UNROLL)
    last_rows = (pl.ds(tb_len - 1, 1), pl.ds(0, 1))
    lhs = [(st_s[d, g] * views[d][0][g, last_rows[d], :]).astype(BF16) for d, g in chains]
    res = jnp.dot(jnp.concatenate(lhs, axis=0), ones16, preferred_element_type=F32)
    for i, (d, g) in enumerate(chains):
        y_refs[d][g, last_rows[d], :] = readout(res[i * A_HEAD:(i + 1) * A_HEAD])

    @pl.when(tb == n_tb - 1)
    def _():
        for d, g in chains:
            s = st_s[d, g]
            for h in range(n_heads):
                sfin_ref[g, d, h] = s[:, h * A_HEAD:(h + 1) * A_HEAD]


def rwkv_scan(u, p, s0, layer_idx):
    bsz, t_len, _ = u.shape
    n_tb = t_len // RWKV_TB
    n_hpg = A_WIDTH // RWKV_W
    fwd = lambda tb: tb
    bwd = lambda tb: n_tb - 1 - tb

    def view(tmap):
        def col(first_blk, width=RWKV_W, per_group=True):
            return pl.BlockSpec((RWKV_G, RWKV_TB, width),
                                lambda bi, hg, tb: (bi, tmap(tb), first_blk + (hg if per_group else 0)))
        return [col(0), col(n_hpg), col(2 * n_hpg),
                col(AB_XW_BLK, LANES, False), col(AB_XA_BLK, LANES, False)]

    in_specs = view(fwd) + view(bwd) + [
        pl.BlockSpec((2, W_LORA, RWKV_W), lambda bi, hg, tb: (0, 0, hg)),
        pl.BlockSpec((2, 1, RWKV_W), lambda bi, hg, tb: (0, 0, hg)),
        pl.BlockSpec((2, ICL_LORA, RWKV_W), lambda bi, hg, tb: (0, 0, hg)),
        pl.BlockSpec((2, 1, RWKV_W), lambda bi, hg, tb: (0, 0, hg)),
        pl.BlockSpec((1, RWKV_W), lambda bi, hg, tb: (0, hg)),
        pl.BlockSpec((1, RWKV_W), lambda bi, hg, tb: (0, hg)),
    ]
    args = [u] * 10 + [p['w2'], p['w0'].reshape(2, 1, A_WIDTH), p['a2'], p['a0'].reshape(2, 1, A_WIDTH),
                       p['k_k'].reshape(1, A_WIDTH), p['k_a'].reshape(1, A_WIDTH)]
    heads_blk = 2 * RWKV_HPB
    if s0 is not None:
        in_specs.append(pl.BlockSpec((RWKV_G, None, 2, heads_blk, A_HEAD, A_HEAD),
                                     lambda bi, hg, tb: (bi, layer_idx, 0, hg, 0, 0)))
        args.append(s0)
    blk = (2, RWKV_G, RWKV_TB, RWKV_W)
    return pl.pallas_call(
        functools.partial(_rwkv_scan_kernel, has_s0=s0 is not None),
        grid=(bsz // RWKV_G, n_hpg, n_tb),
        in_specs=in_specs,
        out_specs=[
            pl.BlockSpec((RWKV_G, RWKV_TB, RWKV_W), lambda bi, hg, tb: (bi, fwd(tb), hg)),
            pl.BlockSpec((RWKV_G, RWKV_TB, RWKV_W), lambda bi, hg, tb: (bi, bwd(tb), hg)),
            pl.BlockSpec((RWKV_G, 2, heads_blk, A_HEAD, A_HEAD), lambda bi, hg, tb: (bi, 0, hg, 0, 0)),
        ],
        out_shape=[
            jax.ShapeDtypeStruct((bsz, t_len, A_WIDTH), F32),
            jax.ShapeDtypeStruct((bsz, t_len, A_WIDTH), F32),
            jax.ShapeDtypeStruct((bsz, 2, A_HEADS, A_HEAD, A_HEAD), F32),
        ],
        scratch_shapes=[
            pltpu.VMEM(blk, F32),
            pltpu.VMEM(blk, F32),
            pltpu.VMEM(blk, F32),
            pltpu.VMEM(blk, F32),
            pltpu.VMEM((2, RWKV_G, A_HEAD, RWKV_W), F32),
        ],
        compiler_params=_cparams(("parallel", "parallel", "arbitrary")),
        name="rwkv7_scan",
    )(*args)


def _rwkv_post_kernel(yf_ref, yb_ref, r_ref, k_ref, v_ref, xa_ref, xg_ref,
                      a2_ref, a0_ref, ka_ref, rk_ref, gng_ref, gnb_ref, g2_ref, o_ref):
    ones_bd = _head_ones(LANES)
    inv_n = 1.0 / A_HEAD
    y = yf_ref[...] + yb_ref[...]
    mu = _head_sum(y, ones_bd) * inv_n
    yc = y - mu
    var = _head_sum(yc * yc, ones_bd) * inv_n
    yn = yc * lax.rsqrt(var + GN_EPS) * gng_ref[...] + gnb_ref[...]
    r = r_ref[...]
    kraw = k_ref[...]
    rk_sum = jnp.zeros_like(r)
    for d in range(2):
        xa = xa_ref[:, d * ICL_LORA:(d + 1) * ICL_LORA]
        a = _sigmoid(a0_ref[d] + _dot_x3(xa, a2_ref[d]))
        rk_sum = rk_sum + r * (kraw * (1.0 + (a - 1.0) * ka_ref[...])) * rk_ref[...]
    bonus = _head_sum(rk_sum, ones_bd) * v_ref[...]
    gate = jnp.dot(_sigmoid(xg_ref[...]).astype(BF16), g2_ref[...].astype(BF16), preferred_element_type=F32)
    o_ref[...] = ((yn + bonus) * gate).astype(o_ref.dtype)


def rwkv_post(yf, yb, u2, p):
    m = yf.shape[0]
    tm = 256
    full = lambda shape: pl.BlockSpec(shape, lambda i: (0,) * len(shape))
    row = lambda width, blk: pl.BlockSpec((tm, width), lambda i: (i, blk))
    return pl.pallas_call(
        _rwkv_post_kernel,
        grid=(m // tm,),
        in_specs=[
            row(A_WIDTH, 0), row(A_WIDTH, 0),
            row(A_WIDTH, 0), row(A_WIDTH, 1), row(A_WIDTH, 2),
            row(LANES, AB_XA_BLK), row(G_LORA_PAD, AB_XG_BLK),
            full((2, ICL_LORA, A_WIDTH)), full((2, 1, A_WIDTH)),
            full((1, A_WIDTH)), full((1, A_WIDTH)), full((1, A_WIDTH)), full((1, A_WIDTH)),
            full((G_LORA_PAD, A_WIDTH)),
        ],
        out_specs=row(A_WIDTH, 0),
        out_shape=jax.ShapeDtypeStruct((m, A_WIDTH), BF16),
        compiler_params=_cparams(("parallel",)),
        name="rwkv7_post",
    )(yf, yb, u2, u2, u2, u2, u2, p['a2'], p['a0'].reshape(2, 1, A_WIDTH), p['k_a'].reshape(1, A_WIDTH),
      p['r_k'].reshape(1, A_WIDTH), p['gn_g'].reshape(1, A_WIDTH), p['gn_b'].reshape(1, A_WIDTH), p['g2'])


def rwkv_mix(u, p, s0, layer_idx):
    bsz, t_len, _ = u.shape
    m = bsz * t_len
    yf, yb, sfin = rwkv_scan(u, p, s0, layer_idx)
    out = rwkv_post(yf.reshape(m, A_WIDTH), yb.reshape(m, A_WIDTH), u.reshape(m, AB_PAD_COLS), p)
    return out, sfin


def _mla_kernel(qn_ref, qr_ref, kn_ref, kr_ref, v_ref, o_ref):
    scale = (NOPE + ROPE_D) ** -0.5
    krb = kr_ref[...].astype(BF16)
    nt = (((1,), (1,)), ((), ()))
    for h in range(B_HEADS):
        qn = qn_ref[:, h * NOPE:(h + 1) * NOPE].astype(BF16)
        qr = qr_ref[:, h * ROPE_D:(h + 1) * ROPE_D].astype(BF16)
        s = lax.dot_general(qn, kn_ref[:, h * NOPE:(h + 1) * NOPE], nt, preferred_element_type=F32)
        s = s + lax.dot_general(qr, krb, nt, preferred_element_type=F32)
        s = s * scale
        m = jnp.max(s, axis=-1, keepdims=True)
        e = jnp.exp(s - m)
        p = e / jnp.sum(e, axis=-1, keepdims=True)
        o = jnp.dot(p.astype(BF16), v_ref[:, h * V_HEAD:(h + 1) * V_HEAD], preferred_element_type=F32)
        o_ref[:, h * V_HEAD:(h + 1) * V_HEAD] = o.astype(o_ref.dtype)


def mla_attention(q, kv, kr, *, tq):
    bsz, t_len, _ = q.shape
    s_len = kv.shape[1]
    hn = B_HEADS * NOPE
    return pl.pallas_call(
        _mla_kernel,
        grid=(bsz, t_len // tq),
        in_specs=[
            pl.BlockSpec((None, tq, hn), lambda b, i: (b, i, 0)),
            pl.BlockSpec((None, tq, B_HEADS * ROPE_D), lambda b, i: (b, i, hn // (B_HEADS * ROPE_D))),
            pl.BlockSpec((None, s_len, hn), lambda b, i: (b, 0, 0)),
            pl.BlockSpec((None, s_len, ROPE_D), lambda b, i: (b, 0, 0)),
            pl.BlockSpec((None, s_len, hn), lambda b, i: (b, 0, 1)),
        ],
        out_specs=pl.BlockSpec((None, tq, hn), lambda b, i: (b, i, 0)),
        out_shape=jax.ShapeDtypeStruct((bsz, t_len, hn), BF16),
        compiler_params=_cparams(("parallel", "parallel")),
        name="mla_attention",
    )(q, q, kv, kr, kv)


GLA_SUPER = 4


def _gla_kernel(q_ref, k_ref, v_ref, og_ref, xg_ref, gw_ref, gb_ref, gn_ref, *rest, has_s0):
    if has_s0:
        s0_ref, rest = rest[0], rest[1:]
    o_ref, sfin_ref = rest[:2]
    lg_s, y_s, st_s, qt_s, u_s, dec_s = (rest[2 + 2 * i:4 + 2 * i] for i in range(6))
    t_len = q_ref.shape[0]
    n_chunks = t_len // CHUNK
    scale = C_DK ** -0.5
    nt = (((1,), (1,)), ((), ()))
    xg = xg_ref[...]
    span = GLA_SUPER * CHUNK
    ri = lax.broadcasted_iota(jnp.int32, (span, span), 0)
    ci = lax.broadcasted_iota(jnp.int32, (span, span), 1)
    same_chunk = (ri // CHUNK) == (ci // CHUNK)
    keeps = (same_chunk & (ri >= ci), same_chunk & (ri <= ci))
    tris = tuple(kp.astype(BF16) for kp in keeps)
    for d in range(2):
        z = _dot_x3(xg[:, d * GATE_LORA:(d + 1) * GATE_LORA], gw_ref[d], stack=False) + gb_ref[d]
        lg_s[d][...] = -_softplus(-z) * (1.0 / GATE_TAU)
        if has_s0:
            st_s[d][...] = s0_ref[d].T
        else:
            st_s[d][...] = jnp.zeros(st_s[d].shape, F32)

    def chunks_local(sc, carry):
        rows = pl.ds(pl.multiple_of(sc * span, span), span)
        q = q_ref[rows, :] * scale
        k = k_ref[rows, :]
        v = v_ref[rows, :]
        vb = v.astype(BF16)
        for d in range(2):
            b = _dot_01(tris[d], lg_s[d][rows, :])
            edge = CHUNK - 1 if d == 0 else 0
            bls = [b[cc * CHUNK + edge:cc * CHUNK + edge + 1, :] for cc in range(GLA_SUPER)]
            bl = jnp.concatenate([jnp.broadcast_to(x, (CHUNK, C_DK)) for x in bls], axis=0)
            qt = (q * jnp.exp(b)).astype(BF16)
            kt = (k * jnp.exp(-b)).astype(BF16)
            kd = (k * jnp.exp(bl - b)).astype(BF16)
            att = jnp.where(keeps[d], lax.dot_general(qt, kt, nt, preferred_element_type=F32), 0.0)
            y_s[d][rows, :] = jnp.dot(att.astype(BF16), vb, preferred_element_type=F32)
            qt_s[d][rows, :] = qt
            for cc in range(GLA_SUPER):
                part = slice(cc * CHUNK, (cc + 1) * CHUNK)
                u_s[d][sc * GLA_SUPER + cc] = jnp.dot(v[part].T.astype(BF16), kd[part], preferred_element_type=F32)
                dec_s[d][sc * GLA_SUPER + cc] = jnp.broadcast_to(jnp.exp(bls[cc]), (8, C_DK))
        return carry

    lax.fori_loop(0, n_chunks // GLA_SUPER, chunks_local, 0)

    def chunk_state(i, carry):
        for d in range(2):
            c = i if d == 0 else n_chunks - 1 - i
            rows = pl.ds(pl.multiple_of(c * CHUNK, CHUNK), CHUNK)
            st = st_s[d][...]
            y_s[d][rows, :] = y_s[d][rows, :] + lax.dot_general(qt_s[d][rows, :], st.astype(BF16), nt,
                                                                preferred_element_type=F32)
            st_s[d][...] = st * dec_s[d][c, 0:1, :] + u_s[d][c]
        return carry

    lax.fori_loop(0, n_chunks, chunk_state, 0)
    for d in range(2):
        sfin_ref[d] = st_s[d][...].T
    y = y_s[0][...] + y_s[1][...]
    yn = y * lax.rsqrt(jnp.mean(y * y, axis=-1, keepdims=True) + NORM_EPS) * gn_ref[...]
    o_ref[...] = (yn * _silu(og_ref[...])).astype(o_ref.dtype)


def gla_mix(u, p, s0, layer_idx):
    bsz, t_len, _ = u.shape
    in_specs = [
        pl.BlockSpec((None, t_len, C_DK), lambda b, h: (b, 0, h)),
        pl.BlockSpec((None, t_len, C_DK), lambda b, h: (b, 0, C_HEADS + h)),
        pl.BlockSpec((None, t_len, C_DV), lambda b, h: (b, 0, C_HEADS + h)),
        pl.BlockSpec((None, t_len, C_DV), lambda b, h: (b, 0, 2 * C_HEADS + h)),
        pl.BlockSpec((None, t_len, LANES), lambda b, h: (b, 0, C_XG_BLK)),
        pl.BlockSpec((2, GATE_LORA, C_DK), lambda b, h: (0, 0, h)),
        pl.BlockSpec((2, 1, C_DK), lambda b, h: (0, 0, h)),
        pl.BlockSpec((1, C_DV), lambda b, h: (0, 0)),
    ]
    args = [u, u, u, u, u, p['gate_w2'], p['gate_b'].reshape(2, 1, C_QK), p['gn'].reshape(1, C_DV)]
    if s0 is not None:
        in_specs.append(pl.BlockSpec((None, None, 2, None, C_DK, C_DV), lambda b, h: (b, layer_idx, 0, h, 0, 0)))
        args.append(s0)
    return pl.pallas_call(
        functools.partial(_gla_kernel, has_s0=s0 is not None),
        grid=(bsz, C_HEADS),
        in_specs=in_specs,
        out_specs=[
            pl.BlockSpec((None, t_len, C_DV), lambda b, h: (b, 0, h)),
            pl.BlockSpec((None, 2, None, C_DK, C_DV), lambda b, h: (b, 0, h, 0, 0)),
        ],
        out_shape=[
            jax.ShapeDtypeStruct((bsz, t_len, C_V), BF16),
            jax.ShapeDtypeStruct((bsz, 2, C_HEADS, C_DK, C_DV), F32),
        ],
        scratch_shapes=[
            pltpu.VMEM((t_len, C_DK), F32),
            pltpu.VMEM((t_len, C_DK), F32),
            pltpu.VMEM((t_len, C_DV), F32),
            pltpu.VMEM((t_len, C_DV), F32),
            pltpu.VMEM((C_DV, C_DK), F32),
            pltpu.VMEM((C_DV, C_DK), F32),
            pltpu.VMEM((t_len, C_DK), BF16),
            pltpu.VMEM((t_len, C_DK), BF16),
            pltpu.VMEM((t_len // CHUNK, C_DV, C_DK), F32),
            pltpu.VMEM((t_len // CHUNK, C_DV, C_DK), F32),
            pltpu.VMEM((t_len // CHUNK, 8, C_DK), F32),
            pltpu.VMEM((t_len // CHUNK, 8, C_DK), F32),
        ],
        compiler_params=_cparams(("parallel", "parallel")),
        name="gla_mix",
    )(*args)


DISPATCH_TN = 512


def _dispatch_kernel(x_ref, g_ref, sc_ref, sh_ref, wr_ref, xe_ref, rank_ref, aff_ref, hb_s, onehot_s, *, cap):
    t_len = x_ref.shape[0]
    j = pl.program_id(1)

    @pl.when(j == 0)
    def _():
        h = _norm_mod(x_ref[...], g_ref[...], sc_ref[...], sh_ref[...])
        for n in range(hb_s.shape[0]):
            hb_s[n] = h[:, n * DISPATCH_TN:(n + 1) * DISPATCH_TN].astype(BF16)
        logits = _dot_x3(h, wr_ref[...], stack=False)
        lane = lax.broadcasted_iota(jnp.int32, logits.shape, 1)
        logits = jnp.where(lane < N_EXPERTS, logits, -jnp.inf)
        m = jnp.max(logits, axis=-1, keepdims=True)
        e = jnp.exp(logits - m)
        aff = e / jnp.sum(e, axis=-1, keepdims=True)
        aff_ref[...] = aff
        aff_t = aff.T
        sub_i = lax.broadcasted_iota(jnp.int32, (t_len, t_len), 0)
        lane_i = lax.broadcasted_iota(jnp.int32, (t_len, t_len), 1)
        slot = lax.broadcasted_iota(jnp.int32, (cap, t_len), 0).astype(F32)
        expert_row = lax.broadcasted_iota(jnp.int32, (N_EXPERTS, t_len), 0)
        ranks = jnp.zeros((N_EXPERTS, t_len), F32)
        for ex in range(N_EXPERTS):
            col = aff[:, ex:ex + 1]
            row = aff_t[ex:ex + 1, :]
            ahead = (col > row) | ((col == row) & (sub_i < lane_i))
            rank_row = jnp.sum(ahead.astype(F32), axis=0, keepdims=True)
            onehot_s[ex * cap:(ex + 1) * cap, :] = (rank_row == slot).astype(BF16)
            ranks = jnp.where(expert_row == ex, rank_row, ranks)
        pad = jnp.zeros((LANES - N_EXPERTS, t_len), F32)
        rank_ref[...] = jnp.concatenate([ranks, pad], axis=0).T

    rows = jnp.dot(onehot_s[...], hb_s[j], preferred_element_type=F32).astype(BF16)
    for ex in range(N_EXPERTS):
        xe_ref[ex] = rows[ex * cap:(ex + 1) * cap]


def moe_dispatch(x, g, sc, sh, w_router_pad, cap):
    bsz, t_len, d = x.shape
    per_req = sc.shape[0] > 1
    mod_spec = pl.BlockSpec((None, 1, d), (lambda b, j: (b, 0, 0)) if per_req else (lambda b, j: (0, 0, 0)))
    return pl.pallas_call(
        functools.partial(_dispatch_kernel, cap=cap),
        grid=(bsz, d // DISPATCH_TN),
        in_specs=[
            pl.BlockSpec((None, t_len, d), lambda b, j: (b, 0, 0)),
            pl.BlockSpec((1, d), lambda b, j: (0, 0)),
            mod_spec, mod_spec,
            pl.BlockSpec((d, LANES), lambda b, j: (0, 0)),
        ],
        out_specs=[
            pl.BlockSpec((N_EXPERTS, cap, DISPATCH_TN), lambda b, j: (0, b, j)),
            pl.BlockSpec((None, t_len, LANES), lambda b, j: (b, 0, 0)),
            pl.BlockSpec((None, t_len, LANES), lambda b, j: (b, 0, 0)),
        ],
        out_shape=[
            jax.ShapeDtypeStruct((N_EXPERTS, bsz * cap, d), BF16),
            jax.ShapeDtypeStruct((bsz, t_len, LANES), F32),
            jax.ShapeDtypeStruct((bsz, t_len, LANES), F32),
        ],
        scratch_shapes=[
            pltpu.VMEM((d // DISPATCH_TN, t_len, DISPATCH_TN), BF16),
            pltpu.VMEM((N_EXPERTS * cap, t_len), BF16),
        ],
        compiler_params=_cparams(("parallel", "arbitrary")),
        name="moe_dispatch",
    )(x, g.reshape(1, d), sc, sh, w_router_pad)


EXPERT_TF = 256
EXPERT_TN = 256
N_UP_STEPS = D_EXPERT // EXPERT_TF
N_DOWN_STEPS = D_MODEL // EXPERT_TN


def _experts_kernel(xc_ref, xl_ref, wg_ref, wu_ref, wd_ref, yc_ref, yl_ref, hid_s):
    s = pl.program_id(1)
    mc = xc_ref.shape[0]

    @pl.when(s < N_UP_STEPS)
    def _():
        wg = wg_ref[...].astype(BF16)
        wu = wu_ref[...].astype(BF16)
        cols = pl.ds(pl.multiple_of(s * EXPERT_TF, EXPERT_TF), EXPERT_TF)
        for x_ref, r0 in ((xc_ref, 0), (xl_ref, mc)):
            x = x_ref[...]
            hg = jnp.dot(x, wg, preferred_element_type=F32)
            hu = jnp.dot(x, wu, preferred_element_type=F32)
            hid_s[r0:r0 + x.shape[0], cols] = (_silu(hg) * hu).astype(BF16)

    @pl.when(s >= N_UP_STEPS)
    def _():
        wd = wd_ref[...].astype(BF16)
        yc_ref[...] = jnp.dot(hid_s[0:mc, :], wd, preferred_element_type=F32).astype(yc_ref.dtype)
        yl_ref[...] = jnp.dot(hid_s[mc:, :], wd, preferred_element_type=F32).astype(yl_ref.dtype)


def moe_experts(xc, xl, w_gate, w_up, w_down, layer):
    mc, ml = xc.shape[1], xl.shape[1]
    up_idx = lambda e, s: (layer, e, 0, jnp.minimum(s, N_UP_STEPS - 1))
    down_idx = lambda e, s: (layer, e, 0, jnp.maximum(s - N_UP_STEPS, 0))
    out_idx = lambda e, s: (e, 0, jnp.maximum(s - N_UP_STEPS, 0))
    return pl.pallas_call(
        _experts_kernel,
        grid=(N_EXPERTS, N_UP_STEPS + N_DOWN_STEPS),
        in_specs=[
            pl.BlockSpec((None, mc, D_MODEL), lambda e, s: (e, 0, 0)),
            pl.BlockSpec((None, ml, D_MODEL), lambda e, s: (e, 0, 0)),
            pl.BlockSpec((None, None, D_MODEL, EXPERT_TF), up_idx),
            pl.BlockSpec((None, None, D_MODEL, EXPERT_TF), up_idx),
            pl.BlockSpec((None, None, D_EXPERT, EXPERT_TN), down_idx),
        ],
        out_specs=[
            pl.BlockSpec((None, mc, EXPERT_TN), out_idx),
            pl.BlockSpec((None, ml, EXPERT_TN), out_idx),
        ],
        out_shape=[
            jax.ShapeDtypeStruct((N_EXPERTS, mc, D_MODEL), BF16),
            jax.ShapeDtypeStruct((N_EXPERTS, ml, D_MODEL), BF16),
        ],
        scratch_shapes=[pltpu.VMEM((mc + ml, D_EXPERT), BF16)],
        compiler_params=_cparams(("parallel", "arbitrary")),
        name="moe_experts",
    )(xc, xl, w_gate, w_up, w_down)


def _combine_kernel(x_ref, gate_ref, ye_ref, rank_ref, aff_ref, o_ref, hi_s, lo_s, *, cap):
    t_len = x_ref.shape[0]

    @pl.when(pl.program_id(1) == 0)
    def _():
        per_group = LANES // cap
        lane = lax.broadcasted_iota(jnp.int32, (t_len, LANES), 1)
        slot = (lane % cap).astype(F32)
        for grp in range(N_EXPERTS // per_group):
            w = jnp.zeros((t_len, LANES), F32)
            for i in range(per_group):
                ex = grp * per_group + i
                hit = rank_ref[:, ex:ex + 1] == slot
                if per_group > 1:
                    hit = hit & (lane // cap == i)
                w = jnp.where(hit, aff_ref[:, ex:ex + 1], w)
            hi = w.astype(BF16)
            hi_s[:, grp * LANES:(grp + 1) * LANES] = hi
            lo_s[:, grp * LANES:(grp + 1) * LANES] = (w - hi.astype(F32)).astype(BF16)

    ye = ye_ref[...].reshape(N_EXPERTS * cap, ye_ref.shape[2])
    acc = jnp.dot(hi_s[...], ye, preferred_element_type=F32) + jnp.dot(lo_s[...], ye, preferred_element_type=F32)
    o_ref[...] = x_ref[...] + gate_ref[...] * acc


def moe_combine(x, gate, ye, rank, aff, cap):
    bsz, t_len, d = x.shape
    per_req = gate.shape[0] > 1
    tn = 512
    return pl.pallas_call(
        functools.partial(_combine_kernel, cap=cap),
        grid=(bsz, d // tn),
        in_specs=[
            pl.BlockSpec((None, t_len, tn), lambda b, j: (b, 0, j)),
            pl.BlockSpec((None, 1, tn), (lambda b, j: (b, 0, j)) if per_req else (lambda b, j: (0, 0, j))),
            pl.BlockSpec((N_EXPERTS, cap, tn), lambda b, j: (0, b, j)),
            pl.BlockSpec((None, t_len, LANES), lambda b, j: (b, 0, 0)),
            pl.BlockSpec((None, t_len, LANES), lambda b, j: (b, 0, 0)),
        ],
        out_specs=pl.BlockSpec((None, t_len, tn), lambda b, j: (b, 0, j)),
        out_shape=jax.ShapeDtypeStruct((bsz, t_len, d), F32),
        scratch_shapes=[pltpu.VMEM((t_len, N_EXPERTS * cap), BF16), pltpu.VMEM((t_len, N_EXPERTS * cap), BF16)],
        compiler_params=_cparams(("parallel", "arbitrary")),
        name="moe_combine",
    )(x, gate, ye, rank, aff)


def _pad_cols(w, n):
    return jnp.pad(w, ((0, 0),) * (w.ndim - 1) + ((0, n - w.shape[-1]),))


def _ab_layout(w):
    rkv = w[..., :3 * A_WIDTH]
    o = 3 * A_WIDTH
    xg = _pad_cols(w[..., o:o + G_LORA], G_LORA_PAD)
    o += G_LORA
    xw = w[..., o:o + 2 * W_LORA]
    o += 2 * W_LORA
    xa = w[..., o:o + 2 * ICL_LORA]
    o += 2 * ICL_LORA
    rest = w[..., o:]
    return _pad_cols(jnp.concatenate([rkv, xg, xw, xa, rest], axis=-1), AB_PAD_COLS)


def _c_layout(w):
    qkv = w[:, :2 * C_QK + C_V]
    o = 2 * C_QK + C_V
    xg = _pad_cols(w[:, o:o + 2 * GATE_LORA], LANES)
    og = w[:, o + 2 * GATE_LORA:]
    return _pad_cols(jnp.concatenate([qkv, og, xg], axis=-1), C_PAD_COLS)


def _axial_rope_tables(t):
    rows = t // GRID_W
    row = jnp.repeat(jnp.arange(rows, dtype=F32), GRID_W)
    col = jnp.tile(jnp.arange(GRID_W, dtype=F32), rows)
    half = ROPE_D // 2
    inv = 1.0 / (ROPE_BASE ** (jnp.arange(0, half, 2, dtype=F32) / half))
    ang_r = row[:, None] * inv[None, :]
    ang_c = col[:, None] * inv[None, :]
    return (jnp.cos(ang_r), jnp.sin(ang_r), jnp.cos(ang_c), jnp.sin(ang_c))


def _rotate(x, cos, sin):
    m = x.shape[-1] // 2
    x1, x2 = x[..., :m], x[..., m:]
    c = cos[None, :, None, :]
    s = sin[None, :, None, :]
    return jnp.concatenate([x1 * c - x2 * s, x1 * s + x2 * c], axis=-1)


def _apply_axial_rope(x, tabs):
    cr, sr, cc, sc = tabs
    half = ROPE_D // 2
    return jnp.concatenate([_rotate(x[..., :half], cr, sr), _rotate(x[..., half:], cc, sc)], axis=-1)


def _mixer_layer(x2, l, mod_l, P, caches, rope_tabs, bsz, t_len, outs):
    m = bsz * t_len
    is_ctx = caches is None
    rows_per_mod = m if is_ctx else t_len
    sh1, sc1, g1 = mod_l[0], mod_l[1], mod_l[2]
    i = l // 2
    if l % 2 == 0:
        u = nm_matmul(x2, P['g_norm_mix'][l], sc1, sh1, P['ab_w_in_pad'][i], rows_per_mod=rows_per_mod,
                      tn=512, mu=P['rwkv_mu_pad'][i], shift_t=t_len)
        p = dict(w2=P['rwkv_w2'][i], w0=P['rwkv_w0'][i], a2=P['rwkv_a2'][i], a0=P['rwkv_a0'][i],
                 g2=P['rwkv_g2_pad'][i], k_k=P['rwkv_k_k'][i], k_a=P['rwkv_k_a'][i], r_k=P['rwkv_r_k'][i],
                 gn_g=P['rwkv_gn_g'][i], gn_b=P['rwkv_gn_b'][i])
        u3 = u.reshape(bsz, t_len, AB_PAD_COLS)
        a_out, a_state = rwkv_mix(u3, p, None if is_ctx else caches[2], i)
        (q,) = rms_matmul(u, AB_CQ_BLK, Q_LORA, P['mla_q_norm'][i], P['mla_w_uq_r'][i],
                          normalize=True, emit_norm=False, out_dtype=F32)
        kv, ckv = rms_matmul(u, AB_CKV_BLK, KV_LORA, P['mla_kv_norm'][i], P['mla_w_ukv_r'][i],
                             normalize=True, emit_norm=True, out_dtype=BF16)
        kr = u3[:, :, AB_KR_COL:AB_KR_COL + ROPE_D]
        q = q.reshape(bsz, t_len, -1)
        kv = kv.reshape(bsz, t_len, -1)
        if is_ctx:
            outs['ckv'].append(ckv.reshape(bsz, t_len, KV_LORA))
            outs['kr'].append(kr)
            outs['rwkv'].append(a_state)
            keys_r = kr
        else:
            hn = B_HEADS * NOPE
            q_rope = _apply_axial_rope(q[..., hn:].reshape(bsz, t_len, B_HEADS, ROPE_D), rope_tabs)
            q = jnp.concatenate([q[..., :hn], q_rope.reshape(bsz, t_len, B_HEADS * ROPE_D)], axis=-1)
            kr = _apply_axial_rope(kr[:, :, None, :], rope_tabs)[:, :, 0, :]
            past = caches[0].shape[2]
            (kv_ctx,) = rms_matmul(caches[0][:, i].reshape(bsz * past, KV_LORA), 0, KV_LORA,
                                   P['mla_kv_norm'][i], P['mla_w_ukv_r'][i],
                                   normalize=False, emit_norm=False, out_dtype=BF16)
            kv = jnp.concatenate([kv, kv_ctx.reshape(bsz, past, -1)], axis=1)
            keys_r = jnp.concatenate([kr, caches[1][:, i]], axis=1)
        b_out = mla_attention(q, kv, keys_r, tq=256)
        return res_matmul([a_out, b_out.reshape(m, A_WIDTH)], P['ab_w_out'][i], x2, g1,
                          rows_per_mod=rows_per_mod)
    u = nm_matmul(x2, P['g_norm_mix'][l], sc1, sh1, P['c_w_in_pad'][i], rows_per_mod=rows_per_mod, tn=512)
    p = dict(gate_w2=P['gla_gate_w2'][i], gate_b=P['gla_gate_b'][i], gn=P['gla_norm'][i])
    o, st = gla_mix(u.reshape(bsz, t_len, C_PAD_COLS), p, None if is_ctx else caches[3], i)
    if is_ctx:
        outs['gla'].append(st)
    return res_matmul([o.reshape(m, C_V)], P['c_w_out'][i], x2, g1, rows_per_mod=rows_per_mod)


def kernel(x_prompt, x_sample, cache_mla_ckv, cache_mla_krope, state_rwkv, state_gla, c, c_ctx, w_mod, b_mod, g_norm_mix, g_norm_ffn, ab_w_in, rwkv_mu, rwkv_w0, rwkv_w2, rwkv_a0, rwkv_a2, rwkv_g2, rwkv_k_k, rwkv_k_a, rwkv_r_k, rwkv_gn_g, rwkv_gn_b, mla_q_norm, mla_w_uq, mla_kv_norm, mla_w_ukv, ab_w_out, c_w_in, gla_gate_w2, gla_gate_b, gla_norm, c_w_out, moe_router, moe_w_gate, moe_w_up, moe_w_down, g_final):
    n_ab = ab_w_in.shape[0]
    n_c = c_w_in.shape[0]
    w_uq = mla_w_uq.reshape(n_ab, Q_LORA, B_HEADS, NOPE + ROPE_D)
    w_ukv = mla_w_ukv.reshape(n_ab, KV_LORA, B_HEADS, NOPE + V_HEAD)
    P = dict(
        g_norm_mix=g_norm_mix, g_norm_ffn=g_norm_ffn, g_final=g_final,
        ab_w_in_pad=_ab_layout(ab_w_in.astype(BF16)), rwkv_mu_pad=_ab_layout(rwkv_mu),
        rwkv_w0=rwkv_w0, rwkv_w2=rwkv_w2, rwkv_a0=rwkv_a0, rwkv_a2=rwkv_a2,
        rwkv_g2_pad=jnp.pad(rwkv_g2, ((0, 0), (0, G_LORA_PAD - G_LORA), (0, 0))),
        rwkv_k_k=rwkv_k_k, rwkv_k_a=rwkv_k_a, rwkv_r_k=rwkv_r_k.reshape(n_ab, A_WIDTH),
        rwkv_gn_g=rwkv_gn_g, rwkv_gn_b=rwkv_gn_b,
        mla_q_norm=mla_q_norm, mla_kv_norm=mla_kv_norm,
        mla_w_uq_r=jnp.concatenate([w_uq[..., :NOPE].reshape(n_ab, Q_LORA, -1),
                                    w_uq[..., NOPE:].reshape(n_ab, Q_LORA, -1)], axis=-1).astype(BF16),
        mla_w_ukv_r=jnp.concatenate([w_ukv[..., :NOPE].reshape(n_ab, KV_LORA, -1),
                                     w_ukv[..., NOPE:].reshape(n_ab, KV_LORA, -1)], axis=-1).astype(BF16),
        ab_w_out=ab_w_out.astype(BF16),
        c_w_in_pad=jnp.stack([_c_layout(c_w_in[i].astype(BF16)) for i in range(n_c)]),
        gla_gate_w2=gla_gate_w2, gla_gate_b=gla_gate_b, gla_norm=gla_norm, c_w_out=c_w_out.astype(BF16),
    )
    router_pad = _pad_cols(moe_router, LANES)

    cvecs = jnp.concatenate([c_ctx[None, :], c, jnp.zeros((8 - 1 - c.shape[0], D_MODEL), F32)], axis=0)
    mods = modulation_all(cvecs, w_mod, b_mod)
    n_lat = c.shape[0]

    rope_tabs = _axial_rope_tables(x_sample.shape[1])
    caches = (cache_mla_ckv, cache_mla_krope, state_rwkv, state_gla)
    bc, tc, _ = x_prompt.shape
    bl, tl, _ = x_sample.shape
    cap_c = CAP_FACTOR * tc // N_EXPERTS
    cap_l = CAP_FACTOR * tl // N_EXPERTS
    xc = x_prompt.reshape(bc * tc, D_MODEL)
    xl = x_sample.reshape(bl * tl, D_MODEL)
    outs = dict(ckv=[], kr=[], rwkv=[], gla=[])
    for l in range(DEPTH):
        mod_c = [t[:, None, :] for t in jnp.split(mods[l, 0:1], N_MOD, axis=-1)]
        mod_l = [t[:, None, :] for t in jnp.split(mods[l, 1:1 + n_lat], N_MOD, axis=-1)]
        xc = _mixer_layer(xc, l, mod_c, P, None, None, bc, tc, outs)
        xl = _mixer_layer(xl, l, mod_l, P, caches, rope_tabs, bl, tl, None)
        xc3 = xc.reshape(bc, tc, D_MODEL)
        xl3 = xl.reshape(bl, tl, D_MODEL)
        xe_c, rank_c, aff_c = moe_dispatch(xc3, g_norm_ffn[l], mod_c[4], mod_c[3], router_pad[l], cap_c)
        xe_l, rank_l, aff_l = moe_dispatch(xl3, g_norm_ffn[l], mod_l[4], mod_l[3], router_pad[l], cap_l)
        ye_c, ye_l = moe_experts(xe_c, xe_l, moe_w_gate, moe_w_up, moe_w_down, l)
        xc = moe_combine(xc3, mod_c[5], ye_c, rank_c, aff_c, cap_c).reshape(bc * tc, D_MODEL)
        xl = moe_combine(xl3, mod_l[5], ye_l, rank_l, aff_l, cap_l).reshape(bl * tl, D_MODEL)
    y_prompt = final_norm(xc, g_final).reshape(bc, tc, D_MODEL)
    y_sample = final_norm(xl, g_final).reshape(bl, tl, D_MODEL)
    return (y_prompt, y_sample, jnp.stack(outs['ckv'], axis=1), jnp.stack(outs['kr'], axis=1),
            jnp.stack(outs['rwkv'], axis=1), jnp.stack(outs['gla'], axis=1))
```

```python
import functools

import jax
import jax.numpy as jnp
from jax import lax
from jax.experimental import pallas as pl
from jax.experimental.pallas import tpu as pltpu

F32 = jnp.float32
BF16 = jnp.bfloat16
HIGHEST = lax.Precision.HIGHEST

VMEM_LIMIT_BYTES = 56 * 1024 * 1024
LANES = 128

D_MODEL = 2048
DEPTH = 4
GRID_W = 64
N_MOD = 6
NORM_EPS = 1e-6

A_WIDTH = 1024
A_HEAD = 64
A_HEADS = 16
W_LORA = 64
ICL_LORA = 64
G_LORA = 160
G_LORA_PAD = 256
GN_EPS = 64e-5

B_HEADS = 8
NOPE = 128
ROPE_D = 64
V_HEAD = 128
Q_LORA = 512
KV_LORA = 512
ROPE_BASE = 10000.0

C_HEADS = 4
C_QK = 1024
C_V = 2048
C_DK = 256
C_DV = 512
GATE_LORA = 16
GATE_TAU = 16.0
CHUNK = 64

N_EXPERTS = 16
D_EXPERT = 4096
CAP_FACTOR = 2

AB_PAD_COLS = 5120
AB_XG_BLK = 3072 // G_LORA_PAD
AB_XW_BLK = 3328 // LANES
AB_XA_BLK = 3456 // LANES
AB_CQ_BLK = 3584 // Q_LORA
AB_CKV_BLK = 4096 // KV_LORA
AB_KR_COL = 4608
AB_SHIFT_COLS = 3584
C_PAD_COLS = 6656
C_XG_BLK = 6144 // LANES

TM = 1024


def _cparams(sem):
    return pltpu.CompilerParams(dimension_semantics=sem, vmem_limit_bytes=VMEM_LIMIT_BYTES)


def _sigmoid(x):
    return 1.0 / (1.0 + jnp.exp(-x))


def _softplus(x):
    return jnp.maximum(x, 0.0) + jnp.log(1.0 + jnp.exp(-jnp.abs(x)))


def _silu(x):
    return x * _sigmoid(x)


def _dot_01(a, b, m01_left=True):
    x = b if m01_left else a
    hi = x.astype(BF16)
    r1 = x - hi.astype(F32)
    mid = r1.astype(BF16)
    lo = (r1 - mid.astype(F32)).astype(BF16)
    if m01_left:
        return jnp.dot(jnp.concatenate([a, a, a], axis=1), jnp.concatenate([hi, mid, lo], axis=0),
                       preferred_element_type=F32)
    return jnp.dot(jnp.concatenate([hi, mid, lo], axis=1), jnp.concatenate([b, b, b], axis=0),
                   preferred_element_type=F32)


def _dot_x3(a, b, stack=True):
    ah = a.astype(BF16)
    al = (a - ah.astype(F32)).astype(BF16)
    bh = b.astype(BF16)
    bl = (b - bh.astype(F32)).astype(BF16)
    if stack:
        return jnp.dot(jnp.concatenate([ah, ah, al], axis=1), jnp.concatenate([bh, bl, bh], axis=0),
                       preferred_element_type=F32)
    return (jnp.dot(ah, bh, preferred_element_type=F32) + jnp.dot(ah, bl, preferred_element_type=F32)
            + jnp.dot(al, bh, preferred_element_type=F32))


def _mod_kernel(c_ref, w_ref, b_ref, o_ref):
    a = _silu(c_ref[...])
    o_ref[...] = _dot_x3(a, w_ref[...], stack=False) + b_ref[...]


def modulation_all(cvecs, w_mod, b_mod):
    tn = 1024
    n = w_mod.shape[-1]
    return pl.pallas_call(
        _mod_kernel,
        grid=(DEPTH, n // tn),
        in_specs=[
            pl.BlockSpec((8, D_MODEL), lambda l, j: (0, 0)),
            pl.BlockSpec((None, D_MODEL, tn), lambda l, j: (l, 0, j)),
            pl.BlockSpec((None, 1, tn), lambda l, j: (l, 0, j)),
        ],
        out_specs=pl.BlockSpec((None, 8, tn), lambda l, j: (l, 0, j)),
        out_shape=jax.ShapeDtypeStruct((DEPTH, 8, n), F32),
        compiler_params=_cparams(("parallel", "parallel")),
        name="modulation",
    )(cvecs, w_mod, b_mod.reshape(DEPTH, 1, n))


def _norm_mod(x, g, sc, sh):
    ms = jnp.mean(x * x, axis=-1, keepdims=True)
    y = x * lax.rsqrt(ms + NORM_EPS) * g
    return y * (1.0 + sc) + sh


def _nm_mm_kernel(x_ref, g_ref, sc_ref, sh_ref, w_ref, *rest, shift_t):
    if shift_t:
        mu_ref, o_ref, h_scr = rest
    else:
        o_ref, h_scr = rest

    @pl.when(pl.program_id(1) == 0)
    def _():
        h_scr[...] = _norm_mod(x_ref[...], g_ref[...], sc_ref[...], sh_ref[...]).astype(BF16)

    acc = jnp.dot(h_scr[...], w_ref[...].astype(BF16), preferred_element_type=F32)
    if not shift_t:
        o_ref[...] = acc
        return
    tn = acc.shape[1]
    shifted_tiles = -(-AB_SHIFT_COLS // tn)

    @pl.when(pl.program_id(1) < shifted_tiles)
    def _():
        tm = acc.shape[0]
        row = lax.broadcasted_iota(jnp.int32, acc.shape, 0) & (shift_t - 1)
        prev = jnp.where(row == 0, 0.0, pltpu.roll(acc, 1, 0))
        nxt = jnp.where(row == shift_t - 1, 0.0, pltpu.roll(acc, tm - 1, 0))
        o_ref[...] = acc + mu_ref[0:1, :] * (prev - acc) + mu_ref[1:2, :] * (nxt - acc)

    @pl.when(pl.program_id(1) >= shifted_tiles)
    def _():
        o_ref[...] = acc


def nm_matmul(x, g, sc, sh, w, *, rows_per_mod, tn, mu=None, shift_t=0):
    m, d = x.shape
    n = w.shape[1]
    tpm = rows_per_mod // TM
    in_specs = [
        pl.BlockSpec((TM, d), lambda i, j: (i, 0)),
        pl.BlockSpec((1, d), lambda i, j: (0, 0)),
        pl.BlockSpec((None, 1, d), lambda i, j: (i // tpm, 0, 0)),
        pl.BlockSpec((None, 1, d), lambda i, j: (i // tpm, 0, 0)),
        pl.BlockSpec((d, tn), lambda i, j: (0, j)),
    ]
    args = [x, g.reshape(1, d), sc, sh, w]
    if shift_t:
        in_specs.append(pl.BlockSpec((2, tn), lambda i, j: (0, j)))
        args.append(mu)
    return pl.pallas_call(
        functools.partial(_nm_mm_kernel, shift_t=shift_t),
        grid=(m // TM, n // tn),
        in_specs=in_specs,
        out_specs=pl.BlockSpec((TM, tn), lambda i, j: (i, j)),
        out_shape=jax.ShapeDtypeStruct((m, n), F32),
        scratch_shapes=[pltpu.VMEM((TM, d), BF16)],
        compiler_params=_cparams(("parallel", "arbitrary")),
        name="norm_mod_matmul",
    )(*args)


def _rms_mm_kernel(x_ref, g_ref, w_ref, *out_refs, normalize, emit_norm):
    x = x_ref[...]
    if normalize:
        ms = jnp.mean(x * x, axis=-1, keepdims=True)
        x = x * lax.rsqrt(ms + NORM_EPS) * g_ref[...]
    if emit_norm:
        out_refs[1][...] = x
    out_refs[0][...] = jnp.dot(x.astype(BF16), w_ref[...].astype(BF16),
                               preferred_element_type=F32).astype(out_refs[0].dtype)


def rms_matmul(x, col_blk, k, g, w, *, normalize, emit_norm, out_dtype):
    m = x.shape[0]
    n = w.shape[1]
    tm = min(TM, m)
    out_shape = [jax.ShapeDtypeStruct((m, n), out_dtype)]
    out_specs = [pl.BlockSpec((tm, n), lambda i: (i, 0))]
    if emit_norm:
        out_shape.append(jax.ShapeDtypeStruct((m, k), F32))
        out_specs.append(pl.BlockSpec((tm, k), lambda i: (i, 0)))
    res = pl.pallas_call(
        functools.partial(_rms_mm_kernel, normalize=normalize, emit_norm=emit_norm),
        grid=(m // tm,),
        in_specs=[
            pl.BlockSpec((tm, k), lambda i: (i, col_blk)),
            pl.BlockSpec((1, k), lambda i: (0, 0)),
            pl.BlockSpec((k, n), lambda i: (0, 0)),
        ],
        out_specs=out_specs,
        out_shape=out_shape,
        compiler_params=_cparams(("parallel",)),
        name="rms_matmul",
    )(x, g.reshape(1, k), w)
    return res


def _res_mm_kernel(*refs, n_x):
    x_refs = refs[:n_x]
    w_refs = refs[n_x:2 * n_x]
    res_ref, gate_ref, o_ref = refs[2 * n_x:]
    acc = jnp.dot(x_refs[0][...], w_refs[0][...].astype(BF16), preferred_element_type=F32)
    for xr, wr in zip(x_refs[1:], w_refs[1:]):
        acc = acc + jnp.dot(xr[...], wr[...].astype(BF16), preferred_element_type=F32)
    o_ref[...] = res_ref[...] + gate_ref[...] * acc


def res_matmul(xs, w, res, gate, *, rows_per_mod, tn=512):
    m, n = res.shape
    tpm = rows_per_mod // TM
    in_specs, k0 = [], 0
    for xx in xs:
        in_specs.append(pl.BlockSpec((TM, xx.shape[1]), lambda i, j: (i, 0)))
    for xx in xs:
        kx = xx.shape[1]
        blk = k0 // kx
        in_specs.append(pl.BlockSpec((kx, tn), lambda i, j, blk=blk: (blk, j)))
        k0 += kx
    in_specs.append(pl.BlockSpec((TM, tn), lambda i, j: (i, j)))
    in_specs.append(pl.BlockSpec((None, 1, tn), lambda i, j: (i // tpm, 0, j)))
    return pl.pallas_call(
        functools.partial(_res_mm_kernel, n_x=len(xs)),
        grid=(m // TM, n // tn),
        in_specs=in_specs,
        out_specs=pl.BlockSpec((TM, tn), lambda i, j: (i, j)),
        out_shape=jax.ShapeDtypeStruct((m, n), F32),
        compiler_params=_cparams(("parallel", "parallel")),
        name="residual_matmul",
    )(*xs, *([w] * len(xs)), res, gate)


def _final_norm_kernel(x_ref, g_ref, o_ref):
    x = x_ref[...]
    ms = jnp.mean(x * x, axis=-1, keepdims=True)
    o_ref[...] = x * lax.rsqrt(ms + NORM_EPS) * g_ref[...]


def final_norm(x, g):
    m, d = x.shape
    return pl.pallas_call(
        _final_norm_kernel,
        grid=(m // TM,),
        in_specs=[pl.BlockSpec((TM, d), lambda i: (i, 0)), pl.BlockSpec((1, d), lambda i: (0, 0))],
        out_specs=pl.BlockSpec((TM, d), lambda i: (i, 0)),
        out_shape=jax.ShapeDtypeStruct((m, d), F32),
        compiler_params=_cparams(("parallel",)),
        name="final_norm",
    )(x, g.reshape(1, d))


def _head_ones(width):
    r = lax.broadcasted_iota(jnp.int32, (width, width), 0) // A_HEAD
    c = lax.broadcasted_iota(jnp.int32, (width, width), 1) // A_HEAD
    return (r == c).astype(BF16)


def _head_eye(width):
    r = lax.broadcasted_iota(jnp.int32, (A_HEAD, width), 0)
    c = lax.broadcasted_iota(jnp.int32, (A_HEAD, width), 1) & (A_HEAD - 1)
    return r == c


RWKV_G = 4
RWKV_HPB = 2
RWKV_TB = 256
RWKV_W = RWKV_HPB * LANES
RWKV_MM_GROUPS = 2
RWKV_UNROLL = 8


def _head_sum(z, ones16):
    ones128 = ones16[:LANES, :LANES]
    parts = [_dot_01(z[:, j * LANES:(j + 1) * LANES], ones128, m01_left=False)
             for j in range(z.shape[1] // LANES)]
    return parts[0] if len(parts) == 1 else jnp.concatenate(parts, axis=1)


def _rwkv_scan_kernel(*refs, has_s0, has_acc):
    views = (refs[0:5], refs[5:10])
    w2_ref, w0_ref, a2_ref, a0_ref, kk_ref, ka_ref = refs[10:16]
    rest = refs[16:]
    if has_s0:
        s0_ref, rest = rest[0], rest[1:]
    if has_acc:
        rest = rest[1:]
    yf_ref, yb_ref, sfin_ref, a_s, w_s, b_s, k_s, st_s = rest
    y_refs = (yf_ref, yb_ref)
    n_g, tb_len, width = views[0][0].shape
    n_heads = width // A_HEAD
    tb = pl.program_id(2)
    n_tb = pl.num_programs(2)
    ones16 = _head_ones(width)
    eye = _head_eye(width)
    eye_f32 = eye.astype(F32)
    eye_mask16 = eye_f32.astype(BF16) > 0.5
    zero16 = jnp.zeros((A_HEAD, width), BF16)

    for d in range(2):
        _, k_ref, _, xw_ref, xa_ref = views[d]
        for g in range(n_g):
            kraw = k_ref[g]
            kk = kraw * kk_ref[...]
            kk = kk * lax.rsqrt(jnp.maximum(_head_sum(kk * kk, ones16), 1e-24))
            a_s[d, g] = -kk
            xw = xw_ref[g][:, d * W_LORA:(d + 1) * W_LORA]
            xa = xa_ref[g][:, d * ICL_LORA:(d + 1) * ICL_LORA]
            wl = w0_ref[d] + _dot_x3(jnp.tanh(xw), w2_ref[d])
            w_s[d, g] = jnp.exp(-jnp.exp(-_softplus(-wl) - 0.5))
            a = _sigmoid(a0_ref[d] + _dot_x3(xa, a2_ref[d]))
            k_s[d, g] = kraw * (1.0 + (a - 1.0) * ka_ref[...])
            b_s[d, g] = kk * a

    chains = [(d, g) for d in range(2) for g in range(n_g)]

    @pl.when(tb == 0)
    def _():
        for d, g in chains:
            if has_s0:
                st_s[d, g] = jnp.concatenate([s0_ref[g, d, h] for h in range(n_heads)], axis=1)
            else:
                st_s[d, g] = jnp.zeros((A_HEAD, width), F32)

    group_len = len(chains) // RWKV_MM_GROUPS
    groups = [chains[i * group_len:(i + 1) * group_len] for i in range(RWKV_MM_GROUPS)]

    def readout(yb):
        m = yb * eye_f32
        m8 = m[0:8]
        for i in range(1, A_HEAD // 8):
            m8 = m8 + m[8 * i:8 * i + 8]
        return jnp.sum(m8, axis=0, keepdims=True)

    def step(rows, prev_rows):
        for grp in groups:
            lhs = []
            for d, g in grp:
                r_ref, _, v_ref, _, _ = views[d]
                s = st_s[d, g]
                lhs.append((s * a_s[d, g, rows[d], :]).astype(BF16))
                lhs.append(jnp.where(eye, v_ref[g, rows[d], :], 0.0).astype(BF16))
                lhs.append((s * r_ref[g, prev_rows[d], :]).astype(BF16))
            res = jnp.dot(jnp.concatenate(lhs, axis=0), ones16, preferred_element_type=F32)
            for i, (d, g) in enumerate(grp):
                sa, vb, yb = (res[(3 * i + q) * A_HEAD:(3 * i + q + 1) * A_HEAD] for q in range(3))
                y_refs[d][g, prev_rows[d], :] = readout(yb)
                row = rows[d]
                st_s[d, g] = st_s[d, g] * w_s[d, g, row, :] + sa * b_s[d, g, row, :] + vb * k_s[d, g, row, :]

    def steps(tile, carry):
        n = RWKV_UNROLL
        base_f = pl.multiple_of(tile * n, n)
        base_b = pl.multiple_of(tb_len - n - tile * n, n)
        prev_f = pl.multiple_of(jnp.maximum(tile * n - n, 0), n)
        prev_b = pl.multiple_of(jnp.minimum(tb_len - tile * n, tb_len - n), n)
        for i in range(n):
            rows = [pl.ds(base_f + i, 1), pl.ds(base_b + (n - 1 - i), 1)]
            if i == 0:
                prev_rows = [pl.ds(prev_f + (n - 1), 1), pl.ds(prev_b, 1)]
            else:
                prev_rows = [pl.ds(base_f + (i - 1), 1), pl.ds(base_b + (n - i), 1)]
            step(rows, prev_rows)
        return carry

    lax.fori_loop(0, tb_len // RWKV_UNROLL, steps, 0)
    last_rows = (pl.ds(tb_len - 1, 1), pl.ds(0, 1))
    lhs = [(st_s[d, g] * views[d][0][g, last_rows[d], :]).astype(BF16) for d, g in chains]
    res = jnp.dot(jnp.concatenate(lhs, axis=0), ones16, preferred_element_type=F32)
    for i, (d, g) in enumerate(chains):
        y_refs[d][g, last_rows[d], :] = readout(res[i * A_HEAD:(i + 1) * A_HEAD])

    @pl.when(tb == n_tb - 1)
    def _():
        for d, g in chains:
            s = st_s[d, g]
            for h in range(n_heads):
                sfin_ref[g, d, h] = s[:, h * A_HEAD:(h + 1) * A_HEAD]


def rwkv_scan(u, p, s0, layer_idx, states_acc=None, n_state_layers=1):
    bsz, t_len, _ = u.shape
    state_layer = layer_idx % n_state_layers
    n_tb = t_len // RWKV_TB
    n_hpg = A_WIDTH // RWKV_W
    fwd = lambda tb: tb
    bwd = lambda tb: n_tb - 1 - tb

    def view(tmap):
        def col(first_blk, width=RWKV_W, per_group=True):
            return pl.BlockSpec((RWKV_G, RWKV_TB, width),
                                lambda bi, hg, tb: (bi, tmap(tb), first_blk + (hg if per_group else 0)))
        return [col(0), col(n_hpg), col(2 * n_hpg),
                col(AB_XW_BLK, LANES, False), col(AB_XA_BLK, LANES, False)]

    in_specs = view(fwd) + view(bwd) + [
        pl.BlockSpec((2, W_LORA, RWKV_W), lambda bi, hg, tb: (0, 0, hg)),
        pl.BlockSpec((2, 1, RWKV_W), lambda bi, hg, tb: (0, 0, hg)),
        pl.BlockSpec((2, ICL_LORA, RWKV_W), lambda bi, hg, tb: (0, 0, hg)),
        pl.BlockSpec((2, 1, RWKV_W), lambda bi, hg, tb: (0, 0, hg)),
        pl.BlockSpec((1, RWKV_W), lambda bi, hg, tb: (0, hg)),
        pl.BlockSpec((1, RWKV_W), lambda bi, hg, tb: (0, hg)),
    ]
    args = [u] * 10 + [p['w2'], p['w0'].reshape(2, 1, A_WIDTH), p['a2'], p['a0'].reshape(2, 1, A_WIDTH),
                       p['k_k'].reshape(1, A_WIDTH), p['k_a'].reshape(1, A_WIDTH)]
    heads_blk = 2 * RWKV_HPB
    if s0 is not None:
        in_specs.append(pl.BlockSpec((RWKV_G, None, 2, heads_blk, A_HEAD, A_HEAD),
                                     lambda bi, hg, tb: (bi, layer_idx, 0, hg, 0, 0)))
        args.append(s0)
    aliases = {}
    if states_acc is not None:
        in_specs.append(pl.BlockSpec(memory_space=pl.ANY))
        args.append(states_acc)
        aliases = {len(args) - 1: 2}
    blk = (2, RWKV_G, RWKV_TB, RWKV_W)
    return pl.pallas_call(
        functools.partial(_rwkv_scan_kernel, has_s0=s0 is not None, has_acc=states_acc is not None),
        grid=(bsz // RWKV_G, n_hpg, n_tb),
        in_specs=in_specs,
        out_specs=[
            pl.BlockSpec((RWKV_G, RWKV_TB, RWKV_W), lambda bi, hg, tb: (bi, fwd(tb), hg)),
            pl.BlockSpec((RWKV_G, RWKV_TB, RWKV_W), lambda bi, hg, tb: (bi, bwd(tb), hg)),
            pl.BlockSpec((RWKV_G, None, 2, heads_blk, A_HEAD, A_HEAD),
                         lambda bi, hg, tb: (bi, state_layer, 0, hg, 0, 0)),
        ],
        out_shape=[
            jax.ShapeDtypeStruct((bsz, t_len, A_WIDTH), F32),
            jax.ShapeDtypeStruct((bsz, t_len, A_WIDTH), F32),
            jax.ShapeDtypeStruct((bsz, n_state_layers, 2, A_HEADS, A_HEAD, A_HEAD), F32),
        ],
        input_output_aliases=aliases,
        scratch_shapes=[
            pltpu.VMEM(blk, F32),
            pltpu.VMEM(blk, F32),
            pltpu.VMEM(blk, F32),
            pltpu.VMEM(blk, F32),
            pltpu.VMEM((2, RWKV_G, A_HEAD, RWKV_W), F32),
        ],
        compiler_params=_cparams(("parallel", "parallel", "arbitrary")),
        name="rwkv7_scan",
    )(*args)


def _rwkv_post_kernel(yf_ref, yb_ref, r_ref, k_ref, v_ref, xa_ref, xg_ref,
                      a2_ref, a0_ref, ka_ref, rk_ref, gng_ref, gnb_ref, g2_ref, o_ref):
    ones_bd = _head_ones(LANES)
    inv_n = 1.0 / A_HEAD
    y = yf_ref[...] + yb_ref[...]
    mu = _head_sum(y, ones_bd) * inv_n
    yc = y - mu
    var = _head_sum(yc * yc, ones_bd) * inv_n
    yn = yc * lax.rsqrt(var + GN_EPS) * gng_ref[...] + gnb_ref[...]
    r = r_ref[...]
    kraw = k_ref[...]
    rk_sum = jnp.zeros_like(r)
    for d in range(2):
        xa = xa_ref[:, d * ICL_LORA:(d + 1) * ICL_LORA]
        a = _sigmoid(a0_ref[d] + _dot_x3(xa, a2_ref[d]))
        rk_sum = rk_sum + r * (kraw * (1.0 + (a - 1.0) * ka_ref[...])) * rk_ref[...]
    bonus = _head_sum(rk_sum, ones_bd) * v_ref[...]
    gate = jnp.dot(_sigmoid(xg_ref[...]).astype(BF16), g2_ref[...].astype(BF16), preferred_element_type=F32)
    o_ref[...] = ((yn + bonus) * gate).astype(o_ref.dtype)


def rwkv_post(yf, yb, u2, p):
    m = yf.shape[0]
    tm = 256
    full = lambda shape: pl.BlockSpec(shape, lambda i: (0,) * len(shape))
    row = lambda width, blk: pl.BlockSpec((tm, width), lambda i: (i, blk))
    return pl.pallas_call(
        _rwkv_post_kernel,
        grid=(m // tm,),
        in_specs=[
            row(A_WIDTH, 0), row(A_WIDTH, 0),
            row(A_WIDTH, 0), row(A_WIDTH, 1), row(A_WIDTH, 2),
            row(LANES, AB_XA_BLK), row(G_LORA_PAD, AB_XG_BLK),
            full((2, ICL_LORA, A_WIDTH)), full((2, 1, A_WIDTH)),
            full((1, A_WIDTH)), full((1, A_WIDTH)), full((1, A_WIDTH)), full((1, A_WIDTH)),
            full((G_LORA_PAD, A_WIDTH)),
        ],
        out_specs=row(A_WIDTH, 0),
        out_shape=jax.ShapeDtypeStruct((m, A_WIDTH), BF16),
        compiler_params=_cparams(("parallel",)),
        name="rwkv7_post",
    )(yf, yb, u2, u2, u2, u2, u2, p['a2'], p['a0'].reshape(2, 1, A_WIDTH), p['k_a'].reshape(1, A_WIDTH),
      p['r_k'].reshape(1, A_WIDTH), p['gn_g'].reshape(1, A_WIDTH), p['gn_b'].reshape(1, A_WIDTH), p['g2'])


def rwkv_mix(u, p, s0, layer_idx, states_acc=None, n_state_layers=1):
    bsz, t_len, _ = u.shape
    m = bsz * t_len
    yf, yb, sfin = rwkv_scan(u, p, s0, layer_idx, states_acc, n_state_layers)
    out = rwkv_post(yf.reshape(m, A_WIDTH), yb.reshape(m, A_WIDTH), u.reshape(m, AB_PAD_COLS), p)
    return out, sfin


def _mla_kernel(qn_ref, qr_ref, kn_ref, kr_ref, v_ref, o_ref):
    scale = (NOPE + ROPE_D) ** -0.5
    krb = kr_ref[...].astype(BF16)
    nt = (((1,), (1,)), ((), ()))
    for h in range(B_HEADS):
        qn = qn_ref[:, h * NOPE:(h + 1) * NOPE].astype(BF16)
        qr = qr_ref[:, h * ROPE_D:(h + 1) * ROPE_D].astype(BF16)
        s = lax.dot_general(qn, kn_ref[:, h * NOPE:(h + 1) * NOPE], nt, preferred_element_type=F32)
        s = s + lax.dot_general(qr, krb, nt, preferred_element_type=F32)
        s = s * scale
        m = jnp.max(s, axis=-1, keepdims=True)
        e = jnp.exp(s - m)
        p = e / jnp.sum(e, axis=-1, keepdims=True)
        o = jnp.dot(p.astype(BF16), v_ref[:, h * V_HEAD:(h + 1) * V_HEAD], preferred_element_type=F32)
        o_ref[:, h * V_HEAD:(h + 1) * V_HEAD] = o.astype(o_ref.dtype)


def mla_attention(q, kv, kr, *, tq):
    bsz, t_len, _ = q.shape
    s_len = kv.shape[1]
    hn = B_HEADS * NOPE
    return pl.pallas_call(
        _mla_kernel,
        grid=(bsz, t_len // tq),
        in_specs=[
            pl.BlockSpec((None, tq, hn), lambda b, i: (b, i, 0)),
            pl.BlockSpec((None, tq, B_HEADS * ROPE_D), lambda b, i: (b, i, hn // (B_HEADS * ROPE_D))),
            pl.BlockSpec((None, s_len, hn), lambda b, i: (b, 0, 0)),
            pl.BlockSpec((None, s_len, ROPE_D), lambda b, i: (b, 0, 0)),
            pl.BlockSpec((None, s_len, hn), lambda b, i: (b, 0, 1)),
        ],
        out_specs=pl.BlockSpec((None, tq, hn), lambda b, i: (b, i, 0)),
        out_shape=jax.ShapeDtypeStruct((bsz, t_len, hn), BF16),
        compiler_params=_cparams(("parallel", "parallel")),
        name="mla_attention",
    )(q, q, kv, kr, kv)


GLA_SUPER = 4


def _gla_kernel(q_ref, k_ref, v_ref, og_ref, xg_ref, gw_ref, gb_ref, gn_ref, *rest, has_s0, has_acc):
    if has_s0:
        s0_ref, rest = rest[0], rest[1:]
    if has_acc:
        rest = rest[1:]
    o_ref, sfin_ref = rest[:2]
    lg_s, y_s, st_s, qt_s, u_s, dec_s = (rest[2 + 2 * i:4 + 2 * i] for i in range(6))
    t_len = q_ref.shape[0]
    n_chunks = t_len // CHUNK
    scale = C_DK ** -0.5
    nt = (((1,), (1,)), ((), ()))
    xg = xg_ref[...]
    span = GLA_SUPER * CHUNK
    ri = lax.broadcasted_iota(jnp.int32, (span, span), 0)
    ci = lax.broadcasted_iota(jnp.int32, (span, span), 1)
    same_chunk = (ri // CHUNK) == (ci // CHUNK)
    keeps = (same_chunk & (ri >= ci), same_chunk & (ri <= ci))
    tris = tuple(kp.astype(BF16) for kp in keeps)
    for d in range(2):
        z = _dot_x3(xg[:, d * GATE_LORA:(d + 1) * GATE_LORA], gw_ref[d], stack=False) + gb_ref[d]
        lg_s[d][...] = -_softplus(-z) * (1.0 / GATE_TAU)
        if has_s0:
            st_s[d][...] = s0_ref[d].T
        else:
            st_s[d][...] = jnp.zeros(st_s[d].shape, F32)

    def chunks_local(sc, carry):
        rows = pl.ds(pl.multiple_of(sc * span, span), span)
        q = q_ref[rows, :] * scale
        k = k_ref[rows, :]
        v = v_ref[rows, :]
        vb = v.astype(BF16)
        for d in range(2):
            b = _dot_01(tris[d], lg_s[d][rows, :])
            edge = CHUNK - 1 if d == 0 else 0
            bls = [b[cc * CHUNK + edge:cc * CHUNK + edge + 1, :] for cc in range(GLA_SUPER)]
            bl = jnp.concatenate([jnp.broadcast_to(x, (CHUNK, C_DK)) for x in bls], axis=0)
            qt = (q * jnp.exp(b)).astype(BF16)
            kt = (k * jnp.exp(-b)).astype(BF16)
            kd = (k * jnp.exp(bl - b)).astype(BF16)
            att = jnp.where(keeps[d], lax.dot_general(qt, kt, nt, preferred_element_type=F32), 0.0)
            y_s[d][rows, :] = jnp.dot(att.astype(BF16), vb, preferred_element_type=F32)
            qt_s[d][rows, :] = qt
            for cc in range(GLA_SUPER):
                part = slice(cc * CHUNK, (cc + 1) * CHUNK)
                u_s[d][sc * GLA_SUPER + cc] = jnp.dot(v[part].T.astype(BF16), kd[part], preferred_element_type=F32)
                dec_s[d][sc * GLA_SUPER + cc] = jnp.broadcast_to(jnp.exp(bls[cc]), (8, C_DK))
        return carry

    lax.fori_loop(0, n_chunks // GLA_SUPER, chunks_local, 0)

    def chunk_state(i, carry):
        for d in range(2):
            c = i if d == 0 else n_chunks - 1 - i
            rows = pl.ds(pl.multiple_of(c * CHUNK, CHUNK), CHUNK)
            st = st_s[d][...]
            y_s[d][rows, :] = y_s[d][rows, :] + lax.dot_general(qt_s[d][rows, :], st.astype(BF16), nt,
                                                                preferred_element_type=F32)
            st_s[d][...] = st * dec_s[d][c, 0:1, :] + u_s[d][c]
        return carry

    lax.fori_loop(0, n_chunks, chunk_state, 0)
    for d in range(2):
        sfin_ref[d] = st_s[d][...].T
    y = y_s[0][...] + y_s[1][...]
    yn = y * lax.rsqrt(jnp.mean(y * y, axis=-1, keepdims=True) + NORM_EPS) * gn_ref[...]
    o_ref[...] = (yn * _silu(og_ref[...])).astype(o_ref.dtype)


def gla_mix(u, p, s0, layer_idx, states_acc=None, n_state_layers=1):
    bsz, t_len, _ = u.shape
    state_layer = layer_idx % n_state_layers
    in_specs = [
        pl.BlockSpec((None, t_len, C_DK), lambda b, h: (b, 0, h)),
        pl.BlockSpec((None, t_len, C_DK), lambda b, h: (b, 0, C_HEADS + h)),
        pl.BlockSpec((None, t_len, C_DV), lambda b, h: (b, 0, C_HEADS + h)),
        pl.BlockSpec((None, t_len, C_DV), lambda b, h: (b, 0, 2 * C_HEADS + h)),
        pl.BlockSpec((None, t_len, LANES), lambda b, h: (b, 0, C_XG_BLK)),
        pl.BlockSpec((2, GATE_LORA, C_DK), lambda b, h: (0, 0, h)),
        pl.BlockSpec((2, 1, C_DK), lambda b, h: (0, 0, h)),
        pl.BlockSpec((1, C_DV), lambda b, h: (0, 0)),
    ]
    args = [u, u, u, u, u, p['gate_w2'], p['gate_b'].reshape(2, 1, C_QK), p['gn'].reshape(1, C_DV)]
    if s0 is not None:
        in_specs.append(pl.BlockSpec((None, None, 2, None, C_DK, C_DV), lambda b, h: (b, layer_idx, 0, h, 0, 0)))
        args.append(s0)
    aliases = {}
    if states_acc is not None:
        in_specs.append(pl.BlockSpec(memory_space=pl.ANY))
        args.append(states_acc)
        aliases = {len(args) - 1: 1}
    return pl.pallas_call(
        functools.partial(_gla_kernel, has_s0=s0 is not None, has_acc=states_acc is not None),
        grid=(bsz, C_HEADS),
        in_specs=in_specs,
        out_specs=[
            pl.BlockSpec((None, t_len, C_DV), lambda b, h: (b, 0, h)),
            pl.BlockSpec((None, None, 2, None, C_DK, C_DV), lambda b, h: (b, state_layer, 0, h, 0, 0)),
        ],
        out_shape=[
            jax.ShapeDtypeStruct((bsz, t_len, C_V), BF16),
            jax.ShapeDtypeStruct((bsz, n_state_layers, 2, C_HEADS, C_DK, C_DV), F32),
        ],
        input_output_aliases=aliases,
        scratch_shapes=[
            pltpu.VMEM((t_len, C_DK), F32),
            pltpu.VMEM((t_len, C_DK), F32),
            pltpu.VMEM((t_len, C_DV), F32),
            pltpu.VMEM((t_len, C_DV), F32),
            pltpu.VMEM((C_DV, C_DK), F32),
            pltpu.VMEM((C_DV, C_DK), F32),
            pltpu.VMEM((t_len, C_DK), BF16),
            pltpu.VMEM((t_len, C_DK), BF16),
            pltpu.VMEM((t_len // CHUNK, C_DV, C_DK), F32),
            pltpu.VMEM((t_len // CHUNK, C_DV, C_DK), F32),
            pltpu.VMEM((t_len // CHUNK, 8, C_DK), F32),
            pltpu.VMEM((t_len // CHUNK, 8, C_DK), F32),
        ],
        compiler_params=_cparams(("parallel", "parallel")),
        name="gla_mix",
    )(*args)


def _moe_col_tile(t_len, d):
    return d if t_len <= 256 else 512


def _dispatch_kernel(x_ref, g_ref, sc_ref, sh_ref, wr_ref, xe_ref, rank_ref, aff_ref, hb_s, onehot_s, *, cap):
    t_len = x_ref.shape[0]
    j = pl.program_id(1)

    @pl.when(j == 0)
    def _():
        h = _norm_mod(x_ref[...], g_ref[...], sc_ref[...], sh_ref[...])
        tn = hb_s.shape[2]
        for n in range(hb_s.shape[0]):
            hb_s[n] = h[:, n * tn:(n + 1) * tn].astype(BF16)
        logits = _dot_x3(h, wr_ref[...], stack=False)
        lane = lax.broadcasted_iota(jnp.int32, logits.shape, 1)
        logits = jnp.where(lane < N_EXPERTS, logits, -jnp.inf)
        m = jnp.max(logits, axis=-1, keepdims=True)
        e = jnp.exp(logits - m)
        aff = e / jnp.sum(e, axis=-1, keepdims=True)
        aff_ref[...] = aff
        aff_t = aff.T
        sub_i = lax.broadcasted_iota(jnp.int32, (t_len, t_len), 0)
        lane_i = lax.broadcasted_iota(jnp.int32, (t_len, t_len), 1)
        slot = lax.broadcasted_iota(jnp.int32, (cap, t_len), 0).astype(F32)
        expert_row = lax.broadcasted_iota(jnp.int32, (N_EXPERTS, t_len), 0)
        ranks = jnp.zeros((N_EXPERTS, t_len), F32)
        for ex in range(N_EXPERTS):
            col = aff[:, ex:ex + 1]
            row = aff_t[ex:ex + 1, :]
            ahead = (col > row) | ((col == row) & (sub_i < lane_i))
            rank_row = jnp.sum(ahead.astype(F32), axis=0, keepdims=True)
            onehot_s[ex * cap:(ex + 1) * cap, :] = (rank_row == slot).astype(BF16)
            ranks = jnp.where(expert_row == ex, rank_row, ranks)
        pad = jnp.zeros((LANES - N_EXPERTS, t_len), F32)
        rank_ref[...] = jnp.concatenate([ranks, pad], axis=0).T

    rows = jnp.dot(onehot_s[...], hb_s[j], preferred_element_type=F32).astype(BF16)
    for ex in range(N_EXPERTS):
        xe_ref[ex] = rows[ex * cap:(ex + 1) * cap]


def moe_dispatch(x, g, sc, sh, w_router_pad, cap):
    bsz, t_len, d = x.shape
    per_req = sc.shape[0] > 1
    tn = _moe_col_tile(t_len, d)
    mod_spec = pl.BlockSpec((None, 1, d), (lambda b, j: (b, 0, 0)) if per_req else (lambda b, j: (0, 0, 0)))
    return pl.pallas_call(
        functools.partial(_dispatch_kernel, cap=cap),
        grid=(bsz, d // tn),
        in_specs=[
            pl.BlockSpec((None, t_len, d), lambda b, j: (b, 0, 0)),
            pl.BlockSpec((1, d), lambda b, j: (0, 0)),
            mod_spec, mod_spec,
            pl.BlockSpec((d, LANES), lambda b, j: (0, 0)),
        ],
        out_specs=[
            pl.BlockSpec((N_EXPERTS, cap, tn), lambda b, j: (0, b, j)),
            pl.BlockSpec((None, t_len, LANES), lambda b, j: (b, 0, 0)),
            pl.BlockSpec((None, t_len, LANES), lambda b, j: (b, 0, 0)),
        ],
        out_shape=[
            jax.ShapeDtypeStruct((N_EXPERTS, bsz * cap, d), BF16),
            jax.ShapeDtypeStruct((bsz, t_len, LANES), F32),
            jax.ShapeDtypeStruct((bsz, t_len, LANES), F32),
        ],
        scratch_shapes=[
            pltpu.VMEM((d // tn, t_len, tn), BF16),
            pltpu.VMEM((N_EXPERTS * cap, t_len), BF16),
        ],
        compiler_params=_cparams(("parallel", "arbitrary")),
        name="moe_dispatch",
    )(x, g.reshape(1, d), sc, sh, w_router_pad)


EXPERT_TF = 256
EXPERT_TN = 256
N_UP_STEPS = D_EXPERT // EXPERT_TF
N_DOWN_STEPS = D_MODEL // EXPERT_TN


def _experts_kernel(xc_ref, xl_ref, wg_ref, wu_ref, wd_ref, yc_ref, yl_ref, hid_s):
    s = pl.program_id(1)
    mc = xc_ref.shape[0]

    @pl.when(s < N_UP_STEPS)
    def _():
        wg = wg_ref[...].astype(BF16)
        wu = wu_ref[...].astype(BF16)
        cols = pl.ds(pl.multiple_of(s * EXPERT_TF, EXPERT_TF), EXPERT_TF)
        for x_ref, r0 in ((xc_ref, 0), (xl_ref, mc)):
            x = x_ref[...]
            hg = jnp.dot(x, wg, preferred_element_type=F32)
            hu = jnp.dot(x, wu, preferred_element_type=F32)
            hid_s[r0:r0 + x.shape[0], cols] = (_silu(hg) * hu).astype(BF16)

    @pl.when(s >= N_UP_STEPS)
    def _():
        wd = wd_ref[...].astype(BF16)
        yc_ref[...] = jnp.dot(hid_s[0:mc, :], wd, preferred_element_type=F32).astype(yc_ref.dtype)
        yl_ref[...] = jnp.dot(hid_s[mc:, :], wd, preferred_element_type=F32).astype(yl_ref.dtype)


def moe_experts(xc, xl, w_gate, w_up, w_down, layer):
    mc, ml = xc.shape[1], xl.shape[1]
    up_idx = lambda e, s: (layer, e, 0, jnp.minimum(s, N_UP_STEPS - 1))
    down_idx = lambda e, s: (layer, e, 0, jnp.maximum(s - N_UP_STEPS, 0))
    out_idx = lambda e, s: (e, 0, jnp.maximum(s - N_UP_STEPS, 0))
    return pl.pallas_call(
        _experts_kernel,
        grid=(N_EXPERTS, N_UP_STEPS + N_DOWN_STEPS),
        in_specs=[
            pl.BlockSpec((None, mc, D_MODEL), lambda e, s: (e, 0, 0)),
            pl.BlockSpec((None, ml, D_MODEL), lambda e, s: (e, 0, 0)),
            pl.BlockSpec((None, None, D_MODEL, EXPERT_TF), up_idx),
            pl.BlockSpec((None, None, D_MODEL, EXPERT_TF), up_idx),
            pl.BlockSpec((None, None, D_EXPERT, EXPERT_TN), down_idx),
        ],
        out_specs=[
            pl.BlockSpec((None, mc, EXPERT_TN), out_idx),
            pl.BlockSpec((None, ml, EXPERT_TN), out_idx),
        ],
        out_shape=[
            jax.ShapeDtypeStruct((N_EXPERTS, mc, D_MODEL), BF16),
            jax.ShapeDtypeStruct((N_EXPERTS, ml, D_MODEL), BF16),
        ],
        scratch_shapes=[pltpu.VMEM((mc + ml, D_EXPERT), BF16)],
        compiler_params=_cparams(("parallel", "arbitrary")),
        name="moe_experts",
    )(xc, xl, w_gate, w_up, w_down)


def _combine_kernel(x_ref, gate_ref, ye_ref, rank_ref, aff_ref, o_ref, hi_s, lo_s, *, cap):
    t_len = x_ref.shape[0]

    @pl.when(pl.program_id(1) == 0)
    def _():
        per_group = LANES // cap
        lane = lax.broadcasted_iota(jnp.int32, (t_len, LANES), 1)
        slot = (lane % cap).astype(F32)
        for grp in range(N_EXPERTS // per_group):
            w = jnp.zeros((t_len, LANES), F32)
            for i in range(per_group):
                ex = grp * per_group + i
                hit = rank_ref[:, ex:ex + 1] == slot
                if per_group > 1:
                    hit = hit & (lane // cap == i)
                w = jnp.where(hit, aff_ref[:, ex:ex + 1], w)
            hi = w.astype(BF16)
            hi_s[:, grp * LANES:(grp + 1) * LANES] = hi
            lo_s[:, grp * LANES:(grp + 1) * LANES] = (w - hi.astype(F32)).astype(BF16)

    ye = ye_ref[...].reshape(N_EXPERTS * cap, ye_ref.shape[2])
    acc = jnp.dot(hi_s[...], ye, preferred_element_type=F32) + jnp.dot(lo_s[...], ye, preferred_element_type=F32)
    o_ref[...] = x_ref[...] + gate_ref[...] * acc


def moe_combine(x, gate, ye, rank, aff, cap):
    bsz, t_len, d = x.shape
    per_req = gate.shape[0] > 1
    tn = _moe_col_tile(t_len, d)
    return pl.pallas_call(
        functools.partial(_combine_kernel, cap=cap),
        grid=(bsz, d // tn),
        in_specs=[
            pl.BlockSpec((None, t_len, tn), lambda b, j: (b, 0, j)),
            pl.BlockSpec((None, 1, tn), (lambda b, j: (b, 0, j)) if per_req else (lambda b, j: (0, 0, j))),
            pl.BlockSpec((N_EXPERTS, cap, tn), lambda b, j: (0, b, j)),
            pl.BlockSpec((None, t_len, LANES), lambda b, j: (b, 0, 0)),
            pl.BlockSpec((None, t_len, LANES), lambda b, j: (b, 0, 0)),
        ],
        out_specs=pl.BlockSpec((None, t_len, tn), lambda b, j: (b, 0, j)),
        out_shape=jax.ShapeDtypeStruct((bsz, t_len, d), F32),
        scratch_shapes=[pltpu.VMEM((t_len, N_EXPERTS * cap), BF16), pltpu.VMEM((t_len, N_EXPERTS * cap), BF16)],
        compiler_params=_cparams(("parallel", "arbitrary")),
        name="moe_combine",
    )(x, gate, ye, rank, aff)


def _pad_cols(w, n):
    return jnp.pad(w, ((0, 0),) * (w.ndim - 1) + ((0, n - w.shape[-1]),))


def _ab_layout(w):
    rkv = w[..., :3 * A_WIDTH]
    o = 3 * A_WIDTH
    xg = _pad_cols(w[..., o:o + G_LORA], G_LORA_PAD)
    o += G_LORA
    xw = w[..., o:o + 2 * W_LORA]
    o += 2 * W_LORA
    xa = w[..., o:o + 2 * ICL_LORA]
    o += 2 * ICL_LORA
    rest = w[..., o:]
    return _pad_cols(jnp.concatenate([rkv, xg, xw, xa, rest], axis=-1), AB_PAD_COLS)


def _c_layout(w):
    qkv = w[:, :2 * C_QK + C_V]
    o = 2 * C_QK + C_V
    xg = _pad_cols(w[:, o:o + 2 * GATE_LORA], LANES)
    og = w[:, o + 2 * GATE_LORA:]
    return _pad_cols(jnp.concatenate([qkv, og, xg], axis=-1), C_PAD_COLS)


def _axial_rope_tables(t):
    rows = t // GRID_W
    row = jnp.repeat(jnp.arange(rows, dtype=F32), GRID_W)
    col = jnp.tile(jnp.arange(GRID_W, dtype=F32), rows)
    half = ROPE_D // 2
    inv = 1.0 / (ROPE_BASE ** (jnp.arange(0, half, 2, dtype=F32) / half))
    ang_r = row[:, None] * inv[None, :]
    ang_c = col[:, None] * inv[None, :]
    return (jnp.cos(ang_r), jnp.sin(ang_r), jnp.cos(ang_c), jnp.sin(ang_c))


def _rotate(x, cos, sin):
    m = x.shape[-1] // 2
    x1, x2 = x[..., :m], x[..., m:]
    c = cos[None, :, None, :]
    s = sin[None, :, None, :]
    return jnp.concatenate([x1 * c - x2 * s, x1 * s + x2 * c], axis=-1)


def _apply_axial_rope(x, tabs):
    cr, sr, cc, sc = tabs
    half = ROPE_D // 2
    return jnp.concatenate([_rotate(x[..., :half], cr, sr), _rotate(x[..., half:], cc, sc)], axis=-1)


def _mixer_layer(x2, l, mod_l, P, caches, rope_tabs, bsz, t_len, outs):
    m = bsz * t_len
    is_ctx = caches is None
    rows_per_mod = m if is_ctx else t_len
    sh1, sc1, g1 = mod_l[0], mod_l[1], mod_l[2]
    i = l // 2
    if l % 2 == 0:
        u = nm_matmul(x2, P['g_norm_mix'][l], sc1, sh1, P['ab_w_in_pad'][i], rows_per_mod=rows_per_mod,
                      tn=512, mu=P['rwkv_mu_pad'][i], shift_t=t_len)
        p = dict(w2=P['rwkv_w2'][i], w0=P['rwkv_w0'][i], a2=P['rwkv_a2'][i], a0=P['rwkv_a0'][i],
                 g2=P['rwkv_g2_pad'][i], k_k=P['rwkv_k_k'][i], k_a=P['rwkv_k_a'][i], r_k=P['rwkv_r_k'][i],
                 gn_g=P['rwkv_gn_g'][i], gn_b=P['rwkv_gn_b'][i])
        u3 = u.reshape(bsz, t_len, AB_PAD_COLS)
        if is_ctx:
            a_out, a_state = rwkv_mix(u3, p, None, i, outs['rwkv'], P['n_ab'])
            outs['rwkv'] = a_state
        else:
            a_out, _ = rwkv_mix(u3, p, caches[2], i)
        (q,) = rms_matmul(u, AB_CQ_BLK, Q_LORA, P['mla_q_norm'][i], P['mla_w_uq_r'][i],
                          normalize=True, emit_norm=False, out_dtype=F32)
        kv, ckv = rms_matmul(u, AB_CKV_BLK, KV_LORA, P['mla_kv_norm'][i], P['mla_w_ukv_r'][i],
                             normalize=True, emit_norm=True, out_dtype=BF16)
        kr = u3[:, :, AB_KR_COL:AB_KR_COL + ROPE_D]
        q = q.reshape(bsz, t_len, -1)
        kv = kv.reshape(bsz, t_len, -1)
        if is_ctx:
            outs['ckv'].append(ckv.reshape(bsz, t_len, KV_LORA))
            outs['kr'].append(kr)
            keys_r = kr
        else:
            hn = B_HEADS * NOPE
            q_rope = _apply_axial_rope(q[..., hn:].reshape(bsz, t_len, B_HEADS, ROPE_D), rope_tabs)
            q = jnp.concatenate([q[..., :hn], q_rope.reshape(bsz, t_len, B_HEADS * ROPE_D)], axis=-1)
            kr = _apply_axial_rope(kr[:, :, None, :], rope_tabs)[:, :, 0, :]
            past = caches[0].shape[2]
            (kv_ctx,) = rms_matmul(caches[0][:, i].reshape(bsz * past, KV_LORA), 0, KV_LORA,
                                   P['mla_kv_norm'][i], P['mla_w_ukv_r'][i],
                                   normalize=False, emit_norm=False, out_dtype=BF16)
            kv = jnp.concatenate([kv, kv_ctx.reshape(bsz, past, -1)], axis=1)
            keys_r = jnp.concatenate([kr, caches[1][:, i]], axis=1)
        b_out = mla_attention(q, kv, keys_r, tq=256)
        return res_matmul([a_out, b_out.reshape(m, A_WIDTH)], P['ab_w_out'][i], x2, g1,
                          rows_per_mod=rows_per_mod)
    u = nm_matmul(x2, P['g_norm_mix'][l], sc1, sh1, P['c_w_in_pad'][i], rows_per_mod=rows_per_mod, tn=512)
    p = dict(gate_w2=P['gla_gate_w2'][i], gate_b=P['gla_gate_b'][i], gn=P['gla_norm'][i])
    if is_ctx:
        o, outs['gla'] = gla_mix(u.reshape(bsz, t_len, C_PAD_COLS), p, None, i, outs['gla'], P['n_c'])
    else:
        o, _ = gla_mix(u.reshape(bsz, t_len, C_PAD_COLS), p, caches[3], i)
    return res_matmul([o.reshape(m, C_V)], P['c_w_out'][i], x2, g1, rows_per_mod=rows_per_mod)


def kernel(x_prompt, x_sample, cache_mla_ckv, cache_mla_krope, state_rwkv, state_gla, c, c_ctx, w_mod, b_mod, g_norm_mix, g_norm_ffn, ab_w_in, rwkv_mu, rwkv_w0, rwkv_w2, rwkv_a0, rwkv_a2, rwkv_g2, rwkv_k_k, rwkv_k_a, rwkv_r_k, rwkv_gn_g, rwkv_gn_b, mla_q_norm, mla_w_uq, mla_kv_norm, mla_w_ukv, ab_w_out, c_w_in, gla_gate_w2, gla_gate_b, gla_norm, c_w_out, moe_router, moe_w_gate, moe_w_up, moe_w_down, g_final):
    n_ab = ab_w_in.shape[0]
    n_c = c_w_in.shape[0]
    w_uq = mla_w_uq.reshape(n_ab, Q_LORA, B_HEADS, NOPE + ROPE_D)
    w_ukv = mla_w_ukv.reshape(n_ab, KV_LORA, B_HEADS, NOPE + V_HEAD)
    P = dict(
        n_ab=n_ab, n_c=n_c, g_norm_mix=g_norm_mix, g_norm_ffn=g_norm_ffn, g_final=g_final,
        ab_w_in_pad=_ab_layout(ab_w_in.astype(BF16)), rwkv_mu_pad=_ab_layout(rwkv_mu),
        rwkv_w0=rwkv_w0, rwkv_w2=rwkv_w2, rwkv_a0=rwkv_a0, rwkv_a2=rwkv_a2,
        rwkv_g2_pad=jnp.pad(rwkv_g2, ((0, 0), (0, G_LORA_PAD - G_LORA), (0, 0))),
        rwkv_k_k=rwkv_k_k, rwkv_k_a=rwkv_k_a, rwkv_r_k=rwkv_r_k.reshape(n_ab, A_WIDTH),
        rwkv_gn_g=rwkv_gn_g, rwkv_gn_b=rwkv_gn_b,
        mla_q_norm=mla_q_norm, mla_kv_norm=mla_kv_norm,
        mla_w_uq_r=jnp.concatenate([w_uq[..., :NOPE].reshape(n_ab, Q_LORA, -1),
                                    w_uq[..., NOPE:].reshape(n_ab, Q_LORA, -1)], axis=-1).astype(BF16),
        mla_w_ukv_r=jnp.concatenate([w_ukv[..., :NOPE].reshape(n_ab, KV_LORA, -1),
                                     w_ukv[..., NOPE:].reshape(n_ab, KV_LORA, -1)], axis=-1).astype(BF16),
        ab_w_out=ab_w_out.astype(BF16),
        c_w_in_pad=jnp.stack([_c_layout(c_w_in[i].astype(BF16)) for i in range(n_c)]),
        gla_gate_w2=gla_gate_w2, gla_gate_b=gla_gate_b, gla_norm=gla_norm, c_w_out=c_w_out.astype(BF16),
    )
    router_pad = _pad_cols(moe_router, LANES)

    cvecs = jnp.concatenate([c_ctx[None, :], c, jnp.zeros((8 - 1 - c.shape[0], D_MODEL), F32)], axis=0)
    mods = modulation_all(cvecs, w_mod, b_mod)
    n_lat = c.shape[0]

    rope_tabs = _axial_rope_tables(x_sample.shape[1])
    caches = (cache_mla_ckv, cache_mla_krope, state_rwkv, state_gla)
    bc, tc, _ = x_prompt.shape
    bl, tl, _ = x_sample.shape
    cap_c = CAP_FACTOR * tc // N_EXPERTS
    cap_l = CAP_FACTOR * tl // N_EXPERTS
    xc = x_prompt.reshape(bc * tc, D_MODEL)
    xl = x_sample.reshape(bl * tl, D_MODEL)
    outs = dict(ckv=[], kr=[], rwkv=None, gla=None)
    for l in range(DEPTH):
        mod_c = [t[:, None, :] for t in jnp.split(mods[l, 0:1], N_MOD, axis=-1)]
        mod_l = [t[:, None, :] for t in jnp.split(mods[l, 1:1 + n_lat], N_MOD, axis=-1)]
        xc = _mixer_layer(xc, l, mod_c, P, None, None, bc, tc, outs)
        xl = _mixer_layer(xl, l, mod_l, P, caches, rope_tabs, bl, tl, None)
        xc3 = xc.reshape(bc, tc, D_MODEL)
        xl3 = xl.reshape(bl, tl, D_MODEL)
        xe_c, rank_c, aff_c = moe_dispatch(xc3, g_norm_ffn[l], mod_c[4], mod_c[3], router_pad[l], cap_c)
        xe_l, rank_l, aff_l = moe_dispatch(xl3, g_norm_ffn[l], mod_l[4], mod_l[3], router_pad[l], cap_l)
        ye_c, ye_l = moe_experts(xe_c, xe_l, moe_w_gate, moe_w_up, moe_w_down, l)
        xc = moe_combine(xc3, mod_c[5], ye_c, rank_c, aff_c, cap_c).reshape(bc * tc, D_MODEL)
        xl = moe_combine(xl3, mod_l[5], ye_l, rank_l, aff_l, cap_l).reshape(bl * tl, D_MODEL)
    y_prompt = final_norm(xc, g_final).reshape(bc, tc, D_MODEL)
    y_sample = final_norm(xl, g_final).reshape(bl, tl, D_MODEL)
    return (y_prompt, y_sample, jnp.stack(outs['ckv'], axis=1), jnp.stack(outs['kr'], axis=1),
            outs['rwkv'], outs['gla'])
```

```python
import functools

import jax
import jax.numpy as jnp
from jax import lax
from jax.experimental import pallas as pl
from jax.experimental.pallas import tpu as pltpu

F32 = jnp.float32
BF16 = jnp.bfloat16
HIGHEST = lax.Precision.HIGHEST

VMEM_LIMIT_BYTES = 56 * 1024 * 1024
LANES = 128

D_MODEL = 2048
DEPTH = 4
GRID_W = 64
N_MOD = 6
NORM_EPS = 1e-6

A_WIDTH = 1024
A_HEAD = 64
A_HEADS = 16
W_LORA = 64
ICL_LORA = 64
G_LORA = 160
G_LORA_PAD = 256
GN_EPS = 64e-5

B_HEADS = 8
NOPE = 128
ROPE_D = 64
V_HEAD = 128
Q_LORA = 512
KV_LORA = 512
ROPE_BASE = 10000.0

C_HEADS = 4
C_QK = 1024
C_V = 2048
C_DK = 256
C_DV = 512
GATE_LORA = 16
GATE_TAU = 16.0
CHUNK = 64

N_EXPERTS = 16
D_EXPERT = 4096
CAP_FACTOR = 2

AB_PAD_COLS = 5120
AB_XG_BLK = 3072 // G_LORA_PAD
AB_XW_BLK = 3328 // LANES
AB_XA_BLK = 3456 // LANES
AB_CQ_BLK = 3584 // Q_LORA
AB_CKV_BLK = 4096 // KV_LORA
AB_KR_COL = 4608
AB_SHIFT_COLS = 3584
C_PAD_COLS = 6656
C_XG_BLK = 6144 // LANES

TM = 1024


def _cparams(sem):
    return pltpu.CompilerParams(dimension_semantics=sem, vmem_limit_bytes=VMEM_LIMIT_BYTES)


def _sigmoid(x):
    return 1.0 / (1.0 + jnp.exp(-x))


def _softplus(x):
    return jnp.maximum(x, 0.0) + jnp.log(1.0 + jnp.exp(-jnp.abs(x)))


def _silu(x):
    return x * _sigmoid(x)


def _dot_01(a, b, m01_left=True):
    x = b if m01_left else a
    hi = x.astype(BF16)
    r1 = x - hi.astype(F32)
    mid = r1.astype(BF16)
    lo = (r1 - mid.astype(F32)).astype(BF16)
    if m01_left:
        return jnp.dot(jnp.concatenate([a, a, a], axis=1), jnp.concatenate([hi, mid, lo], axis=0),
                       preferred_element_type=F32)
    return jnp.dot(jnp.concatenate([hi, mid, lo], axis=1), jnp.concatenate([b, b, b], axis=0),
                   preferred_element_type=F32)


def _dot_x3(a, b, stack=True):
    ah = a.astype(BF16)
    al = (a - ah.astype(F32)).astype(BF16)
    bh = b.astype(BF16)
    bl = (b - bh.astype(F32)).astype(BF16)
    if stack:
        return jnp.dot(jnp.concatenate([ah, ah, al], axis=1), jnp.concatenate([bh, bl, bh], axis=0),
                       preferred_element_type=F32)
    return (jnp.dot(ah, bh, preferred_element_type=F32) + jnp.dot(ah, bl, preferred_element_type=F32)
            + jnp.dot(al, bh, preferred_element_type=F32))


def _mod_kernel(c_ref, w_ref, b_ref, o_ref):
    a = _silu(c_ref[...])
    o_ref[...] = _dot_x3(a, w_ref[...], stack=False) + b_ref[...]


def modulation_all(cvecs, w_mod, b_mod):
    tn = 1024
    n = w_mod.shape[-1]
    return pl.pallas_call(
        _mod_kernel,
        grid=(DEPTH, n // tn),
        in_specs=[
            pl.BlockSpec((8, D_MODEL), lambda l, j: (0, 0)),
            pl.BlockSpec((None, D_MODEL, tn), lambda l, j: (l, 0, j)),
            pl.BlockSpec((None, 1, tn), lambda l, j: (l, 0, j)),
        ],
        out_specs=pl.BlockSpec((None, 8, tn), lambda l, j: (l, 0, j)),
        out_shape=jax.ShapeDtypeStruct((DEPTH, 8, n), F32),
        compiler_params=_cparams(("parallel", "parallel")),
        name="modulation",
    )(cvecs, w_mod, b_mod.reshape(DEPTH, 1, n))


def _norm_mod(x, g, sc, sh):
    ms = jnp.mean(x * x, axis=-1, keepdims=True)
    y = x * lax.rsqrt(ms + NORM_EPS) * g
    return y * (1.0 + sc) + sh


def _nm_mm_kernel(x_ref, g_ref, sc_ref, sh_ref, w_ref, *rest, shift_t):
    if shift_t:
        mu_ref, o_ref, h_scr = rest
    else:
        o_ref, h_scr = rest

    @pl.when(pl.program_id(1) == 0)
    def _():
        h_scr[...] = _norm_mod(x_ref[...], g_ref[...], sc_ref[...], sh_ref[...]).astype(BF16)

    acc = jnp.dot(h_scr[...], w_ref[...].astype(BF16), preferred_element_type=F32)
    if not shift_t:
        o_ref[...] = acc
        return
    tn = acc.shape[1]
    shifted_tiles = -(-AB_SHIFT_COLS // tn)

    @pl.when(pl.program_id(1) < shifted_tiles)
    def _():
        tm = acc.shape[0]
        row = lax.broadcasted_iota(jnp.int32, acc.shape, 0) & (shift_t - 1)
        prev = jnp.where(row == 0, 0.0, pltpu.roll(acc, 1, 0))
        nxt = jnp.where(row == shift_t - 1, 0.0, pltpu.roll(acc, tm - 1, 0))
        o_ref[...] = acc + mu_ref[0:1, :] * (prev - acc) + mu_ref[1:2, :] * (nxt - acc)

    @pl.when(pl.program_id(1) >= shifted_tiles)
    def _():
        o_ref[...] = acc


def nm_matmul(x, g, sc, sh, w, *, rows_per_mod, tn, mu=None, shift_t=0):
    m, d = x.shape
    n = w.shape[1]
    tpm = rows_per_mod // TM
    in_specs = [
        pl.BlockSpec((TM, d), lambda i, j: (i, 0)),
        pl.BlockSpec((1, d), lambda i, j: (0, 0)),
        pl.BlockSpec((None, 1, d), lambda i, j: (i // tpm, 0, 0)),
        pl.BlockSpec((None, 1, d), lambda i, j: (i // tpm, 0, 0)),
        pl.BlockSpec((d, tn), lambda i, j: (0, j)),
    ]
    args = [x, g.reshape(1, d), sc, sh, w]
    if shift_t:
        in_specs.append(pl.BlockSpec((2, tn), lambda i, j: (0, j)))
        args.append(mu)
    return pl.pallas_call(
        functools.partial(_nm_mm_kernel, shift_t=shift_t),
        grid=(m // TM, n // tn),
        in_specs=in_specs,
        out_specs=pl.BlockSpec((TM, tn), lambda i, j: (i, j)),
        out_shape=jax.ShapeDtypeStruct((m, n), F32),
        scratch_shapes=[pltpu.VMEM((TM, d), BF16)],
        compiler_params=_cparams(("parallel", "arbitrary")),
        name="norm_mod_matmul",
    )(*args)


def _rms_mm_kernel(x_ref, g_ref, w_ref, *out_refs, normalize, emit_norm):
    x = x_ref[...]
    if normalize:
        ms = jnp.mean(x * x, axis=-1, keepdims=True)
        x = x * lax.rsqrt(ms + NORM_EPS) * g_ref[...]
    if emit_norm:
        out_refs[1][...] = x
    out_refs[0][...] = jnp.dot(x.astype(BF16), w_ref[...].astype(BF16),
                               preferred_element_type=F32).astype(out_refs[0].dtype)


def rms_matmul(x, col_blk, k, g, w, *, normalize, emit_norm, out_dtype):
    m = x.shape[0]
    n = w.shape[1]
    tm = min(TM, m)
    out_shape = [jax.ShapeDtypeStruct((m, n), out_dtype)]
    out_specs = [pl.BlockSpec((tm, n), lambda i: (i, 0))]
    if emit_norm:
        out_shape.append(jax.ShapeDtypeStruct((m, k), F32))
        out_specs.append(pl.BlockSpec((tm, k), lambda i: (i, 0)))
    res = pl.pallas_call(
        functools.partial(_rms_mm_kernel, normalize=normalize, emit_norm=emit_norm),
        grid=(m // tm,),
        in_specs=[
            pl.BlockSpec((tm, k), lambda i: (i, col_blk)),
            pl.BlockSpec((1, k), lambda i: (0, 0)),
            pl.BlockSpec((k, n), lambda i: (0, 0)),
        ],
        out_specs=out_specs,
        out_shape=out_shape,
        compiler_params=_cparams(("parallel",)),
        name="rms_matmul",
    )(x, g.reshape(1, k), w)
    return res


def _res_mm_kernel(*refs, n_x):
    x_refs = refs[:n_x]
    w_refs = refs[n_x:2 * n_x]
    res_ref, gate_ref, o_ref = refs[2 * n_x:]
    acc = jnp.dot(x_refs[0][...], w_refs[0][...].astype(BF16), preferred_element_type=F32)
    for xr, wr in zip(x_refs[1:], w_refs[1:]):
        acc = acc + jnp.dot(xr[...], wr[...].astype(BF16), preferred_element_type=F32)
    o_ref[...] = res_ref[...] + gate_ref[...] * acc


def res_matmul(xs, w, res, gate, *, rows_per_mod, tn=512):
    m, n = res.shape
    tpm = rows_per_mod // TM
    in_specs, k0 = [], 0
    for xx in xs:
        in_specs.append(pl.BlockSpec((TM, xx.shape[1]), lambda i, j: (i, 0)))
    for xx in xs:
        kx = xx.shape[1]
        blk = k0 // kx
        in_specs.append(pl.BlockSpec((kx, tn), lambda i, j, blk=blk: (blk, j)))
        k0 += kx
    in_specs.append(pl.BlockSpec((TM, tn), lambda i, j: (i, j)))
    in_specs.append(pl.BlockSpec((None, 1, tn), lambda i, j: (i // tpm, 0, j)))
    return pl.pallas_call(
        functools.partial(_res_mm_kernel, n_x=len(xs)),
        grid=(m // TM, n // tn),
        in_specs=in_specs,
        out_specs=pl.BlockSpec((TM, tn), lambda i, j: (i, j)),
        out_shape=jax.ShapeDtypeStruct((m, n), F32),
        compiler_params=_cparams(("parallel", "parallel")),
        name="residual_matmul",
    )(*xs, *([w] * len(xs)), res, gate)


def _final_norm_kernel(x_ref, g_ref, o_ref):
    x = x_ref[...]
    ms = jnp.mean(x * x, axis=-1, keepdims=True)
    o_ref[...] = x * lax.rsqrt(ms + NORM_EPS) * g_ref[...]


def final_norm(x, g):
    m, d = x.shape
    return pl.pallas_call(
        _final_norm_kernel,
        grid=(m // TM,),
        in_specs=[pl.BlockSpec((TM, d), lambda i: (i, 0)), pl.BlockSpec((1, d), lambda i: (0, 0))],
        out_specs=pl.BlockSpec((TM, d), lambda i: (i, 0)),
        out_shape=jax.ShapeDtypeStruct((m, d), F32),
        compiler_params=_cparams(("parallel",)),
        name="final_norm",
    )(x, g.reshape(1, d))


def _head_ones(width):
    r = lax.broadcasted_iota(jnp.int32, (width, width), 0) // A_HEAD
    c = lax.broadcasted_iota(jnp.int32, (width, width), 1) // A_HEAD
    return (r == c).astype(BF16)


def _head_eye(width):
    r = lax.broadcasted_iota(jnp.int32, (A_HEAD, width), 0)
    c = lax.broadcasted_iota(jnp.int32, (A_HEAD, width), 1) & (A_HEAD - 1)
    return r == c


RWKV_G = 4
RWKV_HPB = 2
RWKV_TB = 256
RWKV_W = RWKV_HPB * LANES
RWKV_MM_GROUPS = 2
RWKV_UNROLL = 8


def _head_sum(z, ones16):
    ones128 = ones16[:LANES, :LANES]
    parts = [_dot_01(z[:, j * LANES:(j + 1) * LANES], ones128, m01_left=False)
             for j in range(z.shape[1] // LANES)]
    return parts[0] if len(parts) == 1 else jnp.concatenate(parts, axis=1)


def _rwkv_scan_kernel(*refs, has_s0, has_acc):
    views = (refs[0:5], refs[5:10])
    w2_ref, w0_ref, a2_ref, a0_ref, kk_ref, ka_ref = refs[10:16]
    rest = refs[16:]
    if has_s0:
        s0_ref, rest = rest[0], rest[1:]
    if has_acc:
        rest = rest[1:]
    yf_ref, yb_ref, sfin_ref, a_s, w_s, b_s, k_s, st_s = rest
    y_refs = (yf_ref, yb_ref)
    n_g, tb_len, width = views[0][0].shape
    n_heads = width // A_HEAD
    tb = pl.program_id(2)
    n_tb = pl.num_programs(2)
    ones16 = _head_ones(width)
    eye = _head_eye(width)
    eye_f32 = eye.astype(F32)
    eye_mask16 = eye_f32.astype(BF16) > 0.5
    zero16 = jnp.zeros((A_HEAD, width), BF16)

    for d in range(2):
        _, k_ref, _, xw_ref, xa_ref = views[d]
        for g in range(n_g):
            kraw = k_ref[g]
            kk = kraw * kk_ref[...]
            kk = kk * lax.rsqrt(jnp.maximum(_head_sum(kk * kk, ones16), 1e-24))
            a_s[d, g] = -kk
            xw = xw_ref[g][:, d * W_LORA:(d + 1) * W_LORA]
            xa = xa_ref[g][:, d * ICL_LORA:(d + 1) * ICL_LORA]
            wl = w0_ref[d] + _dot_x3(jnp.tanh(xw), w2_ref[d])
            w_s[d, g] = jnp.exp(-jnp.exp(-_softplus(-wl) - 0.5))
            a = _sigmoid(a0_ref[d] + _dot_x3(xa, a2_ref[d]))
            k_s[d, g] = kraw * (1.0 + (a - 1.0) * ka_ref[...])
            b_s[d, g] = kk * a

    chains = [(d, g) for d in range(2) for g in range(n_g)]

    @pl.when(tb == 0)
    def _():
        for d, g in chains:
            if has_s0:
                st_s[d, g] = jnp.concatenate([s0_ref[g, d, h] for h in range(n_heads)], axis=1)
            else:
                st_s[d, g] = jnp.zeros((A_HEAD, width), F32)

    group_len = len(chains) // RWKV_MM_GROUPS
    groups = [chains[i * group_len:(i + 1) * group_len] for i in range(RWKV_MM_GROUPS)]

    def readout(yb):
        m = yb * eye_f32
        m8 = m[0:8]
        for i in range(1, A_HEAD // 8):
            m8 = m8 + m[8 * i:8 * i + 8]
        return jnp.sum(m8, axis=0, keepdims=True)

    def step(rows, prev_rows):
        for grp in groups:
            lhs = []
            for d, g in grp:
                r_ref, _, v_ref, _, _ = views[d]
                s = st_s[d, g].astype(BF16)
                lhs.append(s * a_s[d, g, rows[d], :].astype(BF16))
                lhs.append(jnp.where(eye, v_ref[g, rows[d], :], 0.0).astype(BF16))
                lhs.append(s * r_ref[g, prev_rows[d], :].astype(BF16))
            res = jnp.dot(jnp.concatenate(lhs, axis=0), ones16, preferred_element_type=F32)
            for i, (d, g) in enumerate(grp):
                sa, vb, yb = (res[(3 * i + q) * A_HEAD:(3 * i + q + 1) * A_HEAD] for q in range(3))
                y_refs[d][g, prev_rows[d], :] = readout(yb)
                row = rows[d]
                st_s[d, g] = st_s[d, g] * w_s[d, g, row, :] + sa * b_s[d, g, row, :] + vb * k_s[d, g, row, :]

    def steps(tile, carry):
        n = RWKV_UNROLL
        base_f = pl.multiple_of(tile * n, n)
        base_b = pl.multiple_of(tb_len - n - tile * n, n)
        prev_f = pl.multiple_of(jnp.maximum(tile * n - n, 0), n)
        prev_b = pl.multiple_of(jnp.minimum(tb_len - tile * n, tb_len - n), n)
        for i in range(n):
            rows = [pl.ds(base_f + i, 1), pl.ds(base_b + (n - 1 - i), 1)]
            if i == 0:
                prev_rows = [pl.ds(prev_f + (n - 1), 1), pl.ds(prev_b, 1)]
            else:
                prev_rows = [pl.ds(base_f + (i - 1), 1), pl.ds(base_b + (n - i), 1)]
            step(rows, prev_rows)
        return carry

    lax.fori_loop(0, tb_len // RWKV_UNROLL, steps, 0)
    last_rows = (pl.ds(tb_len - 1, 1), pl.ds(0, 1))
    lhs = [(st_s[d, g] * views[d][0][g, last_rows[d], :]).astype(BF16) for d, g in chains]
    res = jnp.dot(jnp.concatenate(lhs, axis=0), ones16, preferred_element_type=F32)
    for i, (d, g) in enumerate(chains):
        y_refs[d][g, last_rows[d], :] = readout(res[i * A_HEAD:(i + 1) * A_HEAD])

    @pl.when(tb == n_tb - 1)
    def _():
        for d, g in chains:
            s = st_s[d, g]
            for h in range(n_heads):
                sfin_ref[g, d, h] = s[:, h * A_HEAD:(h + 1) * A_HEAD]


def rwkv_scan(u, p, s0, layer_idx, states_acc=None, n_state_layers=1):
    bsz, t_len, _ = u.shape
    state_layer = layer_idx % n_state_layers
    n_tb = t_len // RWKV_TB
    n_hpg = A_WIDTH // RWKV_W
    fwd = lambda tb: tb
    bwd = lambda tb: n_tb - 1 - tb

    def view(tmap):
        def col(first_blk, width=RWKV_W, per_group=True):
            return pl.BlockSpec((RWKV_G, RWKV_TB, width),
                                lambda bi, hg, tb: (bi, tmap(tb), first_blk + (hg if per_group else 0)))
        return [col(0), col(n_hpg), col(2 * n_hpg),
                col(AB_XW_BLK, LANES, False), col(AB_XA_BLK, LANES, False)]

    in_specs = view(fwd) + view(bwd) + [
        pl.BlockSpec((2, W_LORA, RWKV_W), lambda bi, hg, tb: (0, 0, hg)),
        pl.BlockSpec((2, 1, RWKV_W), lambda bi, hg, tb: (0, 0, hg)),
        pl.BlockSpec((2, ICL_LORA, RWKV_W), lambda bi, hg, tb: (0, 0, hg)),
        pl.BlockSpec((2, 1, RWKV_W), lambda bi, hg, tb: (0, 0, hg)),
        pl.BlockSpec((1, RWKV_W), lambda bi, hg, tb: (0, hg)),
        pl.BlockSpec((1, RWKV_W), lambda bi, hg, tb: (0, hg)),
    ]
    args = [u] * 10 + [p['w2'], p['w0'].reshape(2, 1, A_WIDTH), p['a2'], p['a0'].reshape(2, 1, A_WIDTH),
                       p['k_k'].reshape(1, A_WIDTH), p['k_a'].reshape(1, A_WIDTH)]
    heads_blk = 2 * RWKV_HPB
    if s0 is not None:
        in_specs.append(pl.BlockSpec((RWKV_G, None, 2, heads_blk, A_HEAD, A_HEAD),
                                     lambda bi, hg, tb: (bi, layer_idx, 0, hg, 0, 0)))
        args.append(s0)
    aliases = {}
    if states_acc is not None:
        in_specs.append(pl.BlockSpec(memory_space=pl.ANY))
        args.append(states_acc)
        aliases = {len(args) - 1: 2}
    blk = (2, RWKV_G, RWKV_TB, RWKV_W)
    return pl.pallas_call(
        functools.partial(_rwkv_scan_kernel, has_s0=s0 is not None, has_acc=states_acc is not None),
        grid=(bsz // RWKV_G, n_hpg, n_tb),
        in_specs=in_specs,
        out_specs=[
            pl.BlockSpec((RWKV_G, RWKV_TB, RWKV_W), lambda bi, hg, tb: (bi, fwd(tb), hg)),
            pl.BlockSpec((RWKV_G, RWKV_TB, RWKV_W), lambda bi, hg, tb: (bi, bwd(tb), hg)),
            pl.BlockSpec((RWKV_G, None, 2, heads_blk, A_HEAD, A_HEAD),
                         lambda bi, hg, tb: (bi, state_layer, 0, hg, 0, 0)),
        ],
        out_shape=[
            jax.ShapeDtypeStruct((bsz, t_len, A_WIDTH), F32),
            jax.ShapeDtypeStruct((bsz, t_len, A_WIDTH), F32),
            jax.ShapeDtypeStruct((bsz, n_state_layers, 2, A_HEADS, A_HEAD, A_HEAD), F32),
        ],
        input_output_aliases=aliases,
        scratch_shapes=[
            pltpu.VMEM(blk, F32),
            pltpu.VMEM(blk, F32),
            pltpu.VMEM(blk, F32),
            pltpu.VMEM(blk, F32),
            pltpu.VMEM((2, RWKV_G, A_HEAD, RWKV_W), F32),
        ],
        compiler_params=_cparams(("parallel", "parallel", "arbitrary")),
        name="rwkv7_scan",
    )(*args)


def _rwkv_post_kernel(yf_ref, yb_ref, r_ref, k_ref, v_ref, xa_ref, xg_ref,
                      a2_ref, a0_ref, ka_ref, rk_ref, gng_ref, gnb_ref, g2_ref, o_ref):
    ones_bd = _head_ones(LANES)
    inv_n = 1.0 / A_HEAD
    y = yf_ref[...] + yb_ref[...]
    mu = _head_sum(y, ones_bd) * inv_n
    yc = y - mu
    var = _head_sum(yc * yc, ones_bd) * inv_n
    yn = yc * lax.rsqrt(var + GN_EPS) * gng_ref[...] + gnb_ref[...]
    r = r_ref[...]
    kraw = k_ref[...]
    rk_sum = jnp.zeros_like(r)
    for d in range(2):
        xa = xa_ref[:, d * ICL_LORA:(d + 1) * ICL_LORA]
        a = _sigmoid(a0_ref[d] + _dot_x3(xa, a2_ref[d]))
        rk_sum = rk_sum + r * (kraw * (1.0 + (a - 1.0) * ka_ref[...])) * rk_ref[...]
    bonus = _head_sum(rk_sum, ones_bd) * v_ref[...]
    gate = jnp.dot(_sigmoid(xg_ref[...]).astype(BF16), g2_ref[...].astype(BF16), preferred_element_type=F32)
    o_ref[...] = ((yn + bonus) * gate).astype(o_ref.dtype)


def rwkv_post(yf, yb, u2, p):
    m = yf.shape[0]
    tm = 256
    full = lambda shape: pl.BlockSpec(shape, lambda i: (0,) * len(shape))
    row = lambda width, blk: pl.BlockSpec((tm, width), lambda i: (i, blk))
    return pl.pallas_call(
        _rwkv_post_kernel,
        grid=(m // tm,),
        in_specs=[
            row(A_WIDTH, 0), row(A_WIDTH, 0),
            row(A_WIDTH, 0), row(A_WIDTH, 1), row(A_WIDTH, 2),
            row(LANES, AB_XA_BLK), row(G_LORA_PAD, AB_XG_BLK),
            full((2, ICL_LORA, A_WIDTH)), full((2, 1, A_WIDTH)),
            full((1, A_WIDTH)), full((1, A_WIDTH)), full((1, A_WIDTH)), full((1, A_WIDTH)),
            full((G_LORA_PAD, A_WIDTH)),
        ],
        out_specs=row(A_WIDTH, 0),
        out_shape=jax.ShapeDtypeStruct((m, A_WIDTH), BF16),
        compiler_params=_cparams(("parallel",)),
        name="rwkv7_post",
    )(yf, yb, u2, u2, u2, u2, u2, p['a2'], p['a0'].reshape(2, 1, A_WIDTH), p['k_a'].reshape(1, A_WIDTH),
      p['r_k'].reshape(1, A_WIDTH), p['gn_g'].reshape(1, A_WIDTH), p['gn_b'].reshape(1, A_WIDTH), p['g2'])


def rwkv_mix(u, p, s0, layer_idx, states_acc=None, n_state_layers=1):
    bsz, t_len, _ = u.shape
    m = bsz * t_len
    yf, yb, sfin = rwkv_scan(u, p, s0, layer_idx, states_acc, n_state_layers)
    out = rwkv_post(yf.reshape(m, A_WIDTH), yb.reshape(m, A_WIDTH), u.reshape(m, AB_PAD_COLS), p)
    return out, sfin


def _mla_kernel(qn_ref, qr_ref, kn_ref, kr_ref, v_ref, o_ref):
    scale = (NOPE + ROPE_D) ** -0.5
    krb = kr_ref[...].astype(BF16)
    nt = (((1,), (1,)), ((), ()))
    for h in range(B_HEADS):
        qn = qn_ref[:, h * NOPE:(h + 1) * NOPE].astype(BF16)
        qr = qr_ref[:, h * ROPE_D:(h + 1) * ROPE_D].astype(BF16)
        s = lax.dot_general(qn, kn_ref[:, h * NOPE:(h + 1) * NOPE], nt, preferred_element_type=F32)
        s = s + lax.dot_general(qr, krb, nt, preferred_element_type=F32)
        s = s * scale
        m = jnp.max(s, axis=-1, keepdims=True)
        e = jnp.exp(s - m)
        p = e / jnp.sum(e, axis=-1, keepdims=True)
        o = jnp.dot(p.astype(BF16), v_ref[:, h * V_HEAD:(h + 1) * V_HEAD], preferred_element_type=F32)
        o_ref[:, h * V_HEAD:(h + 1) * V_HEAD] = o.astype(o_ref.dtype)


def mla_attention(q, kv, kr, *, tq):
    bsz, t_len, _ = q.shape
    s_len = kv.shape[1]
    hn = B_HEADS * NOPE
    return pl.pallas_call(
        _mla_kernel,
        grid=(bsz, t_len // tq),
        in_specs=[
            pl.BlockSpec((None, tq, hn), lambda b, i: (b, i, 0)),
            pl.BlockSpec((None, tq, B_HEADS * ROPE_D), lambda b, i: (b, i, hn // (B_HEADS * ROPE_D))),
            pl.BlockSpec((None, s_len, hn), lambda b, i: (b, 0, 0)),
            pl.BlockSpec((None, s_len, ROPE_D), lambda b, i: (b, 0, 0)),
            pl.BlockSpec((None, s_len, hn), lambda b, i: (b, 0, 1)),
        ],
        out_specs=pl.BlockSpec((None, tq, hn), lambda b, i: (b, i, 0)),
        out_shape=jax.ShapeDtypeStruct((bsz, t_len, hn), BF16),
        compiler_params=_cparams(("parallel", "parallel")),
        name="mla_attention",
    )(q, q, kv, kr, kv)


GLA_SUPER = 4


def _gla_kernel(q_ref, k_ref, v_ref, og_ref, xg_ref, gw_ref, gb_ref, gn_ref, *rest, has_s0, has_acc):
    if has_s0:
        s0_ref, rest = rest[0], rest[1:]
    if has_acc:
        rest = rest[1:]
    o_ref, sfin_ref = rest[:2]
    lg_s, y_s, st_s, qt_s, u_s, dec_s = (rest[2 + 2 * i:4 + 2 * i] for i in range(6))
    t_len = q_ref.shape[0]
    n_chunks = t_len // CHUNK
    scale = C_DK ** -0.5
    nt = (((1,), (1,)), ((), ()))
    xg = xg_ref[...]
    span = GLA_SUPER * CHUNK
    ri = lax.broadcasted_iota(jnp.int32, (span, span), 0)
    ci = lax.broadcasted_iota(jnp.int32, (span, span), 1)
    same_chunk = (ri // CHUNK) == (ci // CHUNK)
    keeps = (same_chunk & (ri >= ci), same_chunk & (ri <= ci))
    tris = tuple(kp.astype(BF16) for kp in keeps)
    for d in range(2):
        z = _dot_x3(xg[:, d * GATE_LORA:(d + 1) * GATE_LORA], gw_ref[d]) + gb_ref[d]
        lg_s[d][...] = -_softplus(-z) * (1.0 / GATE_TAU)
        if has_s0:
            st_s[d][...] = s0_ref[d].T
        else:
            st_s[d][...] = jnp.zeros(st_s[d].shape, F32)

    def chunks_local(sc, carry):
        rows = pl.ds(pl.multiple_of(sc * span, span), span)
        q = q_ref[rows, :] * scale
        k = k_ref[rows, :]
        v = v_ref[rows, :]
        vb = v.astype(BF16)
        for d in range(2):
            b = _dot_01(tris[d], lg_s[d][rows, :])
            edge = CHUNK - 1 if d == 0 else 0
            bls = [b[cc * CHUNK + edge:cc * CHUNK + edge + 1, :] for cc in range(GLA_SUPER)]
            bl = jnp.concatenate([jnp.broadcast_to(x, (CHUNK, C_DK)) for x in bls], axis=0)
            qt = (q * jnp.exp(b)).astype(BF16)
            kt = (k * jnp.exp(-b)).astype(BF16)
            kd = (k * jnp.exp(bl - b)).astype(BF16)
            att = jnp.where(keeps[d], lax.dot_general(qt, kt, nt, preferred_element_type=F32), 0.0)
            y_s[d][rows, :] = jnp.dot(att.astype(BF16), vb, preferred_element_type=F32)
            qt_s[d][rows, :] = qt
            for cc in range(GLA_SUPER):
                part = slice(cc * CHUNK, (cc + 1) * CHUNK)
                u_s[d][sc * GLA_SUPER + cc] = jnp.dot(v[part].T.astype(BF16), kd[part], preferred_element_type=F32)
                dec_s[d][sc * GLA_SUPER + cc] = jnp.broadcast_to(jnp.exp(bls[cc]), (8, C_DK))
        return carry

    lax.fori_loop(0, n_chunks // GLA_SUPER, chunks_local, 0)

    def chunk_state(i, carry):
        for d in range(2):
            c = i if d == 0 else n_chunks - 1 - i
            rows = pl.ds(pl.multiple_of(c * CHUNK, CHUNK), CHUNK)
            st = st_s[d][...]
            y_s[d][rows, :] = y_s[d][rows, :] + lax.dot_general(qt_s[d][rows, :], st.astype(BF16), nt,
                                                                preferred_element_type=F32)
            st_s[d][...] = st * dec_s[d][c, 0:1, :] + u_s[d][c]
        return carry

    lax.fori_loop(0, n_chunks, chunk_state, 0)
    for d in range(2):
        sfin_ref[d] = st_s[d][...].T
    y = y_s[0][...] + y_s[1][...]
    yn = y * lax.rsqrt(jnp.mean(y * y, axis=-1, keepdims=True) + NORM_EPS) * gn_ref[...]
    o_ref[...] = (yn * _silu(og_ref[...])).astype(o_ref.dtype)


def gla_mix(u, p, s0, layer_idx, states_acc=None, n_state_layers=1):
    bsz, t_len, _ = u.shape
    state_layer = layer_idx % n_state_layers
    in_specs = [
        pl.BlockSpec((None, t_len, C_DK), lambda b, h: (b, 0, h)),
        pl.BlockSpec((None, t_len, C_DK), lambda b, h: (b, 0, C_HEADS + h)),
        pl.BlockSpec((None, t_len, C_DV), lambda b, h: (b, 0, C_HEADS + h)),
        pl.BlockSpec((None, t_len, C_DV), lambda b, h: (b, 0, 2 * C_HEADS + h)),
        pl.BlockSpec((None, t_len, LANES), lambda b, h: (b, 0, C_XG_BLK)),
        pl.BlockSpec((2, GATE_LORA, C_DK), lambda b, h: (0, 0, h)),
        pl.BlockSpec((2, 1, C_DK), lambda b, h: (0, 0, h)),
        pl.BlockSpec((1, C_DV), lambda b, h: (0, 0)),
    ]
    args = [u, u, u, u, u, p['gate_w2'], p['gate_b'].reshape(2, 1, C_QK), p['gn'].reshape(1, C_DV)]
    if s0 is not None:
        in_specs.append(pl.BlockSpec((None, None, 2, None, C_DK, C_DV), lambda b, h: (b, layer_idx, 0, h, 0, 0)))
        args.append(s0)
    aliases = {}
    if states_acc is not None:
        in_specs.append(pl.BlockSpec(memory_space=pl.ANY))
        args.append(states_acc)
        aliases = {len(args) - 1: 1}
    return pl.pallas_call(
        functools.partial(_gla_kernel, has_s0=s0 is not None, has_acc=states_acc is not None),
        grid=(bsz, C_HEADS),
        in_specs=in_specs,
        out_specs=[
            pl.BlockSpec((None, t_len, C_DV), lambda b, h: (b, 0, h)),
            pl.BlockSpec((None, None, 2, None, C_DK, C_DV), lambda b, h: (b, state_layer, 0, h, 0, 0)),
        ],
        out_shape=[
            jax.ShapeDtypeStruct((bsz, t_len, C_V), BF16),
            jax.ShapeDtypeStruct((bsz, n_state_layers, 2, C_HEADS, C_DK, C_DV), F32),
        ],
        input_output_aliases=aliases,
        scratch_shapes=[
            pltpu.VMEM((t_len, C_DK), F32),
            pltpu.VMEM((t_len, C_DK), F32),
            pltpu.VMEM((t_len, C_DV), F32),
            pltpu.VMEM((t_len, C_DV), F32),
            pltpu.VMEM((C_DV, C_DK), F32),
            pltpu.VMEM((C_DV, C_DK), F32),
            pltpu.VMEM((t_len, C_DK), BF16),
            pltpu.VMEM((t_len, C_DK), BF16),
            pltpu.VMEM((t_len // CHUNK, C_DV, C_DK), F32),
            pltpu.VMEM((t_len // CHUNK, C_DV, C_DK), F32),
            pltpu.VMEM((t_len // CHUNK, 8, C_DK), F32),
            pltpu.VMEM((t_len // CHUNK, 8, C_DK), F32),
        ],
        compiler_params=_cparams(("parallel", "parallel")),
        name="gla_mix",
    )(*args)


def _moe_col_tile(t_len, d):
    return d if t_len <= 256 else 512


def _dispatch_kernel(x_ref, g_ref, sc_ref, sh_ref, wr_ref, xe_ref, rank_ref, aff_ref, hb_s, onehot_s, *, cap):
    t_len = x_ref.shape[0]
    j = pl.program_id(1)

    @pl.when(j == 0)
    def _():
        h = _norm_mod(x_ref[...], g_ref[...], sc_ref[...], sh_ref[...])
        tn = hb_s.shape[2]
        for n in range(hb_s.shape[0]):
            hb_s[n] = h[:, n * tn:(n + 1) * tn].astype(BF16)
        logits = _dot_x3(h, wr_ref[...], stack=False)
        lane = lax.broadcasted_iota(jnp.int32, logits.shape, 1)
        logits = jnp.where(lane < N_EXPERTS, logits, -jnp.inf)
        m = jnp.max(logits, axis=-1, keepdims=True)
        e = jnp.exp(logits - m)
        aff = e / jnp.sum(e, axis=-1, keepdims=True)
        aff_ref[...] = aff
        aff_t = aff.T
        sub_i = lax.broadcasted_iota(jnp.int32, (t_len, t_len), 0)
        lane_i = lax.broadcasted_iota(jnp.int32, (t_len, t_len), 1)
        slot = lax.broadcasted_iota(jnp.int32, (cap, t_len), 0).astype(F32)
        expert_row = lax.broadcasted_iota(jnp.int32, (N_EXPERTS, t_len), 0)
        ranks = jnp.zeros((N_EXPERTS, t_len), F32)
        for ex in range(N_EXPERTS):
            col = aff[:, ex:ex + 1]
            row = aff_t[ex:ex + 1, :]
            ahead = (col > row) | ((col == row) & (sub_i < lane_i))
            rank_row = jnp.sum(ahead.astype(F32), axis=0, keepdims=True)
            onehot_s[ex * cap:(ex + 1) * cap, :] = (rank_row == slot).astype(BF16)
            ranks = jnp.where(expert_row == ex, rank_row, ranks)
        pad = jnp.zeros((LANES - N_EXPERTS, t_len), F32)
        rank_ref[...] = jnp.concatenate([ranks, pad], axis=0).T

    rows = jnp.dot(onehot_s[...], hb_s[j], preferred_element_type=F32).astype(BF16)
    for ex in range(N_EXPERTS):
        xe_ref[ex] = rows[ex * cap:(ex + 1) * cap]


def moe_dispatch(x, g, sc, sh, w_router_pad, cap):
    bsz, t_len, d = x.shape
    per_req = sc.shape[0] > 1
    tn = _moe_col_tile(t_len, d)
    mod_spec = pl.BlockSpec((None, 1, d), (lambda b, j: (b, 0, 0)) if per_req else (lambda b, j: (0, 0, 0)))
    return pl.pallas_call(
        functools.partial(_dispatch_kernel, cap=cap),
        grid=(bsz, d // tn),
        in_specs=[
            pl.BlockSpec((None, t_len, d), lambda b, j: (b, 0, 0)),
            pl.BlockSpec((1, d), lambda b, j: (0, 0)),
            mod_spec, mod_spec,
            pl.BlockSpec((d, LANES), lambda b, j: (0, 0)),
        ],
        out_specs=[
            pl.BlockSpec((N_EXPERTS, cap, tn), lambda b, j: (0, b, j)),
            pl.BlockSpec((None, t_len, LANES), lambda b, j: (b, 0, 0)),
            pl.BlockSpec((None, t_len, LANES), lambda b, j: (b, 0, 0)),
        ],
        out_shape=[
            jax.ShapeDtypeStruct((N_EXPERTS, bsz * cap, d), BF16),
            jax.ShapeDtypeStruct((bsz, t_len, LANES), F32),
            jax.ShapeDtypeStruct((bsz, t_len, LANES), F32),
        ],
        scratch_shapes=[
            pltpu.VMEM((d // tn, t_len, tn), BF16),
            pltpu.VMEM((N_EXPERTS * cap, t_len), BF16),
        ],
        compiler_params=_cparams(("parallel", "arbitrary")),
        name="moe_dispatch",
    )(x, g.reshape(1, d), sc, sh, w_router_pad)


EXPERT_TF = 256
EXPERT_TN = 256
N_UP_STEPS = D_EXPERT // EXPERT_TF
N_DOWN_STEPS = D_MODEL // EXPERT_TN


def _experts_kernel(xc_ref, xl_ref, wg_ref, wu_ref, wd_ref, yc_ref, yl_ref, hid_s):
    s = pl.program_id(1)
    mc = xc_ref.shape[0]

    @pl.when(s < N_UP_STEPS)
    def _():
        wg = wg_ref[...].astype(BF16)
        wu = wu_ref[...].astype(BF16)
        cols = pl.ds(pl.multiple_of(s * EXPERT_TF, EXPERT_TF), EXPERT_TF)
        for x_ref, r0 in ((xc_ref, 0), (xl_ref, mc)):
            x = x_ref[...]
            hg = jnp.dot(x, wg, preferred_element_type=F32)
            hu = jnp.dot(x, wu, preferred_element_type=F32)
            hid_s[r0:r0 + x.shape[0], cols] = (_silu(hg) * hu).astype(BF16)

    @pl.when(s >= N_UP_STEPS)
    def _():
        wd = wd_ref[...].astype(BF16)
        yc_ref[...] = jnp.dot(hid_s[0:mc, :], wd, preferred_element_type=F32).astype(yc_ref.dtype)
        yl_ref[...] = jnp.dot(hid_s[mc:, :], wd, preferred_element_type=F32).astype(yl_ref.dtype)


def moe_experts(xc, xl, w_gate, w_up, w_down, layer):
    mc, ml = xc.shape[1], xl.shape[1]
    up_idx = lambda e, s: (layer, e, 0, jnp.minimum(s, N_UP_STEPS - 1))
    down_idx = lambda e, s: (layer, e, 0, jnp.maximum(s - N_UP_STEPS, 0))
    out_idx = lambda e, s: (e, 0, jnp.maximum(s - N_UP_STEPS, 0))
    return pl.pallas_call(
        _experts_kernel,
        grid=(N_EXPERTS, N_UP_STEPS + N_DOWN_STEPS),
        in_specs=[
            pl.BlockSpec((None, mc, D_MODEL), lambda e, s: (e, 0, 0)),
            pl.BlockSpec((None, ml, D_MODEL), lambda e, s: (e, 0, 0)),
            pl.BlockSpec((None, None, D_MODEL, EXPERT_TF), up_idx),
            pl.BlockSpec((None, None, D_MODEL, EXPERT_TF), up_idx),
            pl.BlockSpec((None, None, D_EXPERT, EXPERT_TN), down_idx),
        ],
        out_specs=[
            pl.BlockSpec((None, mc, EXPERT_TN), out_idx),
            pl.BlockSpec((None, ml, EXPERT_TN), out_idx),
        ],
        out_shape=[
            jax.ShapeDtypeStruct((N_EXPERTS, mc, D_MODEL), BF16),
            jax.ShapeDtypeStruct((N_EXPERTS, ml, D_MODEL), BF16),
        ],
        scratch_shapes=[pltpu.VMEM((mc + ml, D_EXPERT), BF16)],
        compiler_params=_cparams(("parallel", "arbitrary")),
        name="moe_experts",
    )(xc, xl, w_gate, w_up, w_down)


def _combine_kernel(x_ref, gate_ref, ye_ref, rank_ref, aff_ref, o_ref, hi_s, lo_s, *, cap):
    t_len = x_ref.shape[0]

    @pl.when(pl.program_id(1) == 0)
    def _():
        per_group = LANES // cap
        lane = lax.broadcasted_iota(jnp.int32, (t_len, LANES), 1)
        slot = (lane % cap).astype(F32)
        for grp in range(N_EXPERTS // per_group):
            w = jnp.zeros((t_len, LANES), F32)
            for i in range(per_group):
                ex = grp * per_group + i
                hit = rank_ref[:, ex:ex + 1] == slot
                if per_group > 1:
                    hit = hit & (lane // cap == i)
                w = jnp.where(hit, aff_ref[:, ex:ex + 1], w)
            hi = w.astype(BF16)
            hi_s[:, grp * LANES:(grp + 1) * LANES] = hi
            lo_s[:, grp * LANES:(grp + 1) * LANES] = (w - hi.astype(F32)).astype(BF16)

    ye = ye_ref[...].reshape(N_EXPERTS * cap, ye_ref.shape[2])
    acc = jnp.dot(hi_s[...], ye, preferred_element_type=F32) + jnp.dot(lo_s[...], ye, preferred_element_type=F32)
    o_ref[...] = x_ref[...] + gate_ref[...] * acc


def moe_combine(x, gate, ye, rank, aff, cap):
    bsz, t_len, d = x.shape
    per_req = gate.shape[0] > 1
    tn = _moe_col_tile(t_len, d)
    return pl.pallas_call(
        functools.partial(_combine_kernel, cap=cap),
        grid=(bsz, d // tn),
        in_specs=[
            pl.BlockSpec((None, t_len, tn), lambda b, j: (b, 0, j)),
            pl.BlockSpec((None, 1, tn), (lambda b, j: (b, 0, j)) if per_req else (lambda b, j: (0, 0, j))),
            pl.BlockSpec((N_EXPERTS, cap, tn), lambda b, j: (0, b, j)),
            pl.BlockSpec((None, t_len, LANES), lambda b, j: (b, 0, 0)),
            pl.BlockSpec((None, t_len, LANES), lambda b, j: (b, 0, 0)),
        ],
        out_specs=pl.BlockSpec((None, t_len, tn), lambda b, j: (b, 0, j)),
        out_shape=jax.ShapeDtypeStruct((bsz, t_len, d), F32),
        scratch_shapes=[pltpu.VMEM((t_len, N_EXPERTS * cap), BF16), pltpu.VMEM((t_len, N_EXPERTS * cap), BF16)],
        compiler_params=_cparams(("parallel", "arbitrary")),
        name="moe_combine",
    )(x, gate, ye, rank, aff)


def _pad_cols(w, n):
    return jnp.pad(w, ((0, 0),) * (w.ndim - 1) + ((0, n - w.shape[-1]),))


def _ab_layout(w):
    rkv = w[..., :3 * A_WIDTH]
    o = 3 * A_WIDTH
    xg = _pad_cols(w[..., o:o + G_LORA], G_LORA_PAD)
    o += G_LORA
    xw = w[..., o:o + 2 * W_LORA]
    o += 2 * W_LORA
    xa = w[..., o:o + 2 * ICL_LORA]
    o += 2 * ICL_LORA
    rest = w[..., o:]
    return _pad_cols(jnp.concatenate([rkv, xg, xw, xa, rest], axis=-1), AB_PAD_COLS)


def _c_layout(w):
    qkv = w[:, :2 * C_QK + C_V]
    o = 2 * C_QK + C_V
    xg = _pad_cols(w[:, o:o + 2 * GATE_LORA], LANES)
    og = w[:, o + 2 * GATE_LORA:]
    return _pad_cols(jnp.concatenate([qkv, og, xg], axis=-1), C_PAD_COLS)


def _axial_rope_tables(t):
    rows = t // GRID_W
    row = jnp.repeat(jnp.arange(rows, dtype=F32), GRID_W)
    col = jnp.tile(jnp.arange(GRID_W, dtype=F32), rows)
    half = ROPE_D // 2
    inv = 1.0 / (ROPE_BASE ** (jnp.arange(0, half, 2, dtype=F32) / half))
    ang_r = row[:, None] * inv[None, :]
    ang_c = col[:, None] * inv[None, :]
    return (jnp.cos(ang_r), jnp.sin(ang_r), jnp.cos(ang_c), jnp.sin(ang_c))


def _rotate(x, cos, sin):
    m = x.shape[-1] // 2
    x1, x2 = x[..., :m], x[..., m:]
    c = cos[None, :, None, :]
    s = sin[None, :, None, :]
    return jnp.concatenate([x1 * c - x2 * s, x1 * s + x2 * c], axis=-1)


def _apply_axial_rope(x, tabs):
    cr, sr, cc, sc = tabs
    half = ROPE_D // 2
    return jnp.concatenate([_rotate(x[..., :half], cr, sr), _rotate(x[..., half:], cc, sc)], axis=-1)


def _mixer_layer(x2, l, mod_l, P, caches, rope_tabs, bsz, t_len, outs):
    m = bsz * t_len
    is_ctx = caches is None
    rows_per_mod = m if is_ctx else t_len
    sh1, sc1, g1 = mod_l[0], mod_l[1], mod_l[2]
    i = l // 2
    if l % 2 == 0:
        u = nm_matmul(x2, P['g_norm_mix'][l], sc1, sh1, P['ab_w_in_pad'][i], rows_per_mod=rows_per_mod,
                      tn=512, mu=P['rwkv_mu_pad'][i], shift_t=t_len)
        p = dict(w2=P['rwkv_w2'][i], w0=P['rwkv_w0'][i], a2=P['rwkv_a2'][i], a0=P['rwkv_a0'][i],
                 g2=P['rwkv_g2_pad'][i], k_k=P['rwkv_k_k'][i], k_a=P['rwkv_k_a'][i], r_k=P['rwkv_r_k'][i],
                 gn_g=P['rwkv_gn_g'][i], gn_b=P['rwkv_gn_b'][i])
        u3 = u.reshape(bsz, t_len, AB_PAD_COLS)
        if is_ctx:
            a_out, a_state = rwkv_mix(u3, p, None, i, outs['rwkv'], P['n_ab'])
            outs['rwkv'] = a_state
        else:
            a_out, _ = rwkv_mix(u3, p, caches[2], i)
        (q,) = rms_matmul(u, AB_CQ_BLK, Q_LORA, P['mla_q_norm'][i], P['mla_w_uq_r'][i],
                          normalize=True, emit_norm=False, out_dtype=F32)
        kv, ckv = rms_matmul(u, AB_CKV_BLK, KV_LORA, P['mla_kv_norm'][i], P['mla_w_ukv_r'][i],
                             normalize=True, emit_norm=True, out_dtype=BF16)
        kr = u3[:, :, AB_KR_COL:AB_KR_COL + ROPE_D]
        q = q.reshape(bsz, t_len, -1)
        kv = kv.reshape(bsz, t_len, -1)
        if is_ctx:
            outs['ckv'].append(ckv.reshape(bsz, t_len, KV_LORA))
            outs['kr'].append(kr)
            keys_r = kr
        else:
            hn = B_HEADS * NOPE
            q_rope = _apply_axial_rope(q[..., hn:].reshape(bsz, t_len, B_HEADS, ROPE_D), rope_tabs)
            q = jnp.concatenate([q[..., :hn], q_rope.reshape(bsz, t_len, B_HEADS * ROPE_D)], axis=-1)
            kr = _apply_axial_rope(kr[:, :, None, :], rope_tabs)[:, :, 0, :]
            past = caches[0].shape[2]
            (kv_ctx,) = rms_matmul(caches[0][:, i].reshape(bsz * past, KV_LORA), 0, KV_LORA,
                                   P['mla_kv_norm'][i], P['mla_w_ukv_r'][i],
                                   normalize=False, emit_norm=False, out_dtype=BF16)
            kv = jnp.concatenate([kv, kv_ctx.reshape(bsz, past, -1)], axis=1)
            keys_r = jnp.concatenate([kr, caches[1][:, i]], axis=1)
        b_out = mla_attention(q, kv, keys_r, tq=256)
        return res_matmul([a_out, b_out.reshape(m, A_WIDTH)], P['ab_w_out'][i], x2, g1,
                          rows_per_mod=rows_per_mod)
    u = nm_matmul(x2, P['g_norm_mix'][l], sc1, sh1, P['c_w_in_pad'][i], rows_per_mod=rows_per_mod, tn=512)
    p = dict(gate_w2=P['gla_gate_w2'][i], gate_b=P['gla_gate_b'][i], gn=P['gla_norm'][i])
    if is_ctx:
        o, outs['gla'] = gla_mix(u.reshape(bsz, t_len, C_PAD_COLS), p, None, i, outs['gla'], P['n_c'])
    else:
        o, _ = gla_mix(u.reshape(bsz, t_len, C_PAD_COLS), p, caches[3], i)
    return res_matmul([o.reshape(m, C_V)], P['c_w_out'][i], x2, g1, rows_per_mod=rows_per_mod)


def kernel(x_prompt, x_sample, cache_mla_ckv, cache_mla_krope, state_rwkv, state_gla, c, c_ctx, w_mod, b_mod, g_norm_mix, g_norm_ffn, ab_w_in, rwkv_mu, rwkv_w0, rwkv_w2, rwkv_a0, rwkv_a2, rwkv_g2, rwkv_k_k, rwkv_k_a, rwkv_r_k, rwkv_gn_g, rwkv_gn_b, mla_q_norm, mla_w_uq, mla_kv_norm, mla_w_ukv, ab_w_out, c_w_in, gla_gate_w2, gla_gate_b, gla_norm, c_w_out, moe_router, moe_w_gate, moe_w_up, moe_w_down, g_final):
    n_ab = ab_w_in.shape[0]
    n_c = c_w_in.shape[0]
    w_uq = mla_w_uq.reshape(n_ab, Q_LORA, B_HEADS, NOPE + ROPE_D)
    w_ukv = mla_w_ukv.reshape(n_ab, KV_LORA, B_HEADS, NOPE + V_HEAD)
    P = dict(
        n_ab=n_ab, n_c=n_c, g_norm_mix=g_norm_mix, g_norm_ffn=g_norm_ffn, g_final=g_final,
        ab_w_in_pad=_ab_layout(ab_w_in.astype(BF16)), rwkv_mu_pad=_ab_layout(rwkv_mu),
        rwkv_w0=rwkv_w0, rwkv_w2=rwkv_w2, rwkv_a0=rwkv_a0, rwkv_a2=rwkv_a2,
        rwkv_g2_pad=jnp.pad(rwkv_g2, ((0, 0), (0, G_LORA_PAD - G_LORA), (0, 0))),
        rwkv_k_k=rwkv_k_k, rwkv_k_a=rwkv_k_a, rwkv_r_k=rwkv_r_k.reshape(n_ab, A_WIDTH),
        rwkv_gn_g=rwkv_gn_g, rwkv_gn_b=rwkv_gn_b,
        mla_q_norm=mla_q_norm, mla_kv_norm=mla_kv_norm,
        mla_w_uq_r=jnp.concatenate([w_uq[..., :NOPE].reshape(n_ab, Q_LORA, -1),
                                    w_uq[..., NOPE:].reshape(n_ab, Q_LORA, -1)], axis=-1).astype(BF16),
        mla_w_ukv_r=jnp.concatenate([w_ukv[..., :NOPE].reshape(n_ab, KV_LORA, -1),
                                     w_ukv[..., NOPE:].reshape(n_ab, KV_LORA, -1)], axis=-1).astype(BF16),
        ab_w_out=ab_w_out.astype(BF16),
        c_w_in_pad=jnp.stack([_c_layout(c_w_in[i].astype(BF16)) for i in range(n_c)]),
        gla_gate_w2=gla_gate_w2, gla_gate_b=gla_gate_b, gla_norm=gla_norm, c_w_out=c_w_out.astype(BF16),
    )
    router_pad = _pad_cols(moe_router, LANES)

    cvecs = jnp.concatenate([c_ctx[None, :], c, jnp.zeros((8 - 1 - c.shape[0], D_MODEL), F32)], axis=0)
    mods = modulation_all(cvecs, w_mod, b_mod)
    n_lat = c.shape[0]

    rope_tabs = _axial_rope_tables(x_sample.shape[1])
    caches = (cache_mla_ckv, cache_mla_krope, state_rwkv, state_gla)
    bc, tc, _ = x_prompt.shape
    bl, tl, _ = x_sample.shape
    cap_c = CAP_FACTOR * tc // N_EXPERTS
    cap_l = CAP_FACTOR * tl // N_EXPERTS
    xc = x_prompt.reshape(bc * tc, D_MODEL)
    xl = x_sample.reshape(bl * tl, D_MODEL)
    outs = dict(ckv=[], kr=[], rwkv=None, gla=None)
    for l in range(DEPTH):
        mod_c = [t[:, None, :] for t in jnp.split(mods[l, 0:1], N_MOD, axis=-1)]
        mod_l = [t[:, None, :] for t in jnp.split(mods[l, 1:1 + n_lat], N_MOD, axis=-1)]
        xc = _mixer_layer(xc, l, mod_c, P, None, None, bc, tc, outs)
        xl = _mixer_layer(xl, l, mod_l, P, caches, rope_tabs, bl, tl, None)
        xc3 = xc.reshape(bc, tc, D_MODEL)
        xl3 = xl.reshape(bl, tl, D_MODEL)
        xe_c, rank_c, aff_c = moe_dispatch(xc3, g_norm_ffn[l], mod_c[4], mod_c[3], router_pad[l], cap_c)
        xe_l, rank_l, aff_l = moe_dispatch(xl3, g_norm_ffn[l], mod_l[4], mod_l[3], router_pad[l], cap_l)
        ye_c, ye_l = moe_experts(xe_c, xe_l, moe_w_gate, moe_w_up, moe_w_down, l)
        xc = moe_combine(xc3, mod_c[5], ye_c, rank_c, aff_c, cap_c).reshape(bc * tc, D_MODEL)
        xl = moe_combine(xl3, mod_l[5], ye_l, rank_l, aff_l, cap_l).reshape(bl * tl, D_MODEL)
    y_prompt = final_norm(xc, g_final).reshape(bc, tc, D_MODEL)
    y_sample = final_norm(xl, g_final).reshape(bl, tl, D_MODEL)
    return (y_prompt, y_sample, jnp.stack(outs['ckv'], axis=1), jnp.stack(outs['kr'], axis=1),
            outs['rwkv'], outs['gla'])
```

```python
import functools

import jax
import jax.numpy as jnp
from jax import lax
from jax.experimental import pallas as pl
from jax.experimental.pallas import tpu as pltpu

F32 = jnp.float32
BF16 = jnp.bfloat16
HIGHEST = lax.Precision.HIGHEST

VMEM_LIMIT_BYTES = 58 * 1024 * 1024
LANES = 128

D_MODEL = 2048
DEPTH = 4
GRID_W = 64
N_MOD = 6
NORM_EPS = 1e-6

A_WIDTH = 1024
A_HEAD = 64
A_HEADS = 16
W_LORA = 64
ICL_LORA = 64
G_LORA = 160
G_LORA_PAD = 256
GN_EPS = 64e-5

B_HEADS = 8
NOPE = 128
ROPE_D = 64
V_HEAD = 128
Q_LORA = 512
KV_LORA = 512
ROPE_BASE = 10000.0

C_HEADS = 4
C_QK = 1024
C_V = 2048
C_DK = 256
C_DV = 512
GATE_LORA = 16
GATE_TAU = 16.0
CHUNK = 64

N_EXPERTS = 16
D_EXPERT = 4096
CAP_FACTOR = 2

AB_PAD_COLS = 5120
AB_XG_BLK = 3072 // G_LORA_PAD
AB_XW_BLK = 3328 // LANES
AB_XA_BLK = 3456 // LANES
AB_CQ_BLK = 3584 // Q_LORA
AB_CKV_BLK = 4096 // KV_LORA
AB_KR_COL = 4608
AB_SHIFT_COLS = 3584
C_PAD_COLS = 6656
C_XG_BLK = 6144 // LANES

TM = 1024


def _cparams(sem):
    return pltpu.CompilerParams(dimension_semantics=sem, vmem_limit_bytes=VMEM_LIMIT_BYTES)


def _sigmoid(x):
    return 1.0 / (1.0 + jnp.exp(-x))


def _softplus(x):
    return jnp.maximum(x, 0.0) + jnp.log(1.0 + jnp.exp(-jnp.abs(x)))


def _silu(x):
    return x * _sigmoid(x)


def _dot_01(a, b, m01_left=True):
    x = b if m01_left else a
    hi = x.astype(BF16)
    r1 = x - hi.astype(F32)
    mid = r1.astype(BF16)
    lo = (r1 - mid.astype(F32)).astype(BF16)
    if m01_left:
        return jnp.dot(jnp.concatenate([a, a, a], axis=1), jnp.concatenate([hi, mid, lo], axis=0),
                       preferred_element_type=F32)
    return jnp.dot(jnp.concatenate([hi, mid, lo], axis=1), jnp.concatenate([b, b, b], axis=0),
                   preferred_element_type=F32)


def _dot_x3(a, b, stack=True):
    ah = a.astype(BF16)
    al = (a - ah.astype(F32)).astype(BF16)
    bh = b.astype(BF16)
    bl = (b - bh.astype(F32)).astype(BF16)
    if stack:
        return jnp.dot(jnp.concatenate([ah, ah, al], axis=1), jnp.concatenate([bh, bl, bh], axis=0),
                       preferred_element_type=F32)
    return (jnp.dot(ah, bh, preferred_element_type=F32) + jnp.dot(ah, bl, preferred_element_type=F32)
            + jnp.dot(al, bh, preferred_element_type=F32))


def _mod_kernel(c_ref, w_ref, b_ref, o_ref):
    a = _silu(c_ref[...])
    o_ref[...] = _dot_x3(a, w_ref[...], stack=False) + b_ref[...]


def modulation_all(cvecs, w_mod, b_mod):
    tn = 1024
    n = w_mod.shape[-1]
    return pl.pallas_call(
        _mod_kernel,
        grid=(DEPTH, n // tn),
        in_specs=[
            pl.BlockSpec((8, D_MODEL), lambda l, j: (0, 0)),
            pl.BlockSpec((None, D_MODEL, tn), lambda l, j: (l, 0, j)),
            pl.BlockSpec((None, 1, tn), lambda l, j: (l, 0, j)),
        ],
        out_specs=pl.BlockSpec((None, 8, tn), lambda l, j: (l, 0, j)),
        out_shape=jax.ShapeDtypeStruct((DEPTH, 8, n), F32),
        compiler_params=_cparams(("parallel", "parallel")),
        name="modulation",
    )(cvecs, w_mod, b_mod.reshape(DEPTH, 1, n))


def _norm_mod(x, g, sc, sh):
    ms = jnp.mean(x * x, axis=-1, keepdims=True)
    y = x * lax.rsqrt(ms + NORM_EPS) * g
    return y * (1.0 + sc) + sh


def _nm_mm_kernel(x_ref, g_ref, sc_ref, sh_ref, w_ref, *rest, shift_t):
    if shift_t:
        mu_ref, o_ref, h_scr = rest
    else:
        o_ref, h_scr = rest

    @pl.when(pl.program_id(1) == 0)
    def _():
        h_scr[...] = _norm_mod(x_ref[...], g_ref[...], sc_ref[...], sh_ref[...]).astype(BF16)

    acc = jnp.dot(h_scr[...], w_ref[...].astype(BF16), preferred_element_type=F32)
    if not shift_t:
        o_ref[...] = acc
        return
    tn = acc.shape[1]
    shifted_tiles = -(-AB_SHIFT_COLS // tn)

    @pl.when(pl.program_id(1) < shifted_tiles)
    def _():
        tm = acc.shape[0]
        row = lax.broadcasted_iota(jnp.int32, acc.shape, 0) & (shift_t - 1)
        prev = jnp.where(row == 0, 0.0, pltpu.roll(acc, 1, 0))
        nxt = jnp.where(row == shift_t - 1, 0.0, pltpu.roll(acc, tm - 1, 0))
        o_ref[...] = acc + mu_ref[0:1, :] * (prev - acc) + mu_ref[1:2, :] * (nxt - acc)

    @pl.when(pl.program_id(1) >= shifted_tiles)
    def _():
        o_ref[...] = acc


def nm_matmul(x, g, sc, sh, w, *, rows_per_mod, tn, mu=None, shift_t=0):
    m, d = x.shape
    n = w.shape[1]
    tpm = rows_per_mod // TM
    in_specs = [
        pl.BlockSpec((TM, d), lambda i, j: (i, 0)),
        pl.BlockSpec((1, d), lambda i, j: (0, 0)),
        pl.BlockSpec((None, 1, d), lambda i, j: (i // tpm, 0, 0)),
        pl.BlockSpec((None, 1, d), lambda i, j: (i // tpm, 0, 0)),
        pl.BlockSpec((d, tn), lambda i, j: (0, j)),
    ]
    args = [x, g.reshape(1, d), sc, sh, w]
    if shift_t:
        in_specs.append(pl.BlockSpec((2, tn), lambda i, j: (0, j)))
        args.append(mu)
    return pl.pallas_call(
        functools.partial(_nm_mm_kernel, shift_t=shift_t),
        grid=(m // TM, n // tn),
        in_specs=in_specs,
        out_specs=pl.BlockSpec((TM, tn), lambda i, j: (i, j)),
        out_shape=jax.ShapeDtypeStruct((m, n), F32),
        scratch_shapes=[pltpu.VMEM((TM, d), BF16)],
        compiler_params=_cparams(("parallel", "arbitrary")),
        name="norm_mod_matmul",
    )(*args)


def _rms_mm_kernel(x_ref, g_ref, w_ref, *out_refs, normalize, emit_norm):
    x = x_ref[...]
    if normalize:
        ms = jnp.mean(x * x, axis=-1, keepdims=True)
        x = x * lax.rsqrt(ms + NORM_EPS) * g_ref[...]
    if emit_norm:
        out_refs[1][...] = x
    out_refs[0][...] = jnp.dot(x.astype(BF16), w_ref[...].astype(BF16),
                               preferred_element_type=F32).astype(out_refs[0].dtype)


def rms_matmul(x, col_blk, k, g, w, *, normalize, emit_norm, out_dtype):
    m = x.shape[0]
    n = w.shape[1]
    tm = min(TM, m)
    out_shape = [jax.ShapeDtypeStruct((m, n), out_dtype)]
    out_specs = [pl.BlockSpec((tm, n), lambda i: (i, 0))]
    if emit_norm:
        out_shape.append(jax.ShapeDtypeStruct((m, k), F32))
        out_specs.append(pl.BlockSpec((tm, k), lambda i: (i, 0)))
    res = pl.pallas_call(
        functools.partial(_rms_mm_kernel, normalize=normalize, emit_norm=emit_norm),
        grid=(m // tm,),
        in_specs=[
            pl.BlockSpec((tm, k), lambda i: (i, col_blk)),
            pl.BlockSpec((1, k), lambda i: (0, 0)),
            pl.BlockSpec((k, n), lambda i: (0, 0)),
        ],
        out_specs=out_specs,
        out_shape=out_shape,
        compiler_params=_cparams(("parallel",)),
        name="rms_matmul",
    )(x, g.reshape(1, k), w)
    return res


def _res_mm_kernel(*refs, n_x):
    x_refs = refs[:n_x]
    w_refs = refs[n_x:2 * n_x]
    res_ref, gate_ref, o_ref = refs[2 * n_x:]
    acc = jnp.dot(x_refs[0][...], w_refs[0][...].astype(BF16), preferred_element_type=F32)
    for xr, wr in zip(x_refs[1:], w_refs[1:]):
        acc = acc + jnp.dot(xr[...], wr[...].astype(BF16), preferred_element_type=F32)
    o_ref[...] = res_ref[...] + gate_ref[...] * acc


def res_matmul(xs, w, res, gate, *, rows_per_mod, tn=512):
    m, n = res.shape
    tpm = rows_per_mod // TM
    in_specs, k0 = [], 0
    for xx in xs:
        in_specs.append(pl.BlockSpec((TM, xx.shape[1]), lambda i, j: (i, 0)))
    for xx in xs:
        kx = xx.shape[1]
        blk = k0 // kx
        in_specs.append(pl.BlockSpec((kx, tn), lambda i, j, blk=blk: (blk, j)))
        k0 += kx
    in_specs.append(pl.BlockSpec((TM, tn), lambda i, j: (i, j)))
    in_specs.append(pl.BlockSpec((None, 1, tn), lambda i, j: (i // tpm, 0, j)))
    return pl.pallas_call(
        functools.partial(_res_mm_kernel, n_x=len(xs)),
        grid=(m // TM, n // tn),
        in_specs=in_specs,
        out_specs=pl.BlockSpec((TM, tn), lambda i, j: (i, j)),
        out_shape=jax.ShapeDtypeStruct((m, n), F32),
        compiler_params=_cparams(("parallel", "parallel")),
        name="residual_matmul",
    )(*xs, *([w] * len(xs)), res, gate)


def _final_norm_kernel(x_ref, g_ref, o_ref):
    x = x_ref[...]
    ms = jnp.mean(x * x, axis=-1, keepdims=True)
    o_ref[...] = x * lax.rsqrt(ms + NORM_EPS) * g_ref[...]


def final_norm(x, g):
    m, d = x.shape
    return pl.pallas_call(
        _final_norm_kernel,
        grid=(m // TM,),
        in_specs=[pl.BlockSpec((TM, d), lambda i: (i, 0)), pl.BlockSpec((1, d), lambda i: (0, 0))],
        out_specs=pl.BlockSpec((TM, d), lambda i: (i, 0)),
        out_shape=jax.ShapeDtypeStruct((m, d), F32),
        compiler_params=_cparams(("parallel",)),
        name="final_norm",
    )(x, g.reshape(1, d))


def _head_ones(width):
    r = lax.broadcasted_iota(jnp.int32, (width, width), 0) // A_HEAD
    c = lax.broadcasted_iota(jnp.int32, (width, width), 1) // A_HEAD
    return (r == c).astype(BF16)


def _head_eye(width):
    r = lax.broadcasted_iota(jnp.int32, (A_HEAD, width), 0)
    c = lax.broadcasted_iota(jnp.int32, (A_HEAD, width), 1) & (A_HEAD - 1)
    return r == c


RWKV_G = 4
RWKV_HPB = 2
RWKV_TB = 256
RWKV_W = RWKV_HPB * LANES
RWKV_MM_GROUPS = 2
RWKV_UNROLL = 8


def _head_sum(z, ones16):
    ones128 = ones16[:LANES, :LANES]
    parts = [_dot_01(z[:, j * LANES:(j + 1) * LANES], ones128, m01_left=False)
             for j in range(z.shape[1] // LANES)]
    return parts[0] if len(parts) == 1 else jnp.concatenate(parts, axis=1)


def _rwkv_scan_kernel(*refs, has_s0, has_acc):
    views = (refs[0:5], refs[5:10])
    w2_ref, w0_ref, a2_ref, a0_ref, kk_ref, ka_ref = refs[10:16]
    rest = refs[16:]
    if has_s0:
        s0_ref, rest = rest[0], rest[1:]
    if has_acc:
        rest = rest[1:]
    yf_ref, yb_ref, sfin_ref, a_s, w_s, b_s, k_s, st_s = rest
    y_refs = (yf_ref, yb_ref)
    n_g, tb_len, width = views[0][0].shape
    n_heads = width // A_HEAD
    tb = pl.program_id(2)
    n_tb = pl.num_programs(2)
    ones16 = _head_ones(width)
    eye = _head_eye(width)
    eye_f32 = eye.astype(F32)
    eye_mask16 = eye_f32.astype(BF16) > 0.5
    zero16 = jnp.zeros((A_HEAD, width), BF16)

    for d in range(2):
        _, k_ref, _, xw_ref, xa_ref = views[d]
        for g in range(n_g):
            kraw = k_ref[g]
            kk = kraw * kk_ref[...]
            kk = kk * lax.rsqrt(jnp.maximum(_head_sum(kk * kk, ones16), 1e-24))
            a_s[d, g] = -kk
            xw = xw_ref[g][:, d * W_LORA:(d + 1) * W_LORA]
            xa = xa_ref[g][:, d * ICL_LORA:(d + 1) * ICL_LORA]
            wl = w0_ref[d] + _dot_x3(jnp.tanh(xw), w2_ref[d])
            w_s[d, g] = jnp.exp(-jnp.exp(-_softplus(-wl) - 0.5))
            a = _sigmoid(a0_ref[d] + _dot_x3(xa, a2_ref[d]))
            k_s[d, g] = kraw * (1.0 + (a - 1.0) * ka_ref[...])
            b_s[d, g] = kk * a

    chains = [(d, g) for d in range(2) for g in range(n_g)]

    @pl.when(tb == 0)
    def _():
        for d, g in chains:
            if has_s0:
                st_s[d, g] = jnp.concatenate([s0_ref[g, d, h] for h in range(n_heads)], axis=1)
            else:
                st_s[d, g] = jnp.zeros((A_HEAD, width), F32)

    group_len = len(chains) // RWKV_MM_GROUPS
    groups = [chains[i * group_len:(i + 1) * group_len] for i in range(RWKV_MM_GROUPS)]

    def readout(yb):
        m = yb * eye_f32
        m8 = m[0:8]
        for i in range(1, A_HEAD // 8):
            m8 = m8 + m[8 * i:8 * i + 8]
        return jnp.sum(m8, axis=0, keepdims=True)

    def step(rows, prev_rows):
        for grp in groups:
            lhs = []
            for d, g in grp:
                r_ref, _, v_ref, _, _ = views[d]
                s = st_s[d, g].astype(BF16)
                lhs.append(s * a_s[d, g, rows[d], :].astype(BF16))
                lhs.append(jnp.where(eye, v_ref[g, rows[d], :], 0.0).astype(BF16))
                lhs.append(s * r_ref[g, prev_rows[d], :].astype(BF16))
            res = jnp.dot(jnp.concatenate(lhs, axis=0), ones16, preferred_element_type=F32)
            for i, (d, g) in enumerate(grp):
                sa, vb, yb = (res[(3 * i + q) * A_HEAD:(3 * i + q + 1) * A_HEAD] for q in range(3))
                y_refs[d][g, prev_rows[d], :] = readout(yb)
                row = rows[d]
                st_s[d, g] = st_s[d, g] * w_s[d, g, row, :] + sa * b_s[d, g, row, :] + vb * k_s[d, g, row, :]

    def steps(tile, carry):
        n = RWKV_UNROLL
        base_f = pl.multiple_of(tile * n, n)
        base_b = pl.multiple_of(tb_len - n - tile * n, n)
        prev_f = pl.multiple_of(jnp.maximum(tile * n - n, 0), n)
        prev_b = pl.multiple_of(jnp.minimum(tb_len - tile * n, tb_len - n), n)
        for i in range(n):
            rows = [pl.ds(base_f + i, 1), pl.ds(base_b + (n - 1 - i), 1)]
            if i == 0:
                prev_rows = [pl.ds(prev_f + (n - 1), 1), pl.ds(prev_b, 1)]
            else:
                prev_rows = [pl.ds(base_f + (i - 1), 1), pl.ds(base_b + (n - i), 1)]
            step(rows, prev_rows)
        return carry

    lax.fori_loop(0, tb_len // RWKV_UNROLL, steps, 0)
    last_rows = (pl.ds(tb_len - 1, 1), pl.ds(0, 1))
    lhs = [(st_s[d, g] * views[d][0][g, last_rows[d], :]).astype(BF16) for d, g in chains]
    res = jnp.dot(jnp.concatenate(lhs, axis=0), ones16, preferred_element_type=F32)
    for i, (d, g) in enumerate(chains):
        y_refs[d][g, last_rows[d], :] = readout(res[i * A_HEAD:(i + 1) * A_HEAD])

    @pl.when(tb == n_tb - 1)
    def _():
        for d, g in chains:
            s = st_s[d, g]
            for h in range(n_heads):
                sfin_ref[g, d, h] = s[:, h * A_HEAD:(h + 1) * A_HEAD]


def rwkv_scan(u, p, s0, layer_idx, states_acc=None, n_state_layers=1):
    bsz, t_len, _ = u.shape
    state_layer = layer_idx % n_state_layers
    n_tb = t_len // RWKV_TB
    n_hpg = A_WIDTH // RWKV_W
    fwd = lambda tb: tb
    bwd = lambda tb: n_tb - 1 - tb

    def view(tmap):
        def col(first_blk, width=RWKV_W, per_group=True):
            return pl.BlockSpec((RWKV_G, RWKV_TB, width),
                                lambda bi, hg, tb: (bi, tmap(tb), first_blk + (hg if per_group else 0)))
        return [col(0), col(n_hpg), col(2 * n_hpg),
                col(AB_XW_BLK, LANES, False), col(AB_XA_BLK, LANES, False)]

    in_specs = view(fwd) + view(bwd) + [
        pl.BlockSpec((2, W_LORA, RWKV_W), lambda bi, hg, tb: (0, 0, hg)),
        pl.BlockSpec((2, 1, RWKV_W), lambda bi, hg, tb: (0, 0, hg)),
        pl.BlockSpec((2, ICL_LORA, RWKV_W), lambda bi, hg, tb: (0, 0, hg)),
        pl.BlockSpec((2, 1, RWKV_W), lambda bi, hg, tb: (0, 0, hg)),
        pl.BlockSpec((1, RWKV_W), lambda bi, hg, tb: (0, hg)),
        pl.BlockSpec((1, RWKV_W), lambda bi, hg, tb: (0, hg)),
    ]
    args = [u] * 10 + [p['w2'], p['w0'].reshape(2, 1, A_WIDTH), p['a2'], p['a0'].reshape(2, 1, A_WIDTH),
                       p['k_k'].reshape(1, A_WIDTH), p['k_a'].reshape(1, A_WIDTH)]
    heads_blk = 2 * RWKV_HPB
    if s0 is not None:
        in_specs.append(pl.BlockSpec((RWKV_G, None, 2, heads_blk, A_HEAD, A_HEAD),
                                     lambda bi, hg, tb: (bi, layer_idx, 0, hg, 0, 0)))
        args.append(s0)
    aliases = {}
    if states_acc is not None:
        in_specs.append(pl.BlockSpec(memory_space=pl.ANY))
        args.append(states_acc)
        aliases = {len(args) - 1: 2}
    blk = (2, RWKV_G, RWKV_TB, RWKV_W)
    return pl.pallas_call(
        functools.partial(_rwkv_scan_kernel, has_s0=s0 is not None, has_acc=states_acc is not None),
        grid=(bsz // RWKV_G, n_hpg, n_tb),
        in_specs=in_specs,
        out_specs=[
            pl.BlockSpec((RWKV_G, RWKV_TB, RWKV_W), lambda bi, hg, tb: (bi, fwd(tb), hg)),
            pl.BlockSpec((RWKV_G, RWKV_TB, RWKV_W), lambda bi, hg, tb: (bi, bwd(tb), hg)),
            pl.BlockSpec((RWKV_G, None, 2, heads_blk, A_HEAD, A_HEAD),
                         lambda bi, hg, tb: (bi, state_layer, 0, hg, 0, 0)),
        ],
        out_shape=[
            jax.ShapeDtypeStruct((bsz, t_len, A_WIDTH), F32),
            jax.ShapeDtypeStruct((bsz, t_len, A_WIDTH), F32),
            jax.ShapeDtypeStruct((bsz, n_state_layers, 2, A_HEADS, A_HEAD, A_HEAD), F32),
        ],
        input_output_aliases=aliases,
        scratch_shapes=[
            pltpu.VMEM(blk, F32),
            pltpu.VMEM(blk, F32),
            pltpu.VMEM(blk, F32),
            pltpu.VMEM(blk, F32),
            pltpu.VMEM((2, RWKV_G, A_HEAD, RWKV_W), F32),
        ],
        compiler_params=_cparams(("parallel", "parallel", "arbitrary")),
        name="rwkv7_scan",
    )(*args)


def _rwkv_post_kernel(yf_ref, yb_ref, r_ref, k_ref, v_ref, xa_ref, xg_ref,
                      a2_ref, a0_ref, ka_ref, rk_ref, gng_ref, gnb_ref, g2_ref, o_ref):
    ones_bd = _head_ones(LANES)
    inv_n = 1.0 / A_HEAD
    y = yf_ref[...] + yb_ref[...]
    mu = _head_sum(y, ones_bd) * inv_n
    yc = y - mu
    var = _head_sum(yc * yc, ones_bd) * inv_n
    yn = yc * lax.rsqrt(var + GN_EPS) * gng_ref[...] + gnb_ref[...]
    r = r_ref[...]
    kraw = k_ref[...]
    rk_sum = jnp.zeros_like(r)
    for d in range(2):
        xa = xa_ref[:, d * ICL_LORA:(d + 1) * ICL_LORA]
        a = _sigmoid(a0_ref[d] + _dot_x3(xa, a2_ref[d]))
        rk_sum = rk_sum + r * (kraw * (1.0 + (a - 1.0) * ka_ref[...])) * rk_ref[...]
    bonus = _head_sum(rk_sum, ones_bd) * v_ref[...]
    gate = jnp.dot(_sigmoid(xg_ref[...]).astype(BF16), g2_ref[...].astype(BF16), preferred_element_type=F32)
    o_ref[...] = ((yn + bonus) * gate).astype(o_ref.dtype)


def rwkv_post(yf, yb, u2, p):
    m = yf.shape[0]
    tm = 256
    full = lambda shape: pl.BlockSpec(shape, lambda i: (0,) * len(shape))
    row = lambda width, blk: pl.BlockSpec((tm, width), lambda i: (i, blk))
    return pl.pallas_call(
        _rwkv_post_kernel,
        grid=(m // tm,),
        in_specs=[
            row(A_WIDTH, 0), row(A_WIDTH, 0),
            row(A_WIDTH, 0), row(A_WIDTH, 1), row(A_WIDTH, 2),
            row(LANES, AB_XA_BLK), row(G_LORA_PAD, AB_XG_BLK),
            full((2, ICL_LORA, A_WIDTH)), full((2, 1, A_WIDTH)),
            full((1, A_WIDTH)), full((1, A_WIDTH)), full((1, A_WIDTH)), full((1, A_WIDTH)),
            full((G_LORA_PAD, A_WIDTH)),
        ],
        out_specs=row(A_WIDTH, 0),
        out_shape=jax.ShapeDtypeStruct((m, A_WIDTH), BF16),
        compiler_params=_cparams(("parallel",)),
        name="rwkv7_post",
    )(yf, yb, u2, u2, u2, u2, u2, p['a2'], p['a0'].reshape(2, 1, A_WIDTH), p['k_a'].reshape(1, A_WIDTH),
      p['r_k'].reshape(1, A_WIDTH), p['gn_g'].reshape(1, A_WIDTH), p['gn_b'].reshape(1, A_WIDTH), p['g2'])


def rwkv_mix(u, p, s0, layer_idx, states_acc=None, n_state_layers=1):
    bsz, t_len, _ = u.shape
    m = bsz * t_len
    yf, yb, sfin = rwkv_scan(u, p, s0, layer_idx, states_acc, n_state_layers)
    out = rwkv_post(yf.reshape(m, A_WIDTH), yb.reshape(m, A_WIDTH), u.reshape(m, AB_PAD_COLS), p)
    return out, sfin


def _mla_kernel(qn_ref, qr_ref, kn_ref, kr_ref, v_ref, o_ref):
    scale = (NOPE + ROPE_D) ** -0.5
    krb = kr_ref[...].astype(BF16)
    nt = (((1,), (1,)), ((), ()))
    for h in range(B_HEADS):
        qn = qn_ref[:, h * NOPE:(h + 1) * NOPE].astype(BF16)
        qr = qr_ref[:, h * ROPE_D:(h + 1) * ROPE_D].astype(BF16)
        s = lax.dot_general(qn, kn_ref[:, h * NOPE:(h + 1) * NOPE], nt, preferred_element_type=F32)
        s = s + lax.dot_general(qr, krb, nt, preferred_element_type=F32)
        s = s * scale
        m = jnp.max(s, axis=-1, keepdims=True)
        e = jnp.exp(s - m)
        p = e / jnp.sum(e, axis=-1, keepdims=True)
        o = jnp.dot(p.astype(BF16), v_ref[:, h * V_HEAD:(h + 1) * V_HEAD], preferred_element_type=F32)
        o_ref[:, h * V_HEAD:(h + 1) * V_HEAD] = o.astype(o_ref.dtype)


def mla_attention(q, kv, kr, *, tq):
    bsz, t_len, _ = q.shape
    s_len = kv.shape[1]
    hn = B_HEADS * NOPE
    return pl.pallas_call(
        _mla_kernel,
        grid=(bsz, t_len // tq),
        in_specs=[
            pl.BlockSpec((None, tq, hn), lambda b, i: (b, i, 0)),
            pl.BlockSpec((None, tq, B_HEADS * ROPE_D), lambda b, i: (b, i, hn // (B_HEADS * ROPE_D))),
            pl.BlockSpec((None, s_len, hn), lambda b, i: (b, 0, 0)),
            pl.BlockSpec((None, s_len, ROPE_D), lambda b, i: (b, 0, 0)),
            pl.BlockSpec((None, s_len, hn), lambda b, i: (b, 0, 1)),
        ],
        out_specs=pl.BlockSpec((None, tq, hn), lambda b, i: (b, i, 0)),
        out_shape=jax.ShapeDtypeStruct((bsz, t_len, hn), BF16),
        compiler_params=_cparams(("parallel", "parallel")),
        name="mla_attention",
    )(q, q, kv, kr, kv)


GLA_SUPER = 4


def _gla_kernel(q_ref, k_ref, v_ref, og_ref, xg_ref, gw_ref, gb_ref, gn_ref, *rest, has_s0, has_acc):
    if has_s0:
        s0_ref, rest = rest[0], rest[1:]
    if has_acc:
        rest = rest[1:]
    o_ref, sfin_ref = rest[:2]
    lg_s, y_s, st_s, qt_s, u_s, dec_s = (rest[2 + 2 * i:4 + 2 * i] for i in range(6))
    t_len = q_ref.shape[0]
    n_chunks = t_len // CHUNK
    scale = C_DK ** -0.5
    nt = (((1,), (1,)), ((), ()))
    xg = xg_ref[...]
    span = GLA_SUPER * CHUNK
    ri = lax.broadcasted_iota(jnp.int32, (span, span), 0)
    ci = lax.broadcasted_iota(jnp.int32, (span, span), 1)
    same_chunk = (ri // CHUNK) == (ci // CHUNK)
    keeps = (same_chunk & (ri >= ci), same_chunk & (ri <= ci))
    tris = tuple(kp.astype(BF16) for kp in keeps)
    for d in range(2):
        z = _dot_x3(xg[:, d * GATE_LORA:(d + 1) * GATE_LORA], gw_ref[d]) + gb_ref[d]
        lg_s[d][...] = -_softplus(-z) * (1.0 / GATE_TAU)
        if has_s0:
            st_s[d][...] = s0_ref[d].T
        else:
            st_s[d][...] = jnp.zeros(st_s[d].shape, F32)

    def chunks_local(sc, carry):
        rows = pl.ds(pl.multiple_of(sc * span, span), span)
        q = q_ref[rows, :] * scale
        k = k_ref[rows, :]
        v = v_ref[rows, :]
        vb = v.astype(BF16)
        for d in range(2):
            b = _dot_01(tris[d], lg_s[d][rows, :])
            edge = CHUNK - 1 if d == 0 else 0
            bls = [b[cc * CHUNK + edge:cc * CHUNK + edge + 1, :] for cc in range(GLA_SUPER)]
            bl = jnp.concatenate([jnp.broadcast_to(x, (CHUNK, C_DK)) for x in bls], axis=0)
            qt = (q * jnp.exp(b)).astype(BF16)
            kt = (k * jnp.exp(-b)).astype(BF16)
            kd = (k * jnp.exp(bl - b)).astype(BF16)
            att = jnp.where(keeps[d], lax.dot_general(qt, kt, nt, preferred_element_type=F32), 0.0)
            y_s[d][rows, :] = jnp.dot(att.astype(BF16), vb, preferred_element_type=F32)
            qt_s[d][rows, :] = qt
            for cc in range(GLA_SUPER):
                part = slice(cc * CHUNK, (cc + 1) * CHUNK)
                u_s[d][sc * GLA_SUPER + cc] = jnp.dot(v[part].T.astype(BF16), kd[part], preferred_element_type=F32)
                dec_s[d][sc * GLA_SUPER + cc] = jnp.broadcast_to(jnp.exp(bls[cc]), (8, C_DK))
        return carry

    lax.fori_loop(0, n_chunks // GLA_SUPER, chunks_local, 0)

    def chunk_state(i, carry):
        for d in range(2):
            c = i if d == 0 else n_chunks - 1 - i
            rows = pl.ds(pl.multiple_of(c * CHUNK, CHUNK), CHUNK)
            st = st_s[d][...]
            y_s[d][rows, :] = y_s[d][rows, :] + lax.dot_general(qt_s[d][rows, :], st.astype(BF16), nt,
                                                                preferred_element_type=F32)
            st_s[d][...] = st * dec_s[d][c, 0:1, :] + u_s[d][c]
        return carry

    lax.fori_loop(0, n_chunks, chunk_state, 0)
    for d in range(2):
        sfin_ref[d] = st_s[d][...].T
    y = y_s[0][...] + y_s[1][...]
    yn = y * lax.rsqrt(jnp.mean(y * y, axis=-1, keepdims=True) + NORM_EPS) * gn_ref[...]
    o_ref[...] = (yn * _silu(og_ref[...])).astype(o_ref.dtype)


def gla_mix(u, p, s0, layer_idx, states_acc=None, n_state_layers=1):
    bsz, t_len, _ = u.shape
    state_layer = layer_idx % n_state_layers
    in_specs = [
        pl.BlockSpec((None, t_len, C_DK), lambda b, h: (b, 0, h)),
        pl.BlockSpec((None, t_len, C_DK), lambda b, h: (b, 0, C_HEADS + h)),
        pl.BlockSpec((None, t_len, C_DV), lambda b, h: (b, 0, C_HEADS + h)),
        pl.BlockSpec((None, t_len, C_DV), lambda b, h: (b, 0, 2 * C_HEADS + h)),
        pl.BlockSpec((None, t_len, LANES), lambda b, h: (b, 0, C_XG_BLK)),
        pl.BlockSpec((2, GATE_LORA, C_DK), lambda b, h: (0, 0, h)),
        pl.BlockSpec((2, 1, C_DK), lambda b, h: (0, 0, h)),
        pl.BlockSpec((1, C_DV), lambda b, h: (0, 0)),
    ]
    args = [u, u, u, u, u, p['gate_w2'], p['gate_b'].reshape(2, 1, C_QK), p['gn'].reshape(1, C_DV)]
    if s0 is not None:
        in_specs.append(pl.BlockSpec((None, None, 2, None, C_DK, C_DV), lambda b, h: (b, layer_idx, 0, h, 0, 0)))
        args.append(s0)
    aliases = {}
    if states_acc is not None:
        in_specs.append(pl.BlockSpec(memory_space=pl.ANY))
        args.append(states_acc)
        aliases = {len(args) - 1: 1}
    return pl.pallas_call(
        functools.partial(_gla_kernel, has_s0=s0 is not None, has_acc=states_acc is not None),
        grid=(bsz, C_HEADS),
        in_specs=in_specs,
        out_specs=[
            pl.BlockSpec((None, t_len, C_DV), lambda b, h: (b, 0, h)),
            pl.BlockSpec((None, None, 2, None, C_DK, C_DV), lambda b, h: (b, state_layer, 0, h, 0, 0)),
        ],
        out_shape=[
            jax.ShapeDtypeStruct((bsz, t_len, C_V), BF16),
            jax.ShapeDtypeStruct((bsz, n_state_layers, 2, C_HEADS, C_DK, C_DV), F32),
        ],
        input_output_aliases=aliases,
        scratch_shapes=[
            pltpu.VMEM((t_len, C_DK), F32),
            pltpu.VMEM((t_len, C_DK), F32),
            pltpu.VMEM((t_len, C_DV), F32),
            pltpu.VMEM((t_len, C_DV), F32),
            pltpu.VMEM((C_DV, C_DK), F32),
            pltpu.VMEM((C_DV, C_DK), F32),
            pltpu.VMEM((t_len, C_DK), BF16),
            pltpu.VMEM((t_len, C_DK), BF16),
            pltpu.VMEM((t_len // CHUNK, C_DV, C_DK), F32),
            pltpu.VMEM((t_len // CHUNK, C_DV, C_DK), F32),
            pltpu.VMEM((t_len // CHUNK, 8, C_DK), F32),
            pltpu.VMEM((t_len // CHUNK, 8, C_DK), F32),
        ],
        compiler_params=_cparams(("parallel", "parallel")),
        name="gla_mix",
    )(*args)


def _moe_col_tile(t_len, d):
    return d if t_len <= 256 else 512


def _dispatch_kernel(x_ref, g_ref, sc_ref, sh_ref, wr_ref, xe_ref, rank_ref, aff_ref, hb_s, onehot_s, *, cap):
    t_len = x_ref.shape[0]
    j = pl.program_id(1)

    @pl.when(j == 0)
    def _():
        h = _norm_mod(x_ref[...], g_ref[...], sc_ref[...], sh_ref[...])
        tn = hb_s.shape[2]
        for n in range(hb_s.shape[0]):
            hb_s[n] = h[:, n * tn:(n + 1) * tn].astype(BF16)
        logits = _dot_x3(h, wr_ref[...], stack=False)
        lane = lax.broadcasted_iota(jnp.int32, logits.shape, 1)
        logits = jnp.where(lane < N_EXPERTS, logits, -jnp.inf)
        m = jnp.max(logits, axis=-1, keepdims=True)
        e = jnp.exp(logits - m)
        aff = e / jnp.sum(e, axis=-1, keepdims=True)
        aff_ref[...] = aff
        aff_t = aff.T
        sub_i = lax.broadcasted_iota(jnp.int32, (t_len, t_len), 0)
        lane_i = lax.broadcasted_iota(jnp.int32, (t_len, t_len), 1)
        slot = lax.broadcasted_iota(jnp.int32, (cap, t_len), 0).astype(F32)
        expert_row = lax.broadcasted_iota(jnp.int32, (N_EXPERTS, t_len), 0)
        ranks = jnp.zeros((N_EXPERTS, t_len), F32)
        for ex in range(N_EXPERTS):
            col = aff[:, ex:ex + 1]
            row = aff_t[ex:ex + 1, :]
            ahead = (col > row) | ((col == row) & (sub_i < lane_i))
            rank_row = jnp.sum(ahead.astype(F32), axis=0, keepdims=True)
            onehot_s[ex * cap:(ex + 1) * cap, :] = (rank_row == slot).astype(BF16)
            ranks = jnp.where(expert_row == ex, rank_row, ranks)
        pad = jnp.zeros((LANES - N_EXPERTS, t_len), F32)
        rank_ref[...] = jnp.concatenate([ranks, pad], axis=0).T

    rows = jnp.dot(onehot_s[...], hb_s[j], preferred_element_type=F32).astype(BF16)
    for ex in range(N_EXPERTS):
        xe_ref[ex] = rows[ex * cap:(ex + 1) * cap]


def moe_dispatch(x, g, sc, sh, w_router_pad, cap):
    bsz, t_len, d = x.shape
    per_req = sc.shape[0] > 1
    tn = _moe_col_tile(t_len, d)
    mod_spec = pl.BlockSpec((None, 1, d), (lambda b, j: (b, 0, 0)) if per_req else (lambda b, j: (0, 0, 0)))
    return pl.pallas_call(
        functools.partial(_dispatch_kernel, cap=cap),
        grid=(bsz, d // tn),
        in_specs=[
            pl.BlockSpec((None, t_len, d), lambda b, j: (b, 0, 0)),
            pl.BlockSpec((1, d), lambda b, j: (0, 0)),
            mod_spec, mod_spec,
            pl.BlockSpec((d, LANES), lambda b, j: (0, 0)),
        ],
        out_specs=[
            pl.BlockSpec((N_EXPERTS, cap, tn), lambda b, j: (0, b, j)),
            pl.BlockSpec((None, t_len, LANES), lambda b, j: (b, 0, 0)),
            pl.BlockSpec((None, t_len, LANES), lambda b, j: (b, 0, 0)),
        ],
        out_shape=[
            jax.ShapeDtypeStruct((N_EXPERTS, bsz * cap, d), BF16),
            jax.ShapeDtypeStruct((bsz, t_len, LANES), F32),
            jax.ShapeDtypeStruct((bsz, t_len, LANES), F32),
        ],
        scratch_shapes=[
            pltpu.VMEM((d // tn, t_len, tn), BF16),
            pltpu.VMEM((N_EXPERTS * cap, t_len), BF16),
        ],
        compiler_params=_cparams(("parallel", "arbitrary")),
        name="moe_dispatch",
    )(x, g.reshape(1, d), sc, sh, w_router_pad)


EXPERT_TF = 512
EXPERT_TN = 256
N_UP_STEPS = D_EXPERT // EXPERT_TF
N_DOWN_STEPS = D_MODEL // EXPERT_TN


def _experts_kernel(xc_ref, xl_ref, wg_ref, wu_ref, wd_ref, yc_ref, yl_ref, hid_s):
    s = pl.program_id(1)
    mc = xc_ref.shape[0]

    @pl.when(s < N_UP_STEPS)
    def _():
        wg = wg_ref[...].astype(BF16)
        wu = wu_ref[...].astype(BF16)
        cols = pl.ds(pl.multiple_of(s * EXPERT_TF, EXPERT_TF), EXPERT_TF)
        for x_ref, r0 in ((xc_ref, 0), (xl_ref, mc)):
            x = x_ref[...]
            hg = jnp.dot(x, wg, preferred_element_type=F32)
            hu = jnp.dot(x, wu, preferred_element_type=F32)
            hid_s[r0:r0 + x.shape[0], cols] = (_silu(hg) * hu).astype(BF16)

    @pl.when(s >= N_UP_STEPS)
    def _():
        wd = wd_ref[...].astype(BF16)
        yc_ref[...] = jnp.dot(hid_s[0:mc, :], wd, preferred_element_type=F32).astype(yc_ref.dtype)
        yl_ref[...] = jnp.dot(hid_s[mc:, :], wd, preferred_element_type=F32).astype(yl_ref.dtype)


def moe_experts(xc, xl, w_gate, w_up, w_down, layer):
    mc, ml = xc.shape[1], xl.shape[1]
    up_idx = lambda e, s: (layer, e, 0, jnp.minimum(s, N_UP_STEPS - 1))
    down_idx = lambda e, s: (layer, e, 0, jnp.maximum(s - N_UP_STEPS, 0))
    out_idx = lambda e, s: (e, 0, jnp.maximum(s - N_UP_STEPS, 0))
    return pl.pallas_call(
        _experts_kernel,
        grid=(N_EXPERTS, N_UP_STEPS + N_DOWN_STEPS),
        in_specs=[
            pl.BlockSpec((None, mc, D_MODEL), lambda e, s: (e, 0, 0)),
            pl.BlockSpec((None, ml, D_MODEL), lambda e, s: (e, 0, 0)),
            pl.BlockSpec((None, None, D_MODEL, EXPERT_TF), up_idx),
            pl.BlockSpec((None, None, D_MODEL, EXPERT_TF), up_idx),
            pl.BlockSpec((None, None, D_EXPERT, EXPERT_TN), down_idx),
        ],
        out_specs=[
            pl.BlockSpec((None, mc, EXPERT_TN), out_idx),
            pl.BlockSpec((None, ml, EXPERT_TN), out_idx),
        ],
        out_shape=[
            jax.ShapeDtypeStruct((N_EXPERTS, mc, D_MODEL), BF16),
            jax.ShapeDtypeStruct((N_EXPERTS, ml, D_MODEL), BF16),
        ],
        scratch_shapes=[pltpu.VMEM((mc + ml, D_EXPERT), BF16)],
        compiler_params=_cparams(("parallel", "arbitrary")),
        name="moe_experts",
    )(xc, xl, w_gate, w_up, w_down)


def _combine_kernel(x_ref, gate_ref, ye_ref, rank_ref, aff_ref, o_ref, hi_s, lo_s, *, cap):
    t_len = x_ref.shape[0]

    @pl.when(pl.program_id(1) == 0)
    def _():
        per_group = LANES // cap
        lane = lax.broadcasted_iota(jnp.int32, (t_len, LANES), 1)
        slot = (lane % cap).astype(F32)
        for grp in range(N_EXPERTS // per_group):
            w = jnp.zeros((t_len, LANES), F32)
            for i in range(per_group):
                ex = grp * per_group + i
                hit = rank_ref[:, ex:ex + 1] == slot
                if per_group > 1:
                    hit = hit & (lane // cap == i)
                w = jnp.where(hit, aff_ref[:, ex:ex + 1], w)
            hi = w.astype(BF16)
            hi_s[:, grp * LANES:(grp + 1) * LANES] = hi
            lo_s[:, grp * LANES:(grp + 1) * LANES] = (w - hi.astype(F32)).astype(BF16)

    ye = ye_ref[...].reshape(N_EXPERTS * cap, ye_ref.shape[2])
    acc = jnp.dot(hi_s[...], ye, preferred_element_type=F32) + jnp.dot(lo_s[...], ye, preferred_element_type=F32)
    o_ref[...] = x_ref[...] + gate_ref[...] * acc


def moe_combine(x, gate, ye, rank, aff, cap):
    bsz, t_len, d = x.shape
    per_req = gate.shape[0] > 1
    tn = _moe_col_tile(t_len, d)
    return pl.pallas_call(
        functools.partial(_combine_kernel, cap=cap),
        grid=(bsz, d // tn),
        in_specs=[
            pl.BlockSpec((None, t_len, tn), lambda b, j: (b, 0, j)),
            pl.BlockSpec((None, 1, tn), (lambda b, j: (b, 0, j)) if per_req else (lambda b, j: (0, 0, j))),
            pl.BlockSpec((N_EXPERTS, cap, tn), lambda b, j: (0, b, j)),
            pl.BlockSpec((None, t_len, LANES), lambda b, j: (b, 0, 0)),
            pl.BlockSpec((None, t_len, LANES), lambda b, j: (b, 0, 0)),
        ],
        out_specs=pl.BlockSpec((None, t_len, tn), lambda b, j: (b, 0, j)),
        out_shape=jax.ShapeDtypeStruct((bsz, t_len, d), F32),
        scratch_shapes=[pltpu.VMEM((t_len, N_EXPERTS * cap), BF16), pltpu.VMEM((t_len, N_EXPERTS * cap), BF16)],
        compiler_params=_cparams(("parallel", "arbitrary")),
        name="moe_combine",
    )(x, gate, ye, rank, aff)


def _pad_cols(w, n):
    return jnp.pad(w, ((0, 0),) * (w.ndim - 1) + ((0, n - w.shape[-1]),))


def _ab_layout(w):
    rkv = w[..., :3 * A_WIDTH]
    o = 3 * A_WIDTH
    xg = _pad_cols(w[..., o:o + G_LORA], G_LORA_PAD)
    o += G_LORA
    xw = w[..., o:o + 2 * W_LORA]
    o += 2 * W_LORA
    xa = w[..., o:o + 2 * ICL_LORA]
    o += 2 * ICL_LORA
    rest = w[..., o:]
    return _pad_cols(jnp.concatenate([rkv, xg, xw, xa, rest], axis=-1), AB_PAD_COLS)


def _c_layout(w):
    qkv = w[:, :2 * C_QK + C_V]
    o = 2 * C_QK + C_V
    xg = _pad_cols(w[:, o:o + 2 * GATE_LORA], LANES)
    og = w[:, o + 2 * GATE_LORA:]
    return _pad_cols(jnp.concatenate([qkv, og, xg], axis=-1), C_PAD_COLS)


def _axial_rope_tables(t):
    rows = t // GRID_W
    row = jnp.repeat(jnp.arange(rows, dtype=F32), GRID_W)
    col = jnp.tile(jnp.arange(GRID_W, dtype=F32), rows)
    half = ROPE_D // 2
    inv = 1.0 / (ROPE_BASE ** (jnp.arange(0, half, 2, dtype=F32) / half))
    ang_r = row[:, None] * inv[None, :]
    ang_c = col[:, None] * inv[None, :]
    return (jnp.cos(ang_r), jnp.sin(ang_r), jnp.cos(ang_c), jnp.sin(ang_c))


def _rotate(x, cos, sin):
    m = x.shape[-1] // 2
    x1, x2 = x[..., :m], x[..., m:]
    c = cos[None, :, None, :]
    s = sin[None, :, None, :]
    return jnp.concatenate([x1 * c - x2 * s, x1 * s + x2 * c], axis=-1)


def _apply_axial_rope(x, tabs):
    cr, sr, cc, sc = tabs
    half = ROPE_D // 2
    return jnp.concatenate([_rotate(x[..., :half], cr, sr), _rotate(x[..., half:], cc, sc)], axis=-1)


def _mixer_layer(x2, l, mod_l, P, caches, rope_tabs, bsz, t_len, outs):
    m = bsz * t_len
    is_ctx = caches is None
    rows_per_mod = m if is_ctx else t_len
    sh1, sc1, g1 = mod_l[0], mod_l[1], mod_l[2]
    i = l // 2
    if l % 2 == 0:
        u = nm_matmul(x2, P['g_norm_mix'][l], sc1, sh1, P['ab_w_in_pad'][i], rows_per_mod=rows_per_mod,
                      tn=512, mu=P['rwkv_mu_pad'][i], shift_t=t_len)
        p = dict(w2=P['rwkv_w2'][i], w0=P['rwkv_w0'][i], a2=P['rwkv_a2'][i], a0=P['rwkv_a0'][i],
                 g2=P['rwkv_g2_pad'][i], k_k=P['rwkv_k_k'][i], k_a=P['rwkv_k_a'][i], r_k=P['rwkv_r_k'][i],
                 gn_g=P['rwkv_gn_g'][i], gn_b=P['rwkv_gn_b'][i])
        u3 = u.reshape(bsz, t_len, AB_PAD_COLS)
        if is_ctx:
            a_out, a_state = rwkv_mix(u3, p, None, i, outs['rwkv'], P['n_ab'])
            outs['rwkv'] = a_state
        else:
            a_out, _ = rwkv_mix(u3, p, caches[2], i)
        (q,) = rms_matmul(u, AB_CQ_BLK, Q_LORA, P['mla_q_norm'][i], P['mla_w_uq_r'][i],
                          normalize=True, emit_norm=False, out_dtype=F32)
        kv, ckv = rms_matmul(u, AB_CKV_BLK, KV_LORA, P['mla_kv_norm'][i], P['mla_w_ukv_r'][i],
                             normalize=True, emit_norm=True, out_dtype=BF16)
        kr = u3[:, :, AB_KR_COL:AB_KR_COL + ROPE_D]
        q = q.reshape(bsz, t_len, -1)
        kv = kv.reshape(bsz, t_len, -1)
        if is_ctx:
            outs['ckv'].append(ckv.reshape(bsz, t_len, KV_LORA))
            outs['kr'].append(kr)
            keys_r = kr
        else:
            hn = B_HEADS * NOPE
            q_rope = _apply_axial_rope(q[..., hn:].reshape(bsz, t_len, B_HEADS, ROPE_D), rope_tabs)
            q = jnp.concatenate([q[..., :hn], q_rope.reshape(bsz, t_len, B_HEADS * ROPE_D)], axis=-1)
            kr = _apply_axial_rope(kr[:, :, None, :], rope_tabs)[:, :, 0, :]
            past = caches[0].shape[2]
            (kv_ctx,) = rms_matmul(caches[0][:, i].reshape(bsz * past, KV_LORA), 0, KV_LORA,
                                   P['mla_kv_norm'][i], P['mla_w_ukv_r'][i],
                                   normalize=False, emit_norm=False, out_dtype=BF16)
            kv = jnp.concatenate([kv, kv_ctx.reshape(bsz, past, -1)], axis=1)
            keys_r = jnp.concatenate([kr, caches[1][:, i]], axis=1)
        b_out = mla_attention(q, kv, keys_r, tq=256)
        return res_matmul([a_out, b_out.reshape(m, A_WIDTH)], P['ab_w_out'][i], x2, g1,
                          rows_per_mod=rows_per_mod)
    u = nm_matmul(x2, P['g_norm_mix'][l], sc1, sh1, P['c_w_in_pad'][i], rows_per_mod=rows_per_mod, tn=512)
    p = dict(gate_w2=P['gla_gate_w2'][i], gate_b=P['gla_gate_b'][i], gn=P['gla_norm'][i])
    if is_ctx:
        o, outs['gla'] = gla_mix(u.reshape(bsz, t_len, C_PAD_COLS), p, None, i, outs['gla'], P['n_c'])
    else:
        o, _ = gla_mix(u.reshape(bsz, t_len, C_PAD_COLS), p, caches[3], i)
    return res_matmul([o.reshape(m, C_V)], P['c_w_out'][i], x2, g1, rows_per_mod=rows_per_mod)


def kernel(x_prompt, x_sample, cache_mla_ckv, cache_mla_krope, state_rwkv, state_gla, c, c_ctx, w_mod, b_mod, g_norm_mix, g_norm_ffn, ab_w_in, rwkv_mu, rwkv_w0, rwkv_w2, rwkv_a0, rwkv_a2, rwkv_g2, rwkv_k_k, rwkv_k_a, rwkv_r_k, rwkv_gn_g, rwkv_gn_b, mla_q_norm, mla_w_uq, mla_kv_norm, mla_w_ukv, ab_w_out, c_w_in, gla_gate_w2, gla_gate_b, gla_norm, c_w_out, moe_router, moe_w_gate, moe_w_up, moe_w_down, g_final):
    n_ab = ab_w_in.shape[0]
    n_c = c_w_in.shape[0]
    w_uq = mla_w_uq.reshape(n_ab, Q_LORA, B_HEADS, NOPE + ROPE_D)
    w_ukv = mla_w_ukv.reshape(n_ab, KV_LORA, B_HEADS, NOPE + V_HEAD)
    P = dict(
        n_ab=n_ab, n_c=n_c, g_norm_mix=g_norm_mix, g_norm_ffn=g_norm_ffn, g_final=g_final,
        ab_w_in_pad=_ab_layout(ab_w_in.astype(BF16)), rwkv_mu_pad=_ab_layout(rwkv_mu),
        rwkv_w0=rwkv_w0, rwkv_w2=rwkv_w2, rwkv_a0=rwkv_a0, rwkv_a2=rwkv_a2,
        rwkv_g2_pad=jnp.pad(rwkv_g2, ((0, 0), (0, G_LORA_PAD - G_LORA), (0, 0))),
        rwkv_k_k=rwkv_k_k, rwkv_k_a=rwkv_k_a, rwkv_r_k=rwkv_r_k.reshape(n_ab, A_WIDTH),
        rwkv_gn_g=rwkv_gn_g, rwkv_gn_b=rwkv_gn_b,
        mla_q_norm=mla_q_norm, mla_kv_norm=mla_kv_norm,
        mla_w_uq_r=jnp.concatenate([w_uq[..., :NOPE].reshape(n_ab, Q_LORA, -1),
                                    w_uq[..., NOPE:].reshape(n_ab, Q_LORA, -1)], axis=-1).astype(BF16),
        mla_w_ukv_r=jnp.concatenate([w_ukv[..., :NOPE].reshape(n_ab, KV_LORA, -1),
                                     w_ukv[..., NOPE:].reshape(n_ab, KV_LORA, -1)], axis=-1).astype(BF16),
        ab_w_out=ab_w_out.astype(BF16),
        c_w_in_pad=jnp.stack([_c_layout(c_w_in[i].astype(BF16)) for i in range(n_c)]),
        gla_gate_w2=gla_gate_w2, gla_gate_b=gla_gate_b, gla_norm=gla_norm, c_w_out=c_w_out.astype(BF16),
    )
    router_pad = _pad_cols(moe_router, LANES)

    cvecs = jnp.concatenate([c_ctx[None, :], c, jnp.zeros((8 - 1 - c.shape[0], D_MODEL), F32)], axis=0)
    mods = modulation_all(cvecs, w_mod, b_mod)
    n_lat = c.shape[0]

    rope_tabs = _axial_rope_tables(x_sample.shape[1])
    caches = (cache_mla_ckv, cache_mla_krope, state_rwkv, state_gla)
    bc, tc, _ = x_prompt.shape
    bl, tl, _ = x_sample.shape
    cap_c = CAP_FACTOR * tc // N_EXPERTS
    cap_l = CAP_FACTOR * tl // N_EXPERTS
    xc = x_prompt.reshape(bc * tc, D_MODEL)
    xl = x_sample.reshape(bl * tl, D_MODEL)
    outs = dict(ckv=[], kr=[], rwkv=None, gla=None)
    for l in range(DEPTH):
        mod_c = [t[:, None, :] for t in jnp.split(mods[l, 0:1], N_MOD, axis=-1)]
        mod_l = [t[:, None, :] for t in jnp.split(mods[l, 1:1 + n_lat], N_MOD, axis=-1)]
        xc = _mixer_layer(xc, l, mod_c, P, None, None, bc, tc, outs)
        xl = _mixer_layer(xl, l, mod_l, P, caches, rope_tabs, bl, tl, None)
        xc3 = xc.reshape(bc, tc, D_MODEL)
        xl3 = xl.reshape(bl, tl, D_MODEL)
        xe_c, rank_c, aff_c = moe_dispatch(xc3, g_norm_ffn[l], mod_c[4], mod_c[3], router_pad[l], cap_c)
        xe_l, rank_l, aff_l = moe_dispatch(xl3, g_norm_ffn[l], mod_l[4], mod_l[3], router_pad[l], cap_l)
        ye_c, ye_l = moe_experts(xe_c, xe_l, moe_w_gate, moe_w_up, moe_w_down, l)
        xc = moe_combine(xc3, mod_c[5], ye_c, rank_c, aff_c, cap_c).reshape(bc * tc, D_MODEL)
        xl = moe_combine(xl3, mod_l[5], ye_l, rank_l, aff_l, cap_l).reshape(bl * tl, D_MODEL)
    y_prompt = final_norm(xc, g_final).reshape(bc, tc, D_MODEL)
    y_sample = final_norm(xl, g_final).reshape(bl, tl, D_MODEL)
    return (y_prompt, y_sample, jnp.stack(outs['ckv'], axis=1), jnp.stack(outs['kr'], axis=1),
            outs['rwkv'], outs['gla'])
```
